```python
import math, functools
import jax, jax.numpy as jnp
from jax import lax
import numpy as np

D_MODEL = 2048
BATCH = 2
SEQ = 4096
DEPTH = 2
DEC_BATCH = 32
DEC_SEQ = 8
PAST_LEN = 8192
PAGE_SIZE = 128

N_HEADS = 6
HEAD_DIM = 128
N_KV_HEADS = 2
KV_GROUP = N_HEADS // N_KV_HEADS
IDX_HEADS = 4
IDX_DIM = 64
TOPK_MAX = 256
Q_BLOCK = 128
SSM_WIDTH = 768
SSM_GROUP = 16
SSM_GROUPS = SSM_WIDTH // SSM_GROUP
SSM_STATE = 64
MEM_TOKENS = 256
MEM_HEADS = 4
MEM_HEAD_DIM = 128
ATTN_W = N_HEADS * HEAD_DIM
KV_W = N_KV_HEADS * HEAD_DIM
IDXQ_W = IDX_HEADS * IDX_DIM
MEM_W = MEM_HEADS * MEM_HEAD_DIM
N_BRANCHES = 3
IN_COLS = ATTN_W + 2 * KV_W + IDXQ_W + IDX_DIM + IDX_HEADS + SSM_WIDTH + MEM_W + N_BRANCHES * D_MODEL
D_FF = ((8 * D_MODEL + 3 * 256 - 1) // (3 * 256)) * 256
EPS = 1e-6

kernel_name = 'hybrid_dsa_s5_memxattn_decoder_step'


def _rms(x, g):
    x32 = x.astype(jnp.float32)
    y = x32 * lax.rsqrt(jnp.mean(x32 * x32, axis=-1, keepdims=True) + EPS)
    return (y * g.astype(jnp.float32)).astype(x.dtype)


def _split_in(z):
    sizes = (ATTN_W, KV_W, KV_W, IDXQ_W, IDX_DIM, IDX_HEADS, SSM_WIDTH, MEM_W, N_BRANCHES * D_MODEL)
    pts, acc = [], 0
    for s in sizes[:-1]:
        acc += s
        pts.append(acc)
    return jnp.split(z, pts, axis=-1)


def _take_rows(a, idx):
    return jax.vmap(lambda ab, ib: ab[ib])(a, idx)


def _indexer_topk(qi, wi, ki, q_pos, topk):
    logits = jnp.einsum('bthd,bsd->bths', qi, ki).astype(jnp.float32) * IDX_DIM ** -0.5
    score = jnp.einsum('bths,bth->bts', jax.nn.relu(logits), wi.astype(jnp.float32))
    causal = jnp.arange(ki.shape[1])[None, None, :] <= q_pos[None, :, None]
    score = jnp.where(causal, score, -jnp.inf)
    _, idx = lax.top_k(score, topk)
    valid = idx <= q_pos[None, :, None]
    return idx, valid


def _sparse_attend(q, kg, vg, valid):
    B, T = q.shape[:2]
    qg = q.reshape(B, T, N_KV_HEADS, KV_GROUP, HEAD_DIM)
    s = jnp.einsum('btgrd,btkgd->btgrk', qg, kg).astype(jnp.float32) * HEAD_DIM ** -0.5
    s = jnp.where(valid[:, :, None, None, :], s, -jnp.inf)
    p = jax.nn.softmax(s, axis=-1).astype(vg.dtype)
    o = jnp.einsum('btgrk,btkgd->btgrd', p, vg)
    return o.reshape(B, T, ATTN_W)


def _dsa_prompt(q, k, v, qi, ki, wi):
    B, T = q.shape[:2]
    topk = min(TOPK_MAX, T // 4)

    def block(i):
        start = i * Q_BLOCK
        sl = lambda a: lax.dynamic_slice_in_dim(a, start, Q_BLOCK, axis=1)
        q_pos = start + jnp.arange(Q_BLOCK)
        idx, valid = _indexer_topk(sl(qi), sl(wi), ki, q_pos, topk)
        return _sparse_attend(sl(q), _take_rows(k, idx), _take_rows(v, idx), valid)

    out = lax.map(block, jnp.arange(T // Q_BLOCK))
    return jnp.moveaxis(out, 0, 1).reshape(B, T, ATTN_W)


def _dsa_sample(q, k, v, qi, ki, wi, ck, cv, cki, page_table):
    B, T = q.shape[:2]
    page = ck.shape[1]
    past = page_table.shape[1] * page
    ki_all = jnp.concatenate([cki[page_table].reshape(B, past, IDX_DIM), ki], axis=1)
    topk = min(TOPK_MAX, (past + T) // 4)
    q_pos = past + jnp.arange(T)
    idx, valid = _indexer_topk(qi, wi, ki_all, q_pos, topk)
    in_past = (idx < past)[..., None, None]
    pidx = jnp.minimum(idx, past - 1)
    phys = jnp.take_along_axis(page_table, (pidx // page).reshape(B, -1), axis=1).reshape(idx.shape)
    off = pidx % page
    nidx = jnp.clip(idx - past, 0, T - 1)
    kg = jnp.where(in_past, ck[phys, off], _take_rows(k, nidx))
    vg = jnp.where(in_past, cv[phys, off], _take_rows(v, nidx))
    return _sparse_attend(q, kg, vg, valid)


def _complex_affine_combine(e1, e2):
    a1r, a1i, b1r, b1i = e1
    a2r, a2i, b2r, b2i = e2
    return (a1r * a2r - a1i * a2i,
            a1r * a2i + a1i * a2r,
            a2r * b1r - a2i * b1i + b2r,
            a2r * b1i + a2i * b1r + b2i)


def _ssm_branch(u, p, s0):
    f32 = jnp.float32
    B, T = u.shape[:2]
    uf = u.astype(f32).reshape(B, T, SSM_GROUPS, SSM_GROUP)
    lr, li = p['lam_re'].astype(f32), p['lam_im'].astype(f32)
    dt = jnp.exp(p['log_dt'].astype(f32))[:, None]
    zr, zi = lr * dt, li * dt
    mag = jnp.exp(zr)
    ar, ai = mag * jnp.cos(zi), mag * jnp.sin(zi)
    den = lr * lr + li * li
    cr = ((ar - 1.0) * lr + ai * li) / den
    ci = (ai * lr - (ar - 1.0) * li) / den
    br, bi = p['b_re'].astype(f32), p['b_im'].astype(f32)
    bbr = cr[..., None] * br - ci[..., None] * bi
    bbi = cr[..., None] * bi + ci[..., None] * br
    xr = jnp.einsum('gpc,btgc->btgp', bbr, uf)
    xi = jnp.einsum('gpc,btgc->btgp', bbi, uf)
    a_r = jnp.broadcast_to(ar, xr.shape)
    a_i = jnp.broadcast_to(ai, xi.shape)
    _, _, sr, si = lax.associative_scan(_complex_affine_combine, (a_r, a_i, xr, xi), axis=1)
    if s0 is not None:
        s0r = s0[0].astype(f32)[:, None]
        s0i = s0[1].astype(f32)[:, None]
        tt = jnp.arange(1, T + 1, dtype=f32)[:, None, None]
        pm = jnp.exp(zr * tt)
        pr, pi = pm * jnp.cos(zi * tt), pm * jnp.sin(zi * tt)
        sr, si = sr + pr * s0r - pi * s0i, si + pr * s0i + pi * s0r
    y = (jnp.einsum('gcp,btgp->btgc', p['c_re'].astype(f32), sr)
         - jnp.einsum('gcp,btgp->btgc', p['c_im'].astype(f32), si)
         + p['d'].astype(f32) * uf)
    y = jax.nn.gelu(y.reshape(B, T, SSM_WIDTH)).astype(u.dtype)
    out = y * jax.nn.sigmoid(y @ p['w_glu'])
    return out, sr[:, -1].astype(u.dtype), si[:, -1].astype(u.dtype)


def _mem_kv(mem, p):
    B, M = mem.shape[:2]
    kv = _rms(mem, p['mem_norm']) @ p['w_mem_kv']
    k, v = jnp.split(kv, 2, axis=-1)
    k = _rms(k.reshape(B, M, MEM_HEADS, MEM_HEAD_DIM), p['mkn'])
    return k, v.reshape(B, M, MEM_HEADS, MEM_HEAD_DIM)


def _mem_attend(q, k, v):
    B, T = q.shape[:2]
    s = jnp.einsum('bthd,bmhd->bhtm', q, k).astype(jnp.float32) * MEM_HEAD_DIM ** -0.5
    pr = jax.nn.softmax(s, axis=-1).astype(v.dtype)
    return jnp.einsum('bhtm,bmhd->bthd', pr, v).reshape(B, T, MEM_W)


def _layer(x, p, attend, s0, mem_k, mem_v):
    B, T, _ = x.shape
    h = _rms(x, p['norm1'])
    q, k, v, qi, ki, wi, u, mq, g = _split_in(h @ p['w_in'])
    q = _rms(q.reshape(B, T, N_HEADS, HEAD_DIM), p['qn'])
    k = _rms(k.reshape(B, T, N_KV_HEADS, HEAD_DIM), p['kn'])
    v = v.reshape(B, T, N_KV_HEADS, HEAD_DIM)
    qi = qi.reshape(B, T, IDX_HEADS, IDX_DIM)
    wi = wi * IDX_HEADS ** -0.5
    a_out = attend(q, k, v, qi, ki, wi)
    s_out, sr, si = _ssm_branch(u, p, s0)
    mq = _rms(mq.reshape(B, T, MEM_HEADS, MEM_HEAD_DIM), p['mqn'])
    m_out = _mem_attend(mq, mem_k, mem_v)
    ga, gs, gm = jnp.split(jax.nn.sigmoid(g), N_BRANCHES, axis=-1)
    merged = (ga * (a_out @ p['w_br_attn']) + gs * (s_out @ p['w_br_ssm'])
              + gm * (m_out @ p['w_br_mem']))
    x = x + merged @ p['w_o']
    h2 = _rms(x, p['norm2'])
    x = x + (jax.nn.silu(h2 @ p['w_ff1']) * (h2 @ p['w_ff3'])) @ p['w_ff2']
    return x, k, v, ki, sr, si


def setup_inputs(seed: int = 0) -> dict:
    key = jax.random.key(seed)
    keys = iter(jax.random.split(key, 48))
    f32 = jnp.float32

    def nrm(shape, scale):
        return jax.random.normal(next(keys), shape, f32) * scale

    def gain(shape):
        return 1.0 + nrm(shape, 0.02)

    n_pages = PAST_LEN // PAGE_SIZE
    n_pool = (5 * DEC_BATCH * n_pages + 3) // 4
    page_table = jax.random.permutation(next(keys), n_pool)[:DEC_BATCH * n_pages]
    page_table = page_table.reshape(DEC_BATCH, n_pages).astype(jnp.int32)
    G, P, C = SSM_GROUPS, SSM_STATE, SSM_GROUP
    lam_im0 = jnp.broadcast_to(jnp.pi * jnp.arange(P, dtype=f32), (DEPTH, G, P))
    return {
        'x_prompt': nrm((BATCH, SEQ, D_MODEL), 1.0),
        'x_sample': nrm((DEC_BATCH, DEC_SEQ, D_MODEL), 1.0),
        'mem_prompt': nrm((BATCH, MEM_TOKENS, D_MODEL), 1.0),
        'cache_k': nrm((DEPTH, n_pool, PAGE_SIZE, N_KV_HEADS, HEAD_DIM), 1.0),
        'cache_v': nrm((DEPTH, n_pool, PAGE_SIZE, N_KV_HEADS, HEAD_DIM), 1.0),
        'cache_kidx': nrm((DEPTH, n_pool, PAGE_SIZE, IDX_DIM), 1.0),
        'cache_mem_k': nrm((DEPTH, DEC_BATCH, MEM_TOKENS, MEM_HEADS, MEM_HEAD_DIM), 1.0),
        'cache_mem_v': nrm((DEPTH, DEC_BATCH, MEM_TOKENS, MEM_HEADS, MEM_HEAD_DIM), 1.0),
        'state_ssm_re': nrm((DEPTH, DEC_BATCH, G, P), 0.1),
        'state_ssm_im': nrm((DEPTH, DEC_BATCH, G, P), 0.1),
        'page_table': page_table,
        'norm1_g': gain((DEPTH, D_MODEL)),
        'w_in': nrm((DEPTH, D_MODEL, IN_COLS), D_MODEL ** -0.5),
        'q_norm_g': gain((DEPTH, HEAD_DIM)),
        'k_norm_g': gain((DEPTH, HEAD_DIM)),
        'ssm_lam_re': -0.5 + nrm((DEPTH, G, P), 0.01),
        'ssm_lam_im': lam_im0 + nrm((DEPTH, G, P), 0.01),
        'ssm_log_dt': jax.random.uniform(next(keys), (DEPTH, G), f32, math.log(1e-3), math.log(1e-1)),
        'ssm_b_re': nrm((DEPTH, G, P, C), (2 * C) ** -0.5),
        'ssm_b_im': nrm((DEPTH, G, P, C), (2 * C) ** -0.5),
        'ssm_c_re': nrm((DEPTH, G, C, P), P ** -0.5),
        'ssm_c_im': nrm((DEPTH, G, C, P), P ** -0.5),
        'ssm_d': nrm((DEPTH, G, C), 1.0),
        'w_glu': nrm((DEPTH, SSM_WIDTH, SSM_WIDTH), SSM_WIDTH ** -0.5),
        'mem_norm_g': gain((DEPTH, D_MODEL)),
        'w_mem_kv': nrm((DEPTH, D_MODEL, 2 * MEM_W), D_MODEL ** -0.5),
        'mq_norm_g': gain((DEPTH, MEM_HEAD_DIM)),
        'mk_norm_g': gain((DEPTH, MEM_HEAD_DIM)),
        'w_br_attn': nrm((DEPTH, ATTN_W, D_MODEL), ATTN_W ** -0.5),
        'w_br_ssm': nrm((DEPTH, SSM_WIDTH, D_MODEL), SSM_WIDTH ** -0.5),
        'w_br_mem': nrm((DEPTH, MEM_W, D_MODEL), MEM_W ** -0.5),
        'w_o': nrm((DEPTH, D_MODEL, D_MODEL), D_MODEL ** -0.5),
        'norm2_g': gain((DEPTH, D_MODEL)),
        'w_ff1': nrm((DEPTH, D_MODEL, D_FF), D_MODEL ** -0.5),
        'w_ff3': nrm((DEPTH, D_MODEL, D_FF), D_MODEL ** -0.5),
        'w_ff2': nrm((DEPTH, D_FF, D_MODEL), D_FF ** -0.5),
    }


def reference(x_prompt, x_sample, mem_prompt, cache_k, cache_v, cache_kidx, cache_mem_k, cache_mem_v,
              state_ssm_re, state_ssm_im, page_table, norm1_g, w_in, q_norm_g, k_norm_g,
              ssm_lam_re, ssm_lam_im, ssm_log_dt, ssm_b_re, ssm_b_im, ssm_c_re, ssm_c_im, ssm_d, w_glu,
              mem_norm_g, w_mem_kv, mq_norm_g, mk_norm_g, w_br_attn, w_br_ssm, w_br_mem, w_o,
              norm2_g, w_ff1, w_ff3, w_ff2):
    y_p, y_s = x_prompt, x_sample
    kp_l, vp_l, kip_l, mkp_l, mvp_l, srp_l, sip_l = [], [], [], [], [], [], []
    ks_l, vs_l, kis_l, srs_l, sis_l = [], [], [], [], []
    for l in range(DEPTH):
        p = {
            'norm1': norm1_g[l], 'w_in': w_in[l], 'qn': q_norm_g[l], 'kn': k_norm_g[l],
            'lam_re': ssm_lam_re[l], 'lam_im': ssm_lam_im[l], 'log_dt': ssm_log_dt[l],
            'b_re': ssm_b_re[l], 'b_im': ssm_b_im[l], 'c_re': ssm_c_re[l], 'c_im': ssm_c_im[l],
            'd': ssm_d[l], 'w_glu': w_glu[l],
            'mem_norm': mem_norm_g[l], 'w_mem_kv': w_mem_kv[l], 'mqn': mq_norm_g[l], 'mkn': mk_norm_g[l],
            'w_br_attn': w_br_attn[l], 'w_br_ssm': w_br_ssm[l], 'w_br_mem': w_br_mem[l], 'w_o': w_o[l],
            'norm2': norm2_g[l], 'w_ff1': w_ff1[l], 'w_ff3': w_ff3[l], 'w_ff2': w_ff2[l],
        }
        mk_p, mv_p = _mem_kv(mem_prompt, p)
        y_p, k_p, v_p, ki_p, sr_p, si_p = _layer(y_p, p, _dsa_prompt, None, mk_p, mv_p)
        attend_s = functools.partial(_dsa_sample, ck=cache_k[l], cv=cache_v[l], cki=cache_kidx[l],
                                     page_table=page_table)
        y_s, k_s, v_s, ki_s, sr_s, si_s = _layer(y_s, p, attend_s, (state_ssm_re[l], state_ssm_im[l]),
                                                 cache_mem_k[l], cache_mem_v[l])
        kp_l.append(k_p); vp_l.append(v_p); kip_l.append(ki_p); mkp_l.append(mk_p); mvp_l.append(mv_p)
        srp_l.append(sr_p); sip_l.append(si_p)
        ks_l.append(k_s); vs_l.append(v_s); kis_l.append(ki_s); srs_l.append(sr_s); sis_l.append(si_s)
    st = lambda xs: jnp.stack(xs, axis=0)
    return (y_p, y_s, st(kp_l), st(vp_l), st(kip_l), st(mkp_l), st(mvp_l), st(srp_l), st(sip_l),
            st(ks_l), st(vs_l), st(kis_l), st(srs_l), st(sis_l))
```

```python
import functools
import math

import jax
import jax.numpy as jnp
from jax import lax
from jax.experimental import pallas as pl
from jax.experimental.pallas import tpu as pltpu

F32 = jnp.float32
BF16 = jnp.bfloat16
I32 = jnp.int32

LANES = 128
EPS = 1e-6
HEAD_DIM = 128
N_HEADS = 6
N_KV_HEADS = 2
KV_GROUP = N_HEADS // N_KV_HEADS
IDX_HEADS = 4
IDX_DIM = 64
TOPK_MAX = 256
SSM_GROUP = 16
SSM_STATE = 64
SSM_CHUNK = 8
MEM_HEADS = 4
KEY_MIN = -2 ** 31
NEG_BIAS = -1e30
M_INIT = -1e29
VMEM_LIMIT = 56 * 1024 * 1024

C_Q, C_K, C_V, C_QI, C_MQ, C_KI, C_WI, C_U = 0, 768, 1024, 1280, 1536, 2048, 2112, 2176
Z_COLS = 3072


def _cparams(sem):
    return pltpu.CompilerParams(dimension_semantics=sem, vmem_limit_bytes=VMEM_LIMIT)


BF16_SUBLANES = 16
MAX_ROW_TILE = 768


def _row_tile(m):
    return max(t for t in range(BF16_SUBLANES, MAX_ROW_TILE + 1, BF16_SUBLANES) if m % t == 0)


def _norm_matmul_kernel(x_ref, g_ref, w_ref, flag_ref, gain_ref, z_ref, *rest, emit_h):
    if emit_h:
        h_ref, hs_ref = rest
    else:
        (hs_ref,) = rest
    j = pl.program_id(1)

    @pl.when(j == 0)
    def _():
        x = x_ref[...]
        r = lax.rsqrt(jnp.mean(x * x, axis=-1, keepdims=True) + EPS)
        h = ((x * r) * g_ref[...]).astype(BF16)
        hs_ref[...] = h
        if emit_h:
            h_ref[...] = h

    acc = jnp.dot(hs_ref[...], w_ref[...], preferred_element_type=F32)
    for c in range(acc.shape[1] // LANES):
        sl = slice(c * LANES, (c + 1) * LANES)
        y = acc[:, sl]
        r = lax.rsqrt(jnp.mean(y * y, axis=-1, keepdims=True) + EPS)
        scale = jnp.where(flag_ref[:, sl] > 0.0, r, 1.0) * gain_ref[:, sl]
        z_ref[:, sl] = y * scale


def _norm_matmul(x, g, w, flag, gain, *, tm, tn, emit_h):
    m, d = x.shape
    n = w.shape[1]
    out_shape = [jax.ShapeDtypeStruct((m, n), F32)]
    out_specs = [pl.BlockSpec((tm, tn), lambda i, j: (i, j))]
    if emit_h:
        out_shape.append(jax.ShapeDtypeStruct((m, d), BF16))
        out_specs.append(pl.BlockSpec((tm, d), lambda i, j: (i, 0)))
    res = pl.pallas_call(
        functools.partial(_norm_matmul_kernel, emit_h=emit_h),
        grid=(m // tm, n // tn),
        in_specs=[
            pl.BlockSpec((tm, d), lambda i, j: (i, 0)),
            pl.BlockSpec((1, d), lambda i, j: (0, 0)),
            pl.BlockSpec((d, tn), lambda i, j: (0, j)),
            pl.BlockSpec((1, tn), lambda i, j: (0, j)),
            pl.BlockSpec((1, tn), lambda i, j: (0, j)),
        ],
        out_specs=out_specs,
        out_shape=out_shape,
        scratch_shapes=[pltpu.VMEM((tm, d), BF16)],
        compiler_params=_cparams(("arbitrary", "arbitrary")),
        name="norm_matmul",
    )(x, g, w, flag, gain)
    return res if emit_h else res[0]


def _score_keys(score, masked):
    bits = pltpu.bitcast(score, I32)
    key = bits ^ ((bits >> 31) & 0x7FFFFFFF)
    return jnp.where(masked, KEY_MIN, key)


def _count_ge(key_ref, n128, cand):
    rows = key_ref.shape[0]

    def body(c, acc):
        off = pl.multiple_of(c * LANES, LANES)
        return acc + jnp.where(key_ref[:, pl.ds(off, LANES)] >= cand, 1.0, 0.0)

    acc = lax.fori_loop(0, n128, body, jnp.zeros((rows, LANES), F32))
    return jnp.sum(acc, axis=1, keepdims=True)


def _kth_largest_key(key_ref, n128, k):
    rows = key_ref.shape[0]
    nonneg = _count_ge(key_ref, n128, jnp.zeros((rows, 1), I32)) >= k
    lo = jnp.where(nonneg, 0, KEY_MIN)

    def body(it, lo):
        cand = lo | (1 << (30 - it))
        return jnp.where(_count_ge(key_ref, n128, cand) >= k, cand, lo)

    return lax.fori_loop(0, 31, body, lo)


def _selection_bias(key_ref, bias_ref, n128, k, thr):
    rows = key_ref.shape[0]
    need = k - _count_ge(key_ref, n128, thr + 1)
    tri = (lax.broadcasted_iota(I32, (LANES, LANES), 0)
           <= lax.broadcasted_iota(I32, (LANES, LANES), 1)).astype(BF16)

    def body(c, seen):
        off = pl.multiple_of(c * LANES, LANES)
        key = key_ref[:, pl.ds(off, LANES)]
        eq = key == thr
        rank = seen + jnp.dot(jnp.where(eq, 1.0, 0.0).astype(BF16), tri, preferred_element_type=F32)
        sel = (key > thr) | (eq & (rank <= need) & (key != KEY_MIN))
        bias_ref[:, pl.ds(off, LANES)] = jnp.where(sel, 0.0, NEG_BIAS)
        return seen + jnp.sum(jnp.where(eq, 1.0, 0.0), axis=1, keepdims=True)

    lax.fori_loop(0, n128, body, jnp.zeros((rows, 1), F32))


def _flash_step(q, k_c, v_c, bias, carry):
    m, l, acc = carry
    s = lax.dot_general(q, k_c, (((1,), (1,)), ((), ())), preferred_element_type=F32) + bias
    m_new = jnp.maximum(m, jnp.max(s, axis=1, keepdims=True))
    alpha = jnp.exp(m - m_new)
    p = jnp.exp(s - m_new)
    l = alpha * l + jnp.sum(p, axis=1, keepdims=True)
    acc = alpha * acc + jnp.dot(p.astype(BF16), v_c, preferred_element_type=F32)
    return m_new, l, acc


def _flash_init(rows):
    return (jnp.full((rows, 1), M_INIT, F32), jnp.zeros((rows, 1), F32), jnp.zeros((rows, HEAD_DIM), F32))


def _stack_heads(x, first, count, width):
    return jnp.concatenate([x[:, (first + r) * width:(first + r + 1) * width] for r in range(count)], axis=0)


def _indexer_scores(qi_stack, w_cols, ki_c, rows):
    logits = lax.dot_general(qi_stack, ki_c, (((1,), (1,)), ((), ())), preferred_element_type=F32)
    score = jnp.maximum(logits[0:rows], 0.0) * w_cols[0]
    for h in range(1, IDX_HEADS):
        score = score + jnp.maximum(logits[h * rows:(h + 1) * rows], 0.0) * w_cols[h]
    return score


TQ = 128
KC = 512


def _dsa_prompt_kernel(q_ref, qi_ref, kiwq_ref, k_ref, v_ref, kiw_ref, o_ref,
                       kbf_ref, vbf_ref, kibf_ref, key_ref, bias_ref, *, topk):
    i = pl.program_id(1)

    @pl.when(i == 0)
    def _():
        kbf_ref[...] = k_ref[...].astype(BF16)
        vbf_ref[...] = v_ref[...].astype(BF16)
        kibf_ref[...] = kiw_ref[...].astype(BF16)

    n_kc = (i * TQ + TQ + KC - 1) // KC
    n128 = n_kc * (KC // LANES)
    qi_stack = _stack_heads(qi_ref[...], 0, IDX_HEADS, IDX_DIM).astype(BF16)
    kiw = kiwq_ref[...]
    scale_i = IDX_DIM ** -0.5
    w_cols = [kiw[:, IDX_DIM + h:IDX_DIM + h + 1] * scale_i for h in range(IDX_HEADS)]
    qpos = i * TQ + lax.broadcasted_iota(I32, (TQ, KC), 0)

    def score_body(c, _):
        off = pl.multiple_of(c * KC, KC)
        ki_c = kibf_ref[pl.ds(off, KC), 0:IDX_DIM]
        score = _indexer_scores(qi_stack, w_cols, ki_c, TQ)
        kpos = off + lax.broadcasted_iota(I32, (TQ, KC), 1)
        key_ref[:, pl.ds(off, KC)] = _score_keys(score, kpos > qpos)
        return 0

    lax.fori_loop(0, n_kc, score_body, 0)
    thr = _kth_largest_key(key_ref, n128, topk)
    _selection_bias(key_ref, bias_ref, n128, topk, thr)

    q = q_ref[...] * (HEAD_DIM ** -0.5)
    for g in range(N_KV_HEADS):
        q_g = _stack_heads(q, g * KV_GROUP, KV_GROUP, HEAD_DIM).astype(BF16)
        gs = slice(g * HEAD_DIM, (g + 1) * HEAD_DIM)

        def att_body(c, carry):
            off = pl.multiple_of(c * KC, KC)
            b = bias_ref[:, pl.ds(off, KC)]
            bias = jnp.concatenate([b] * KV_GROUP, axis=0)
            return _flash_step(q_g, kbf_ref[pl.ds(off, KC), gs], vbf_ref[pl.ds(off, KC), gs], bias, carry)

        _, l, acc = lax.fori_loop(0, n_kc, att_body, _flash_init(KV_GROUP * TQ))
        o = acc / l
        for r in range(KV_GROUP):
            hh = g * KV_GROUP + r
            o_ref[:, hh * HEAD_DIM:(hh + 1) * HEAD_DIM] = o[r * TQ:(r + 1) * TQ]


def _dsa_prompt(z, batch, seq):
    nq = seq // TQ
    topk = min(TOPK_MAX, seq // 4)
    return pl.pallas_call(
        functools.partial(_dsa_prompt_kernel, topk=topk),
        grid=(batch, nq),
        in_specs=[
            pl.BlockSpec((TQ, N_HEADS * HEAD_DIM), lambda b, i: (b * nq + i, C_Q // (N_HEADS * HEAD_DIM))),
            pl.BlockSpec((TQ, IDX_HEADS * IDX_DIM), lambda b, i: (b * nq + i, C_QI // (IDX_HEADS * IDX_DIM))),
            pl.BlockSpec((TQ, LANES), lambda b, i: (b * nq + i, C_KI // LANES)),
            pl.BlockSpec((seq, N_KV_HEADS * HEAD_DIM), lambda b, i: (b, C_K // (N_KV_HEADS * HEAD_DIM))),
            pl.BlockSpec((seq, N_KV_HEADS * HEAD_DIM), lambda b, i: (b, C_V // (N_KV_HEADS * HEAD_DIM))),
            pl.BlockSpec((seq, LANES), lambda b, i: (b, C_KI // LANES)),
        ],
        out_specs=pl.BlockSpec((TQ, N_HEADS * HEAD_DIM), lambda b, i: (b * nq + i, 0)),
        out_shape=jax.ShapeDtypeStruct((batch * seq, N_HEADS * HEAD_DIM), F32),
        scratch_shapes=[
            pltpu.VMEM((seq, N_KV_HEADS * HEAD_DIM), BF16),
            pltpu.VMEM((seq, N_KV_HEADS * HEAD_DIM), BF16),
            pltpu.VMEM((seq, LANES), BF16),
            pltpu.VMEM((TQ, seq), I32),
            pltpu.VMEM((TQ, seq), F32),
        ],
        compiler_params=_cparams(("arbitrary", "arbitrary")),
        name="dsa_prompt",
    )(z, z, z, z, z, z)


def _dsa_sample_kernel(pt_ref, q_ref, qi_ref, kiw_ref, kn_ref, vn_ref, ck_ref, cv_ref, cki_ref, o_ref,
                       kbuf, vbuf, kibuf, key_ref, bias_ref, sem, *, n_pages, page, tq, topk):
    b = pl.program_id(0)
    nb = pl.num_programs(0)
    past = n_pages * page
    slot = b % 2

    streams = ((ck_ref, kbuf), (cv_ref, vbuf), (cki_ref, kibuf))

    def page_copy(which, bb, sl, p):
        src, dst = streams[which]
        return pltpu.make_async_copy(src.at[pt_ref[bb * n_pages + p]], dst.at[sl, pl.ds(p * page, page)],
                                     sem.at[which, sl])

    def start_fetch(bb, sl):
        def body(p, _):
            for which in range(len(streams)):
                page_copy(which, bb, sl, p).start()
            return 0
        lax.fori_loop(0, n_pages, body, 0)

    def wait_fetch(bb, sl, which):
        def body(p, _):
            page_copy(which, bb, sl, p).wait()
            return 0
        lax.fori_loop(0, n_pages, body, 0)

    @pl.when(b == 0)
    def _():
        start_fetch(0, 0)

    @pl.when(b + 1 < nb)
    def _():
        start_fetch(b + 1, 1 - slot)

    qi_stack = _stack_heads(qi_ref[...], 0, IDX_HEADS, IDX_DIM).astype(BF16)
    kiw = kiw_ref[...]
    scale_i = IDX_DIM ** -0.5
    w_cols = [kiw[:, IDX_DIM + h:IDX_DIM + h + 1] * scale_i for h in range(IDX_HEADS)]
    n_kc = past // KC
    n128 = past // LANES + 1

    wait_fetch(b, slot, 2)

    def score_body(c, _):
        off = pl.multiple_of(c * KC, KC)
        ki_c = kibuf[slot, pl.ds(off, KC), :].astype(BF16)
        score = _indexer_scores(qi_stack, w_cols, ki_c, tq)
        key_ref[:, pl.ds(off, KC)] = _score_keys(score, jnp.zeros((tq, KC), jnp.bool_))
        return 0

    lax.fori_loop(0, n_kc, score_body, 0)
    pad = jnp.zeros((LANES - tq, LANES), F32)
    ki_new = jnp.concatenate([kiw, pad], axis=0)[:, 0:IDX_DIM].astype(BF16)
    score_new = _indexer_scores(qi_stack, w_cols, ki_new, tq)
    newer = lax.broadcasted_iota(I32, (tq, LANES), 1) > lax.broadcasted_iota(I32, (tq, LANES), 0)
    key_ref[:, past:past + LANES] = _score_keys(score_new, newer)

    thr = _kth_largest_key(key_ref, n128, topk)
    _selection_bias(key_ref, bias_ref, n128, topk, thr)

    wait_fetch(b, slot, 0)
    wait_fetch(b, slot, 1)
    q = q_ref[...] * (HEAD_DIM ** -0.5)
    kv_pad = jnp.zeros((LANES - tq, N_KV_HEADS * HEAD_DIM), F32)
    k_new = jnp.concatenate([kn_ref[...], kv_pad], axis=0).astype(BF16)
    v_new = jnp.concatenate([vn_ref[...], kv_pad], axis=0).astype(BF16)
    bias_new = jnp.concatenate([bias_ref[:, past:past + LANES]] * KV_GROUP, axis=0)
    for g in range(N_KV_HEADS):
        q_g = _stack_heads(q, g * KV_GROUP, KV_GROUP, HEAD_DIM).astype(BF16)
        gs = slice(g * HEAD_DIM, (g + 1) * HEAD_DIM)

        def att_body(c, carry):
            off = pl.multiple_of(c * KC, KC)
            bias = jnp.concatenate([bias_ref[:, pl.ds(off, KC)]] * KV_GROUP, axis=0)
            k_c = kbuf[slot, pl.ds(off, KC), gs].astype(BF16)
            v_c = vbuf[slot, pl.ds(off, KC), gs].astype(BF16)
            return _flash_step(q_g, k_c, v_c, bias, carry)

        carry = lax.fori_loop(0, n_kc, att_body, _flash_init(KV_GROUP * tq))
        _, l, acc = _flash_step(q_g, k_new[:, gs], v_new[:, gs], bias_new, carry)
        o = acc / l
        for r in range(KV_GROUP):
            hh = g * KV_GROUP + r
            o_ref[:, hh * HEAD_DIM:(hh + 1) * HEAD_DIM] = o[r * tq:(r + 1) * tq]


def _dsa_sample(z, row0, nb, tq, ck, cv, cki, page_table):
    n_pool, page = ck.shape[0], ck.shape[1]
    n_pages = page_table.shape[1]
    past = n_pages * page
    topk = min(TOPK_MAX, (past + tq) // 4)
    kvw = N_KV_HEADS * HEAD_DIM
    rb = row0 // tq
    grid_spec = pltpu.PrefetchScalarGridSpec(
        num_scalar_prefetch=1,
        grid=(nb,),
        in_specs=[
            pl.BlockSpec((tq, N_HEADS * HEAD_DIM), lambda b, pt: (rb + b, C_Q // (N_HEADS * HEAD_DIM))),
            pl.BlockSpec((tq, IDX_HEADS * IDX_DIM), lambda b, pt: (rb + b, C_QI // (IDX_HEADS * IDX_DIM))),
            pl.BlockSpec((tq, LANES), lambda b, pt: (rb + b, C_KI // LANES)),
            pl.BlockSpec((tq, kvw), lambda b, pt: (rb + b, C_K // kvw)),
            pl.BlockSpec((tq, kvw), lambda b, pt: (rb + b, C_V // kvw)),
            pl.BlockSpec(memory_space=pl.ANY),
            pl.BlockSpec(memory_space=pl.ANY),
            pl.BlockSpec(memory_space=pl.ANY),
        ],
        out_specs=pl.BlockSpec((tq, N_HEADS * HEAD_DIM), lambda b, pt: (b, 0)),
        scratch_shapes=[
            pltpu.VMEM((2, past, kvw), F32),
            pltpu.VMEM((2, past, kvw), F32),
            pltpu.VMEM((2, past, IDX_DIM), F32),
            pltpu.VMEM((tq, past + LANES), I32),
            pltpu.VMEM((tq, past + LANES), F32),
            pltpu.SemaphoreType.DMA((3, 2)),
        ],
    )
    return pl.pallas_call(
        functools.partial(_dsa_sample_kernel, n_pages=n_pages, page=page, tq=tq, topk=topk),
        grid_spec=grid_spec,
        out_shape=jax.ShapeDtypeStruct((nb * tq, N_HEADS * HEAD_DIM), F32),
        compiler_params=_cparams(("arbitrary",)),
        name="dsa_sample",
    )(page_table.reshape(-1), z, z, z, z, z,
      ck.reshape(n_pool, page, kvw), cv.reshape(n_pool, page, kvw), cki)


def _mem_attn_kernel(q_ref, k_ref, v_ref, o_ref):
    q = q_ref[...] * (HEAD_DIM ** -0.5)
    for h in range(MEM_HEADS):
        hs = slice(h * HEAD_DIM, (h + 1) * HEAD_DIM)
        s = lax.dot_general(q[:, hs].astype(BF16), k_ref[:, hs].astype(BF16),
                            (((1,), (1,)), ((), ())), preferred_element_type=F32)
        m = jnp.max(s, axis=1, keepdims=True)
        p = jnp.exp(s - m)
        l = jnp.sum(p, axis=1, keepdims=True)
        o = jnp.dot(p.astype(BF16), v_ref[:, hs].astype(BF16), preferred_element_type=F32)
        o_ref[:, hs] = o / l


def _mem_attn(z, row0, nb, t, tq, mk, mv, kcol, vcol):
    mw = MEM_HEADS * HEAD_DIM
    nq = t // tq
    rb = row0 // tq
    m_tok = mk.shape[0] // nb
    return pl.pallas_call(
        _mem_attn_kernel,
        grid=(nb, nq),
        in_specs=[
            pl.BlockSpec((tq, mw), lambda b, i: (rb + b * nq + i, C_MQ // mw)),
            pl.BlockSpec((m_tok, mw), lambda b, i: (b, kcol)),
            pl.BlockSpec((m_tok, mw), lambda b, i: (b, vcol)),
        ],
        out_specs=pl.BlockSpec((tq, mw), lambda b, i: (b * nq + i, 0)),
        out_shape=jax.ShapeDtypeStruct((nb * t, mw), F32),
        compiler_params=_cparams(("arbitrary", "arbitrary")),
        name="mem_attn",
    )(z, mk, mv)


def _shift_rows(x, sh):
    rows = lax.broadcasted_iota(I32, x.shape, 0)
    return jnp.where(rows >= sh, pltpu.roll(x, sh, axis=0), 0.0)


def _ssm_kernel(u_ref, m_ref, wx_ref, wc_ref, pr_ref, pi_ref, s0_ref, y_ref, sf_ref, *, nb, nc):
    u = u_ref[0].astype(BF16)
    y_in = jnp.dot(u, m_ref[0], preferred_element_type=F32)
    x = jnp.dot(u, wx_ref[0], preferred_element_type=F32)
    xr, xi = x[:, :LANES], x[:, LANES:]
    pr, pi = pr_ref[0], pi_ref[0]
    s0 = s0_ref[0]
    s0r, s0i = s0[:, :LANES], s0[:, LANES:]
    ar, ai = pr[0:1], pi[0:1]
    if nc == 1:
        sr = xr + ar * s0r - ai * s0i
        si = xi + ar * s0i + ai * s0r
        pvr, pvi = s0r, s0i
        sf_ref[0] = jnp.concatenate([sr, si], axis=1)
    else:
        prev_r, prev_i = [], []
        for b in range(nb):
            rs = slice(b * nc, (b + 1) * nc)
            b0r, b0i = s0r[b:b + 1], s0i[b:b + 1]
            first = lax.broadcasted_iota(I32, (nc, LANES), 0) == 0
            sr = xr[rs] + jnp.where(first, ar * b0r - ai * b0i, 0.0)
            si = xi[rs] + jnp.where(first, ar * b0i + ai * b0r, 0.0)
            for mstep in range(int(math.log2(nc))):
                sh = 2 ** mstep
                qr, qi = pr[mstep:mstep + 1], pi[mstep:mstep + 1]
                tr, ti = _shift_rows(sr, sh), _shift_rows(si, sh)
                sr, si = sr + qr * tr - qi * ti, si + qr * ti + qi * tr
            sf_ref[0, b:b + 1, :] = jnp.concatenate([sr[nc - 1:nc], si[nc - 1:nc]], axis=1)
            prev_r.append(jnp.where(first, b0r, pltpu.roll(sr, 1, axis=0)))
            prev_i.append(jnp.where(first, b0i, pltpu.roll(si, 1, axis=0)))
        pvr = jnp.concatenate(prev_r, axis=0)
        pvi = jnp.concatenate(prev_i, axis=0)
    prev = jnp.concatenate([pvr, pvi], axis=1).astype(BF16)
    y_ref[0] = y_in + jnp.dot(prev, wc_ref[0], preferred_element_type=F32)


def _ssm(u_pairs, mats, s0_pairs, nb, nc):
    m_mat, wx, wc, pr, pi = mats
    npair, rows, w = u_pairs.shape
    pw = pr.shape[1]
    blk = lambda r: pl.BlockSpec((1, r, w), lambda p: (p, 0, 0))
    return pl.pallas_call(
        functools.partial(_ssm_kernel, nb=nb, nc=nc),
        grid=(npair,),
        in_specs=[blk(rows), blk(w), blk(w), blk(w),
                  pl.BlockSpec((1, pw, LANES), lambda p: (p, 0, 0)),
                  pl.BlockSpec((1, pw, LANES), lambda p: (p, 0, 0)),
                  blk(nb)],
        out_specs=[blk(rows), blk(nb)],
        out_shape=[jax.ShapeDtypeStruct((npair, rows, w), F32), jax.ShapeDtypeStruct((npair, nb, w), F32)],
        compiler_params=_cparams(("arbitrary",)),
        name="ssm",
    )(u_pairs, m_mat, wx, wc, pr, pi, s0_pairs)


def _ssm_matrices(lam_re, lam_im, log_dt, b_re, b_im, c_re, c_im, d_skip, n_steps):
    g, p, c = b_re.shape
    L = SSM_CHUNK
    dt = jnp.exp(log_dt)[:, None]
    zr, zi = lam_re * dt, lam_im * dt
    mag = jnp.exp(zr)
    ar, ai = mag * jnp.cos(zi), mag * jnp.sin(zi)
    den = lam_re * lam_re + lam_im * lam_im
    cr = ((ar - 1.0) * lam_re + ai * lam_im) / den
    ci = (ai * lam_re - (ar - 1.0) * lam_im) / den
    bbr = cr[..., None] * b_re - ci[..., None] * b_im
    bbi = cr[..., None] * b_im + ci[..., None] * b_re
    n = jnp.arange(L + 1, dtype=F32)[:, None, None]
    pm = jnp.exp(zr[None] * n)
    pwr, pwi = pm * jnp.cos(zi[None] * n), pm * jnp.sin(zi[None] * n)
    car = c_re[None] * pwr[:, :, None, :] - c_im[None] * pwi[:, :, None, :]
    cai = c_re[None] * pwi[:, :, None, :] + c_im[None] * pwr[:, :, None, :]
    kern = jnp.einsum('ngop,gpc->ngoc', car, bbr) - jnp.einsum('ngop,gpc->ngoc', cai, bbi)
    lag = jnp.arange(L)[None, :] - jnp.arange(L)[:, None]
    toe = jnp.where((lag >= 0)[:, :, None, None, None], kern[jnp.clip(lag, 0, L)], 0.0)
    eye = (lag == 0)[:, :, None, None, None] * (jnp.eye(c)[None, None, None] * d_skip[None, None, :, :, None])
    toe = toe + eye
    m_g = jnp.transpose(toe, (2, 0, 4, 1, 3)).reshape(g, L * c, L * c)
    rev = pwr[L - 1 - jnp.arange(L)], pwi[L - 1 - jnp.arange(L)]
    wxr = rev[0][..., None] * bbr[None] - rev[1][..., None] * bbi[None]
    wxi = rev[0][..., None] * bbi[None] + rev[1][..., None] * bbr[None]
    wxr = jnp.transpose(wxr, (1, 0, 3, 2)).reshape(g, L * c, p)
    wxi = jnp.transpose(wxi, (1, 0, 3, 2)).reshape(g, L * c, p)
    wcr = jnp.transpose(car[1:], (1, 3, 0, 2)).reshape(g, p, L * c)
    wci = -jnp.transpose(cai[1:], (1, 3, 0, 2)).reshape(g, p, L * c)

    def pair_blockdiag(a):
        a = a.reshape(g // 2, 2, a.shape[1], a.shape[2])
        z = jnp.zeros_like(a[:, 0])
        return jnp.concatenate([jnp.concatenate([a[:, 0], z], axis=2),
                                jnp.concatenate([z, a[:, 1]], axis=2)], axis=1)

    m_pair = pair_blockdiag(m_g)
    wx_pair = jnp.concatenate([pair_blockdiag(wxr), pair_blockdiag(wxi)], axis=2)
    wc_pair = jnp.concatenate([pair_blockdiag(wcr), pair_blockdiag(wci)], axis=1)
    qr, qi = [pwr[L]], [pwi[L]]
    for _ in range(n_steps - 1):
        qr, qi = qr + [qr[-1] * qr[-1] - qi[-1] * qi[-1]], qi + [2.0 * qr[-1] * qi[-1]]
    rows = max(8, -(-n_steps // 8) * 8)
    padz = [jnp.zeros_like(qr[0])] * (rows - n_steps)
    pr = jnp.transpose(jnp.stack(qr + padz), (1, 0, 2)).reshape(g // 2, 2, rows, p)
    pi = jnp.transpose(jnp.stack(qi + padz), (1, 0, 2)).reshape(g // 2, 2, rows, p)
    pr = jnp.concatenate([pr[:, 0], pr[:, 1]], axis=2)
    pi = jnp.concatenate([pi[:, 0], pi[:, 1]], axis=2)
    return m_pair.astype(BF16), wx_pair.astype(BF16), wc_pair.astype(BF16), pr, pi


def _to_pairs(u, nb, t):
    w = u.shape[1]
    npair = w // (2 * SSM_GROUP)
    x = u.reshape(nb, t // SSM_CHUNK, SSM_CHUNK, npair, 2, SSM_GROUP)
    x = jnp.transpose(x, (3, 0, 1, 4, 2, 5))
    return x.reshape(npair, nb * (t // SSM_CHUNK), 2 * SSM_CHUNK * SSM_GROUP)


def _from_pairs(y, nb, t):
    npair = y.shape[0]
    x = y.reshape(npair, nb, t // SSM_CHUNK, 2, SSM_CHUNK, SSM_GROUP)
    x = jnp.transpose(x, (1, 2, 4, 0, 3, 5))
    return x.reshape(nb * t, npair * 2 * SSM_GROUP)


def _state_to_pairs(sr, si):
    nb, g, p = sr.shape
    f = lambda s: jnp.transpose(s.reshape(nb, g // 2, 2 * p), (1, 0, 2))
    return jnp.concatenate([f(sr), f(si)], axis=2)


def _state_from_pairs(s):
    npair, nb, w = s.shape
    f = lambda a: jnp.transpose(a, (1, 0, 2)).reshape(nb, npair * 2, w // 4)
    return f(s[:, :, :w // 2]), f(s[:, :, w // 2:])


def _glu_kernel(y_ref, w_ref, o_ref):
    y = jax.nn.gelu(y_ref[...])
    gate = jnp.dot(y.astype(BF16), w_ref[...], preferred_element_type=F32)
    o_ref[...] = y * jax.nn.sigmoid(gate)


def _glu(y, w, tm):
    m, n = y.shape
    return pl.pallas_call(
        _glu_kernel,
        grid=(m // tm,),
        in_specs=[pl.BlockSpec((tm, n), lambda i: (i, 0)), pl.BlockSpec((n, n), lambda i: (0, 0))],
        out_specs=pl.BlockSpec((tm, n), lambda i: (i, 0)),
        out_shape=jax.ShapeDtypeStruct((m, n), F32),
        compiler_params=_cparams(("arbitrary",)),
        name="glu",
    )(y, w)


def _merge_kernel(h_ref, a_ref, s_ref, m_ref, wga_ref, wgs_ref, wgm_ref, wa_ref, ws_ref, wm_ref, o_ref):
    h = h_ref[...]
    dot = lambda x, w: jnp.dot(x, w[...], preferred_element_type=F32)
    ga = jax.nn.sigmoid(dot(h, wga_ref))
    gs = jax.nn.sigmoid(dot(h, wgs_ref))
    gm = jax.nn.sigmoid(dot(h, wgm_ref))
    merged = (ga * dot(a_ref[...].astype(BF16), wa_ref) + gs * dot(s_ref[...].astype(BF16), ws_ref)
              + gm * dot(m_ref[...].astype(BF16), wm_ref))
    o_ref[...] = merged.astype(BF16)


def _merge(h, a, s, mo, wg, wa, ws, wm, tm, tn):
    m, d = h.shape
    nj = d // tn
    row = lambda w: pl.BlockSpec((tm, w), lambda i, j: (i, 0))
    col = lambda k, off: pl.BlockSpec((k, tn), lambda i, j: (0, off + j))
    return pl.pallas_call(
        _merge_kernel,
        grid=(m // tm, nj),
        in_specs=[row(d), row(a.shape[1]), row(s.shape[1]), row(mo.shape[1]),
                  col(d, 0), col(d, nj), col(d, 2 * nj),
                  col(a.shape[1], 0), col(s.shape[1], 0), col(mo.shape[1], 0)],
        out_specs=pl.BlockSpec((tm, tn), lambda i, j: (i, j)),
        out_shape=jax.ShapeDtypeStruct((m, d), BF16),
        compiler_params=_cparams(("arbitrary", "arbitrary")),
        name="merge",
    )(h, a, s, mo, wg, wg, wg, wa, ws, wm)


def _proj_res_kernel(x_ref, m_ref, w_ref, o_ref):
    o_ref[...] = x_ref[...] + jnp.dot(m_ref[...], w_ref[...], preferred_element_type=F32)


def _proj_res(x, merged, w, tm, tn):
    m, d = x.shape
    return pl.pallas_call(
        _proj_res_kernel,
        grid=(m // tm, d // tn),
        in_specs=[pl.BlockSpec((tm, tn), lambda i, j: (i, j)),
                  pl.BlockSpec((tm, d), lambda i, j: (i, 0)),
                  pl.BlockSpec((d, tn), lambda i, j: (0, j))],
        out_specs=pl.BlockSpec((tm, tn), lambda i, j: (i, j)),
        out_shape=jax.ShapeDtypeStruct((m, d), F32),
        compiler_params=_cparams(("arbitrary", "arbitrary")),
        name="proj_res",
    )(x, merged, w)


def _ffn_kernel(x_ref, g_ref, w1_ref, w3_ref, w2_ref, o_ref, h_ref):
    f = pl.program_id(1)

    @pl.when(f == 0)
    def _():
        x = x_ref[...]
        r = lax.rsqrt(jnp.mean(x * x, axis=-1, keepdims=True) + EPS)
        h_ref[...] = ((x * r) * g_ref[...]).astype(BF16)
        o_ref[...] = x

    h = h_ref[...]
    a = jnp.dot(h, w1_ref[...], preferred_element_type=F32)
    b = jnp.dot(h, w3_ref[...], preferred_element_type=F32)
    act = (jax.nn.silu(a) * b).astype(BF16)
    o_ref[...] += jnp.dot(act, w2_ref[...], preferred_element_type=F32)


def _ffn(x, g, w1, w3, w2, tm, tf):
    m, d = x.shape
    ff = w1.shape[1]
    return pl.pallas_call(
        _ffn_kernel,
        grid=(m // tm, ff // tf),
        in_specs=[pl.BlockSpec((tm, d), lambda i, f: (i, 0)),
                  pl.BlockSpec((1, d), lambda i, f: (0, 0)),
                  pl.BlockSpec((d, tf), lambda i, f: (0, f)),
                  pl.BlockSpec((d, tf), lambda i, f: (0, f)),
                  pl.BlockSpec((tf, d), lambda i, f: (f, 0))],
        out_specs=pl.BlockSpec((tm, d), lambda i, f: (i, 0)),
        out_shape=jax.ShapeDtypeStruct((m, d), F32),
        scratch_shapes=[pltpu.VMEM((tm, d), BF16)],
        compiler_params=_cparams(("arbitrary", "arbitrary")),
        name="ffn",
    )(x, g, w1, w3, w2)


def _pack_w_in(w, qn, kn, mqn):
    d = w.shape[0]
    aw, kvw, iqw, mw = N_HEADS * HEAD_DIM, N_KV_HEADS * HEAD_DIM, IDX_HEADS * IDX_DIM, MEM_HEADS * HEAD_DIM
    o_q, o_k, o_v, o_qi = 0, aw, aw + kvw, aw + 2 * kvw
    o_ki = o_qi + iqw
    o_wi = o_ki + IDX_DIM
    o_u = o_wi + IDX_HEADS
    uw = w.shape[1] - o_u - mw - 3 * d
    o_mq = o_u + uw
    o_g = o_mq + mw
    zeros = lambda n: jnp.zeros((d, n), w.dtype)
    wp = jnp.concatenate([
        w[:, o_q:o_ki], w[:, o_mq:o_g], w[:, o_ki:o_u], zeros(C_U - C_WI - IDX_HEADS),
        w[:, o_u:o_mq], zeros(Z_COLS - C_U - uw)], axis=1).astype(BF16)
    ones = lambda n: jnp.ones((n,), F32)
    zer = lambda n: jnp.zeros((n,), F32)
    flag = jnp.concatenate([ones(aw + kvw), zer(kvw + iqw), ones(mw), zer(Z_COLS - C_KI)])[None]
    gain = jnp.concatenate([
        jnp.tile(qn, N_HEADS), jnp.tile(kn, N_KV_HEADS), ones(kvw + iqw), jnp.tile(mqn, MEM_HEADS),
        ones(IDX_DIM), jnp.full((IDX_HEADS,), IDX_HEADS ** -0.5, F32), ones(Z_COLS - C_WI - IDX_HEADS)])[None]
    return wp, w[:, o_g:].astype(BF16), flag, gain, uw


def kernel(x_prompt, x_sample, mem_prompt, cache_k, cache_v, cache_kidx, cache_mem_k, cache_mem_v, state_ssm_re, state_ssm_im, page_table, norm1_g, w_in, q_norm_g, k_norm_g, ssm_lam_re, ssm_lam_im, ssm_log_dt, ssm_b_re, ssm_b_im, ssm_c_re, ssm_c_im, ssm_d, w_glu, mem_norm_g, w_mem_kv, mq_norm_g, mk_norm_g, w_br_attn, w_br_ssm, w_br_mem, w_o, norm2_g, w_ff1, w_ff3, w_ff2):
    depth = w_in.shape[0]
    bp, tp, d = x_prompt.shape
    bs, ts, _ = x_sample.shape
    rp, rs = bp * tp, bs * ts
    m_tok = mem_prompt.shape[1]
    mw = MEM_HEADS * HEAD_DIM
    kvw = N_KV_HEADS * HEAD_DIM
    groups, p_state = ssm_lam_re.shape[1], ssm_lam_re.shape[2]
    tm = _row_tile(rp + rs)

    x = jnp.concatenate([x_prompt.reshape(rp, d), x_sample.reshape(rs, d)], axis=0)
    mem = mem_prompt.reshape(bp * m_tok, d)
    zero_state = jnp.zeros((bp, groups, p_state), F32)
    outs = [[] for _ in range(12)]
    nc_p, nc_s = tp // SSM_CHUNK, ts // SSM_CHUNK
    n_steps = max(1, int(math.log2(nc_p)))

    for l in range(depth):
        wp, wg, flag, gain, uw = _pack_w_in(w_in[l], q_norm_g[l], k_norm_g[l], mq_norm_g[l])
        z, h = _norm_matmul(x, norm1_g[l][None], wp, flag, gain, tm=tm, tn=1024, emit_h=True)

        mflag = jnp.concatenate([jnp.ones((mw,), F32), jnp.zeros((mw,), F32)])[None]
        mgain = jnp.concatenate([jnp.tile(mk_norm_g[l], MEM_HEADS), jnp.ones((mw,), F32)])[None]
        mkv = _norm_matmul(mem, mem_norm_g[l][None], w_mem_kv[l].astype(BF16), mflag, mgain,
                           tm=bp * m_tok, tn=2 * mw, emit_h=False)

        a_p = _dsa_prompt(z, bp, tp)
        a_s = _dsa_sample(z, rp, bs, ts, cache_k[l], cache_v[l], cache_kidx[l], page_table)
        a_out = jnp.concatenate([a_p, a_s], axis=0)

        mats = _ssm_matrices(ssm_lam_re[l], ssm_lam_im[l], ssm_log_dt[l], ssm_b_re[l], ssm_b_im[l],
                             ssm_c_re[l], ssm_c_im[l], ssm_d[l], n_steps)
        u = z[:, C_U:C_U + uw]
        y_p, sf_p = _ssm(_to_pairs(u[:rp], bp, tp), mats, _state_to_pairs(zero_state, zero_state), bp, nc_p)
        y_s, sf_s = _ssm(_to_pairs(u[rp:], bs, ts), mats,
                         _state_to_pairs(state_ssm_re[l], state_ssm_im[l]), bs, nc_s)
        y = jnp.concatenate([_from_pairs(y_p, bp, tp), _from_pairs(y_s, bs, ts)], axis=0)
        s_out = _glu(y, w_glu[l].astype(BF16), tm)

        m_p = _mem_attn(z, 0, bp, tp, 512, mkv, mkv, 0, 1)
        cmk = cache_mem_k[l].reshape(bs * m_tok, mw)
        cmv = cache_mem_v[l].reshape(bs * m_tok, mw)
        m_s = _mem_attn(z, rp, bs, ts, ts, cmk, cmv, 0, 0)
        m_out = jnp.concatenate([m_p, m_s], axis=0)

        merged = _merge(h, a_out, s_out, m_out, wg, w_br_attn[l].astype(BF16), w_br_ssm[l].astype(BF16),
                        w_br_mem[l].astype(BF16), tm, min(512, d))
        x = _proj_res(x, merged, w_o[l].astype(BF16), tm, min(1024, d))
        x = _ffn(x, norm2_g[l][None], w_ff1[l].astype(BF16), w_ff3[l].astype(BF16), w_ff2[l].astype(BF16),
                 tm, 512)

        srp, sip = _state_from_pairs(sf_p)
        srs, sis = _state_from_pairs(sf_s)
        new = [
            z[:rp, C_K:C_K + kvw].reshape(bp, tp, N_KV_HEADS, HEAD_DIM),
            z[:rp, C_V:C_V + kvw].reshape(bp, tp, N_KV_HEADS, HEAD_DIM),
            z[:rp, C_KI:C_KI + IDX_DIM].reshape(bp, tp, IDX_DIM),
            mkv[:, :mw].reshape(bp, m_tok, MEM_HEADS, HEAD_DIM),
            mkv[:, mw:].reshape(bp, m_tok, MEM_HEADS, HEAD_DIM),
            srp, sip,
            z[rp:, C_K:C_K + kvw].reshape(bs, ts, N_KV_HEADS, HEAD_DIM),
            z[rp:, C_V:C_V + kvw].reshape(bs, ts, N_KV_HEADS, HEAD_DIM),
            z[rp:, C_KI:C_KI + IDX_DIM].reshape(bs, ts, IDX_DIM),
            srs, sis,
        ]
        for o, v in zip(outs, new):
            o.append(v)

    st = lambda xs: jnp.stack(xs, axis=0)
    return (x[:rp].reshape(bp, tp, d), x[rp:].reshape(bs, ts, d), *[st(o) for o in outs])
```

```python
import functools
import math

import jax
import jax.numpy as jnp
from jax import lax
from jax.experimental import pallas as pl
from jax.experimental.pallas import tpu as pltpu

F32 = jnp.float32
BF16 = jnp.bfloat16
I32 = jnp.int32

LANES = 128
SUBLANES = 8
BF16_SUBLANES = 16
EPS = 1e-6
HEAD_DIM = 128
N_HEADS = 6
N_KV_HEADS = 2
KV_GROUP = N_HEADS // N_KV_HEADS
IDX_HEADS = 4
IDX_DIM = 64
TOPK_MAX = 256
SSM_GROUP = 16
SSM_STATE = 64
SSM_CHUNK = SUBLANES
SSM_PAIR_W = 2 * SSM_CHUNK * SSM_GROUP
PAIRS_PER_BLOCK = LANES // (2 * SSM_GROUP)
MEM_HEADS = 4
KEY_MIN = -2 ** 31
NEG_INF = float("-inf")
NEG_BIAS = -1e30
M_INIT = -1e29
VMEM_LIMIT = 56 * 1024 * 1024
MAX_ROW_TILE = 768

C_Q, C_K, C_V, C_QI, C_MQ, C_KI, C_WI, C_U = 0, 768, 1024, 1280, 1536, 2048, 2112, 2176
Z_COLS = 3072


def _cparams(sem):
    return pltpu.CompilerParams(dimension_semantics=sem, vmem_limit_bytes=VMEM_LIMIT)


def _row_tile(m):
    return max(t for t in range(BF16_SUBLANES, MAX_ROW_TILE + 1, BF16_SUBLANES) if m % t == 0)


def _pack_kernel(w_ref, wp_ref, wg_ref, *, segs, zero_spans, o_g):
    for src, width, dst in segs:
        wp_ref[:, dst:dst + width] = w_ref[:, src:src + width].astype(BF16)
    for start, stop in zero_spans:
        wp_ref[:, start:stop] = jnp.zeros((wp_ref.shape[0], stop - start), BF16)
    wg_ref[...] = w_ref[:, o_g:].astype(BF16)


def _pack_w_in(w_in, l):
    _, d, cols = w_in.shape
    aw, kvw, iqw, mw = N_HEADS * HEAD_DIM, N_KV_HEADS * HEAD_DIM, IDX_HEADS * IDX_DIM, MEM_HEADS * HEAD_DIM
    o_ki = aw + 2 * kvw + iqw
    o_u = o_ki + IDX_DIM + IDX_HEADS
    uw = cols - o_u - mw - 3 * d
    o_mq = o_u + uw
    o_g = o_mq + mw
    segs = ((0, o_ki, C_Q), (o_mq, mw, C_MQ), (o_ki, o_u - o_ki, C_KI), (o_u, uw, C_U))
    zero_spans = ((C_WI + IDX_HEADS, C_U), (C_U + uw, Z_COLS))
    tk = 256
    wp, wg = pl.pallas_call(
        functools.partial(_pack_kernel, segs=segs, zero_spans=zero_spans, o_g=o_g),
        grid=(d // tk,),
        in_specs=[pl.BlockSpec((None, tk, cols), lambda i: (l, i, 0))],
        out_specs=[pl.BlockSpec((tk, Z_COLS), lambda i: (i, 0)), pl.BlockSpec((tk, 3 * d), lambda i: (i, 0))],
        out_shape=[jax.ShapeDtypeStruct((d, Z_COLS), BF16), jax.ShapeDtypeStruct((d, 3 * d), BF16)],
        compiler_params=_cparams(("arbitrary",)),
        name="pack_w_in",
    )(w_in)
    return wp, wg, uw


def _epilogue_rows(qn, kn, mqn):
    aw, kvw, iqw, mw = N_HEADS * HEAD_DIM, N_KV_HEADS * HEAD_DIM, IDX_HEADS * IDX_DIM, MEM_HEADS * HEAD_DIM
    ones = lambda n: jnp.ones((n,), F32)
    zer = lambda n: jnp.zeros((n,), F32)
    flag = jnp.concatenate([ones(aw + kvw), zer(kvw + iqw), ones(mw), zer(Z_COLS - C_KI)])[None]
    gain = jnp.concatenate([
        jnp.tile(qn, N_HEADS), jnp.tile(kn, N_KV_HEADS), ones(kvw + iqw), jnp.tile(mqn, MEM_HEADS),
        ones(IDX_DIM), jnp.full((IDX_HEADS,), IDX_HEADS ** -0.5, F32), ones(Z_COLS - C_WI - IDX_HEADS)])[None]
    return flag, gain


def _norm_matmul_kernel(x_ref, g_ref, w_ref, flag_ref, gain_ref, z_ref, *rest, emit_h):
    if emit_h:
        h_ref, hs_ref = rest
    else:
        (hs_ref,) = rest
    j = pl.program_id(1)

    @pl.when(j == 0)
    def _():
        x = x_ref[...]
        r = lax.rsqrt(jnp.mean(x * x, axis=-1, keepdims=True) + EPS)
        h = ((x * r) * g_ref[...]).astype(BF16)
        hs_ref[...] = h
        if emit_h:
            h_ref[...] = h

    acc = jnp.dot(hs_ref[...], w_ref[...], preferred_element_type=F32)
    for c in range(acc.shape[1] // LANES):
        sl = slice(c * LANES, (c + 1) * LANES)
        y = acc[:, sl]
        r = lax.rsqrt(jnp.mean(y * y, axis=-1, keepdims=True) + EPS)
        scale = jnp.where(flag_ref[:, sl] > 0.0, r, 1.0) * gain_ref[:, sl]
        z_ref[:, sl] = y * scale


def _norm_matmul(x, g, w, flag, gain, *, tm, tn, emit_h):
    m, d = x.shape
    n = w.shape[1]
    out_shape = [jax.ShapeDtypeStruct((m, n), F32)]
    out_specs = [pl.BlockSpec((tm, tn), lambda i, j: (i, j))]
    if emit_h:
        out_shape.append(jax.ShapeDtypeStruct((m, d), BF16))
        out_specs.append(pl.BlockSpec((tm, d), lambda i, j: (i, 0)))
    res = pl.pallas_call(
        functools.partial(_norm_matmul_kernel, emit_h=emit_h),
        grid=(m // tm, n // tn),
        in_specs=[
            pl.BlockSpec((tm, d), lambda i, j: (i, 0)),
            pl.BlockSpec((1, d), lambda i, j: (0, 0)),
            pl.BlockSpec((d, tn), lambda i, j: (0, j)),
            pl.BlockSpec((1, tn), lambda i, j: (0, j)),
            pl.BlockSpec((1, tn), lambda i, j: (0, j)),
        ],
        out_specs=out_specs,
        out_shape=out_shape,
        scratch_shapes=[pltpu.VMEM((tm, d), BF16)],
        compiler_params=_cparams(("arbitrary", "arbitrary")),
        name="norm_matmul",
    )(x, g, w, flag, gain)
    return res if emit_h else res[0]


def _ordered_to_f32(key):
    return pltpu.bitcast(jnp.where(key >= 0, key, key ^ 0x7FFFFFFF), F32)


def _count(score_ref, n_slab, slab_w, indicator):
    rows = score_ref.shape[0]

    def slab(s, acc):
        static = isinstance(s, int)
        base = s * slab_w if static else pl.multiple_of(s * slab_w, slab_w)
        parts = []
        for c in range(slab_w // LANES):
            off = base + c * LANES
            x = score_ref[:, off:off + LANES] if static else score_ref[:, pl.ds(off, LANES)]
            parts.append(indicator(x, off))
        while len(parts) > 1:
            parts = [a + b for a, b in zip(parts[0::2], parts[1::2])] + ([parts[-1]] if len(parts) % 2 else [])
        return acc + parts[0]

    acc = jnp.zeros((rows, LANES), F32)
    if isinstance(n_slab, int):
        for s in range(n_slab):
            acc = slab(s, acc)
    else:
        acc = lax.fori_loop(0, n_slab, slab, acc)
    return jnp.sum(acc, axis=1, keepdims=True)


def _kth_largest(score_ref, n_slab, slab_w, k):
    rows = score_ref.shape[0]
    count_ge = lambda cand: _count(score_ref, n_slab, slab_w, lambda x, off: jnp.where(x >= cand, 1.0, 0.0))
    lo = jnp.where(count_ge(jnp.zeros((rows, 1), F32)) >= k, 0, KEY_MIN)

    def body(it, lo):
        cand = lo | (1 << (30 - it))
        return jnp.where(count_ge(_ordered_to_f32(cand)) >= k, cand, lo)

    lo = lax.fori_loop(0, 31, body, lo)
    return jnp.where(lo == KEY_MIN, NEG_INF, _ordered_to_f32(lo))


def _bias_by_prefix(score_ref, bias_ref, n_slab, slab_w, k, thr):
    rows = score_ref.shape[0]
    need = k - _count(score_ref, n_slab, slab_w, lambda x, off: jnp.where(x > thr, 1.0, 0.0))
    tri = (lax.broadcasted_iota(I32, (LANES, LANES), 0)
           <= lax.broadcasted_iota(I32, (LANES, LANES), 1)).astype(BF16)

    def body(c, seen):
        off = pl.multiple_of(c * LANES, LANES)
        x = score_ref[:, pl.ds(off, LANES)]
        eqf = jnp.where(x == thr, 1.0, 0.0)
        rank = seen + jnp.dot(eqf.astype(BF16), tri, preferred_element_type=F32)
        keep = jnp.where(x > thr, 1.0, jnp.where(rank <= need, eqf, 0.0))
        bias_ref[:, pl.ds(off, LANES)] = jnp.where(keep > 0.0, jnp.where(x > NEG_INF, 0.0, NEG_BIAS), NEG_BIAS)
        return seen + jnp.sum(eqf, axis=1, keepdims=True)

    lax.fori_loop(0, n_slab * (slab_w // LANES), body, jnp.zeros((rows, 1), F32))


def _bias_by_bisect(score_ref, bias_ref, n128, k, thr):
    rows = score_ref.shape[0]
    width = n128 * LANES
    need = k - _count(score_ref, 1, width, lambda x, off: jnp.where(x > thr, 1.0, 0.0))
    lane = lax.broadcasted_iota(I32, (rows, LANES), 1)

    def ties_upto(j):
        return _count(score_ref, 1, width, lambda x, off: jnp.where(x == thr, jnp.where(lane + off <= j, 1.0, 0.0), 0.0))

    def body(_, c):
        lo, hi = c
        mid = (lo + hi) >> 1
        ok = ties_upto(mid) >= need
        return jnp.where(ok, lo, mid), jnp.where(ok, mid, hi)

    steps = int(math.ceil(math.log2(width + 1)))
    _, cut = lax.fori_loop(0, steps, body, (jnp.full((rows, 1), -1, I32), jnp.full((rows, 1), width - 1, I32)))
    for c in range(n128):
        x = score_ref[:, c * LANES:(c + 1) * LANES]
        tie = jnp.where(x == thr, jnp.where(lane + c * LANES <= cut, 1.0, 0.0), 0.0)
        keep = jnp.where(x > thr, 1.0, tie)
        bias_ref[:, c * LANES:(c + 1) * LANES] = jnp.where(keep > 0.0, jnp.where(x > NEG_INF, 0.0, NEG_BIAS),
                                                           NEG_BIAS)


def _flash_step(q, k_c, v_c, bias, carry):
    m, l, acc = carry
    s = lax.dot_general(q, k_c, (((1,), (1,)), ((), ())), preferred_element_type=F32) + bias
    m_new = jnp.maximum(m, jnp.max(s, axis=1, keepdims=True))
    alpha = jnp.exp(m - m_new)
    p = jnp.exp(s - m_new)
    l = alpha * l + jnp.sum(p, axis=1, keepdims=True)
    acc = alpha * acc + jnp.dot(p.astype(BF16), v_c, preferred_element_type=F32)
    return m_new, l, acc


def _flash_init(rows):
    return (jnp.full((rows, 1), M_INIT, F32), jnp.zeros((rows, 1), F32), jnp.zeros((rows, HEAD_DIM), F32))


def _stack_heads(x, first, count, width):
    return jnp.concatenate([x[:, (first + r) * width:(first + r + 1) * width] for r in range(count)], axis=0)


def _indexer_scores(qi_stack, w_cols, ki_c, rows, keys_on_lanes=False):
    dims = (((1,), (0,)), ((), ())) if keys_on_lanes else (((1,), (1,)), ((), ()))
    logits = lax.dot_general(qi_stack, ki_c, dims, preferred_element_type=F32)
    score = jnp.maximum(logits[0:rows], 0.0) * w_cols[0]
    for h in range(1, IDX_HEADS):
        score = score + jnp.maximum(logits[h * rows:(h + 1) * rows], 0.0) * w_cols[h]
    return score


def _write_heads(o_ref, carries, rows):
    for g, (_, l, acc) in enumerate(carries):
        o = acc / l
        for r in range(KV_GROUP):
            hh = g * KV_GROUP + r
            o_ref[:, hh * HEAD_DIM:(hh + 1) * HEAD_DIM] = o[r * rows:(r + 1) * rows]


TQ = 128
KC = 512


def _dsa_prompt_kernel(q_ref, qi_ref, kiwq_ref, k_ref, v_ref, kiw_ref, o_ref,
                       kbf_ref, vbf_ref, kibf_ref, score_ref, bias_ref, *, topk):
    i = pl.program_id(1)

    @pl.when(i == 0)
    def _():
        kbf_ref[...] = k_ref[...].astype(BF16)
        vbf_ref[...] = v_ref[...].astype(BF16)
        kibf_ref[...] = kiw_ref[...].astype(BF16)

    n_kc = (i * TQ + TQ + KC - 1) // KC
    qi_stack = _stack_heads(qi_ref[...], 0, IDX_HEADS, IDX_DIM).astype(BF16)
    kiw = kiwq_ref[...]
    scale_i = IDX_DIM ** -0.5
    w_cols = [kiw[:, IDX_DIM + h:IDX_DIM + h + 1] * scale_i for h in range(IDX_HEADS)]
    qpos = i * TQ + lax.broadcasted_iota(I32, (TQ, KC), 0)

    def score_body(c, _):
        off = pl.multiple_of(c * KC, KC)
        ki_c = kibf_ref[pl.ds(off, KC), 0:IDX_DIM]
        score = _indexer_scores(qi_stack, w_cols, ki_c, TQ)
        kpos = off + lax.broadcasted_iota(I32, (TQ, KC), 1)
        score_ref[:, pl.ds(off, KC)] = jnp.where(kpos > qpos, NEG_INF, score)
        return 0

    lax.fori_loop(0, n_kc, score_body, 0)
    thr = _kth_largest(score_ref, n_kc, KC, topk)
    _bias_by_prefix(score_ref, bias_ref, n_kc, KC, topk, thr)

    q = q_ref[...] * (HEAD_DIM ** -0.5)
    q_gs = [_stack_heads(q, g * KV_GROUP, KV_GROUP, HEAD_DIM).astype(BF16) for g in range(N_KV_HEADS)]

    def att_body(c, carries):
        off = pl.multiple_of(c * KC, KC)
        bias = jnp.concatenate([bias_ref[:, pl.ds(off, KC)]] * KV_GROUP, axis=0)
        out = []
        for g in range(N_KV_HEADS):
            gs = slice(g * HEAD_DIM, (g + 1) * HEAD_DIM)
            out.append(_flash_step(q_gs[g], kbf_ref[pl.ds(off, KC), gs], vbf_ref[pl.ds(off, KC), gs],
                                   bias, carries[g]))
        return tuple(out)

    carries = lax.fori_loop(0, n_kc, att_body, tuple(_flash_init(KV_GROUP * TQ) for _ in range(N_KV_HEADS)))
    _write_heads(o_ref, carries, TQ)


def _dsa_prompt(z, batch, seq):
    nq = seq // TQ
    topk = min(TOPK_MAX, seq // 4)
    aw, kvw, iqw = N_HEADS * HEAD_DIM, N_KV_HEADS * HEAD_DIM, IDX_HEADS * IDX_DIM
    return pl.pallas_call(
        functools.partial(_dsa_prompt_kernel, topk=topk),
        grid=(batch, nq),
        in_specs=[
            pl.BlockSpec((TQ, aw), lambda b, i: (b * nq + i, C_Q // aw)),
            pl.BlockSpec((TQ, iqw), lambda b, i: (b * nq + i, C_QI // iqw)),
            pl.BlockSpec((TQ, LANES), lambda b, i: (b * nq + i, C_KI // LANES)),
            pl.BlockSpec((seq, kvw), lambda b, i: (b, C_K // kvw)),
            pl.BlockSpec((seq, kvw), lambda b, i: (b, C_V // kvw)),
            pl.BlockSpec((seq, LANES), lambda b, i: (b, C_KI // LANES)),
        ],
        out_specs=pl.BlockSpec((TQ, aw), lambda b, i: (b * nq + i, 0)),
        out_shape=jax.ShapeDtypeStruct((batch * seq, aw), F32),
        scratch_shapes=[
            pltpu.VMEM((seq, kvw), BF16),
            pltpu.VMEM((seq, kvw), BF16),
            pltpu.VMEM((seq, LANES), BF16),
            pltpu.VMEM((TQ, seq), F32),
            pltpu.VMEM((TQ, seq), F32),
        ],
        compiler_params=_cparams(("arbitrary", "arbitrary")),
        name="dsa_prompt",
    )(z, z, z, z, z, z)


def _dsa_sample_kernel(pt_ref, q_ref, qi_ref, kiw_ref, kn_ref, vn_ref, ck_ref, cv_ref, cki_ref, o_ref,
                       kbuf, vbuf, kibuf, score_ref, bias_ref, sem, *, layer, n_pages, page, tq, topk):
    b = pl.program_id(0)
    nb = pl.num_programs(0)
    past = n_pages * page
    slot = b % 2
    streams = ((ck_ref, kbuf, page * N_KV_HEADS), (cv_ref, vbuf, page * N_KV_HEADS), (cki_ref, kibuf, IDX_DIM))

    def page_copy(which, bb, sl, p):
        src, dst, rpp = streams[which]
        return pltpu.make_async_copy(src.at[layer, pt_ref[bb * n_pages + p]],
                                     dst.at[sl, pl.ds(p * rpp, rpp)], sem.at[which, sl])

    def start_fetch(bb, sl):
        def body(p, _):
            for which in range(len(streams)):
                page_copy(which, bb, sl, p).start()
            return 0
        lax.fori_loop(0, n_pages, body, 0)

    def wait_fetch(bb, sl, which):
        def body(p, _):
            page_copy(which, bb, sl, p).wait()
            return 0
        lax.fori_loop(0, n_pages, body, 0)

    @pl.when(b == 0)
    def _():
        start_fetch(0, 0)

    @pl.when(b + 1 < nb)
    def _():
        start_fetch(b + 1, 1 - slot)

    qi_stack = _stack_heads(qi_ref[...], 0, IDX_HEADS, IDX_DIM).astype(BF16)
    kiw = kiw_ref[...]
    scale_i = IDX_DIM ** -0.5
    w_cols = [kiw[:, IDX_DIM + h:IDX_DIM + h + 1] * scale_i for h in range(IDX_HEADS)]
    n_kc = past // KC
    n128 = past // LANES + 1

    wait_fetch(b, slot, 2)

    def score_body(c, _):
        off = pl.multiple_of(c * KC, KC)
        ki_t = jnp.concatenate([kibuf[slot, pl.ds((c * (KC // page) + j) * IDX_DIM, IDX_DIM), :]
                                for j in range(KC // page)], axis=1).astype(BF16)
        score_ref[:, pl.ds(off, KC)] = _indexer_scores(qi_stack, w_cols, ki_t, tq, keys_on_lanes=True)
        return 0

    lax.fori_loop(0, n_kc, score_body, 0)
    pad = jnp.zeros((LANES - tq, LANES), F32)
    ki_new = jnp.concatenate([kiw, pad], axis=0)[:, 0:IDX_DIM].astype(BF16)
    score_new = _indexer_scores(qi_stack, w_cols, ki_new, tq)
    newer = lax.broadcasted_iota(I32, (tq, LANES), 1) > lax.broadcasted_iota(I32, (tq, LANES), 0)
    score_ref[:, past:past + LANES] = jnp.where(newer, NEG_INF, score_new)

    thr = _kth_largest(score_ref, 1, n128 * LANES, topk)
    _bias_by_bisect(score_ref, bias_ref, n128, topk, thr)

    wait_fetch(b, slot, 0)
    wait_fetch(b, slot, 1)
    q = q_ref[...] * (HEAD_DIM ** -0.5)
    q_gs = [_stack_heads(q, g * KV_GROUP, KV_GROUP, HEAD_DIM).astype(BF16) for g in range(N_KV_HEADS)]
    kv_pad = jnp.zeros((LANES - tq, N_KV_HEADS * HEAD_DIM), F32)
    k_new = jnp.concatenate([kn_ref[...], kv_pad], axis=0).astype(BF16)
    v_new = jnp.concatenate([vn_ref[...], kv_pad], axis=0).astype(BF16)

    def att_body(c, carries):
        off = pl.multiple_of(c * KC, KC)
        bias = jnp.concatenate([bias_ref[:, pl.ds(off, KC)]] * KV_GROUP, axis=0)
        out = []
        for g in range(N_KV_HEADS):
            rows = pl.ds(off * N_KV_HEADS + g, KC, stride=N_KV_HEADS)
            out.append(_flash_step(q_gs[g], kbuf[slot, rows, :].astype(BF16), vbuf[slot, rows, :].astype(BF16),
                                   bias, carries[g]))
        return tuple(out)

    carries = lax.fori_loop(0, n_kc, att_body, tuple(_flash_init(KV_GROUP * tq) for _ in range(N_KV_HEADS)))
    bias_new = jnp.concatenate([bias_ref[:, past:past + LANES]] * KV_GROUP, axis=0)
    carries = [_flash_step(q_gs[g], k_new[:, g * HEAD_DIM:(g + 1) * HEAD_DIM],
                           v_new[:, g * HEAD_DIM:(g + 1) * HEAD_DIM], bias_new, carries[g])
               for g in range(N_KV_HEADS)]
    _write_heads(o_ref, carries, tq)


def _dsa_sample(z, row0, nb, tq, layer, cache_k, cache_v, cache_kidx, page_table):
    depth, n_pool, page = cache_k.shape[:3]
    n_pages = page_table.shape[1]
    past = n_pages * page
    topk = min(TOPK_MAX, (past + tq) // 4)
    aw, kvw, iqw = N_HEADS * HEAD_DIM, N_KV_HEADS * HEAD_DIM, IDX_HEADS * IDX_DIM
    rb = row0 // tq
    grid_spec = pltpu.PrefetchScalarGridSpec(
        num_scalar_prefetch=1,
        grid=(nb,),
        in_specs=[
            pl.BlockSpec((tq, aw), lambda b, pt: (rb + b, C_Q // aw)),
            pl.BlockSpec((tq, iqw), lambda b, pt: (rb + b, C_QI // iqw)),
            pl.BlockSpec((tq, LANES), lambda b, pt: (rb + b, C_KI // LANES)),
            pl.BlockSpec((tq, kvw), lambda b, pt: (rb + b, C_K // kvw)),
            pl.BlockSpec((tq, kvw), lambda b, pt: (rb + b, C_V // kvw)),
            pl.BlockSpec(memory_space=pl.ANY),
            pl.BlockSpec(memory_space=pl.ANY),
            pl.BlockSpec(memory_space=pl.ANY),
        ],
        out_specs=pl.BlockSpec((tq, aw), lambda b, pt: (b, 0)),
        scratch_shapes=[
            pltpu.VMEM((2, past * N_KV_HEADS, HEAD_DIM), F32),
            pltpu.VMEM((2, past * N_KV_HEADS, HEAD_DIM), F32),
            pltpu.VMEM((2, n_pages * IDX_DIM, page), F32),
            pltpu.VMEM((tq, past + LANES), F32),
            pltpu.VMEM((tq, past + LANES), F32),
            pltpu.SemaphoreType.DMA((3, 2)),
        ],
    )
    ck = cache_k.reshape(depth, n_pool, page * N_KV_HEADS, HEAD_DIM)
    cv = cache_v.reshape(depth, n_pool, page * N_KV_HEADS, HEAD_DIM)
    return pl.pallas_call(
        functools.partial(_dsa_sample_kernel, layer=layer, n_pages=n_pages, page=page, tq=tq, topk=topk),
        grid_spec=grid_spec,
        out_shape=jax.ShapeDtypeStruct((nb * tq, aw), F32),
        compiler_params=_cparams(("arbitrary",)),
        name="dsa_sample",
    )(page_table.reshape(-1), z, z, z, z, z, ck, cv, jnp.swapaxes(cache_kidx, 2, 3))


def _mem_attn_kernel(q_ref, k_ref, v_ref, o_ref):
    q = q_ref[...] * (HEAD_DIM ** -0.5)
    m_tok = k_ref.shape[0] // MEM_HEADS
    for h in range(MEM_HEADS):
        hs = slice(h * HEAD_DIM, (h + 1) * HEAD_DIM)
        rows = pl.ds(h, m_tok, stride=MEM_HEADS)
        s = lax.dot_general(q[:, hs].astype(BF16), k_ref[rows, :].astype(BF16),
                            (((1,), (1,)), ((), ())), preferred_element_type=F32)
        m = jnp.max(s, axis=1, keepdims=True)
        p = jnp.exp(s - m)
        l = jnp.sum(p, axis=1, keepdims=True)
        o = jnp.dot(p.astype(BF16), v_ref[rows, :].astype(BF16), preferred_element_type=F32)
        o_ref[:, hs] = o / l


def _mem_attn(z, row0, nb, t, tq, mk, mv, layer):
    mw = MEM_HEADS * HEAD_DIM
    nq = t // tq
    rb = row0 // tq
    kv_spec = pl.BlockSpec((None, None) + mk.shape[2:], lambda b, i: (layer, b, 0, 0))
    return pl.pallas_call(
        _mem_attn_kernel,
        grid=(nb, nq),
        in_specs=[pl.BlockSpec((tq, mw), lambda b, i: (rb + b * nq + i, C_MQ // mw)), kv_spec, kv_spec],
        out_specs=pl.BlockSpec((tq, mw), lambda b, i: (b * nq + i, 0)),
        out_shape=jax.ShapeDtypeStruct((nb * t, mw), F32),
        compiler_params=_cparams(("arbitrary", "arbitrary")),
        name="mem_attn",
    )(z, mk, mv)


def _shift_rows(x, sh):
    rows = lax.broadcasted_iota(I32, x.shape, 0)
    return jnp.where(rows >= sh, pltpu.roll(x, sh, axis=0), 0.0)


def _ssm_pair(u, m_w, wx_w, wc_w, pr, pi, s0, nb, nc):
    half = SSM_PAIR_W // 2
    y_in = jnp.dot(u, m_w, preferred_element_type=F32)
    x = jnp.dot(u, wx_w, preferred_element_type=F32)
    xr, xi = x[:, :half], x[:, half:]
    s0r, s0i = s0[:, :half], s0[:, half:]
    ar, ai = pr[0:1], pi[0:1]
    if nc == 1:
        sr = xr + ar * s0r - ai * s0i
        si = xi + ar * s0i + ai * s0r
        pvr, pvi = s0r, s0i
        sfin = jnp.concatenate([sr, si], axis=1)
    else:
        prev_r, prev_i, fin = [], [], []
        first = lax.broadcasted_iota(I32, (nc, half), 0) == 0
        for b in range(nb):
            rs = slice(b * nc, (b + 1) * nc)
            b0r, b0i = s0r[b:b + 1], s0i[b:b + 1]
            sr = xr[rs] + jnp.where(first, ar * b0r - ai * b0i, 0.0)
            si = xi[rs] + jnp.where(first, ar * b0i + ai * b0r, 0.0)
            for mstep in range(int(math.log2(nc))):
                sh = 2 ** mstep
                qr, qi = pr[mstep:mstep + 1], pi[mstep:mstep + 1]
                tr, ti = _shift_rows(sr, sh), _shift_rows(si, sh)
                sr, si = sr + qr * tr - qi * ti, si + qr * ti + qi * tr
            fin.append(jnp.concatenate([sr[nc - 1:nc], si[nc - 1:nc]], axis=1))
            prev_r.append(jnp.where(first, b0r, pltpu.roll(sr, 1, axis=0)))
            prev_i.append(jnp.where(first, b0i, pltpu.roll(si, 1, axis=0)))
        pvr = jnp.concatenate(prev_r, axis=0)
        pvi = jnp.concatenate(prev_i, axis=0)
        sfin = jnp.concatenate(fin, axis=0)
    prev = jnp.concatenate([pvr, pvi], axis=1).astype(BF16)
    return y_in + jnp.dot(prev, wc_w, preferred_element_type=F32), sfin


def _ssm_kernel(u_ref, m_ref, wx_ref, wc_ref, pr_ref, pi_ref, s0_ref, y_ref, sf_ref, *, nb, nc):
    nch = nb * nc
    gw = SSM_GROUP
    tok = [u_ref[pl.ds(i, nch, stride=SSM_CHUNK), :] for i in range(SSM_CHUNK)]
    ys = []
    for q in range(PAIRS_PER_BLOCK):
        base = q * 2 * gw
        u = jnp.concatenate([tok[i][:, base + gg * gw:base + (gg + 1) * gw]
                             for gg in range(2) for i in range(SSM_CHUNK)], axis=1).astype(BF16)
        y, sfin = _ssm_pair(u, m_ref[q], wx_ref[q], wc_ref[q], pr_ref[q], pi_ref[q], s0_ref[q], nb, nc)
        sf_ref[q] = sfin
        ys.append(y)
    for i in range(SSM_CHUNK):
        y_ref[pl.ds(i, nch, stride=SSM_CHUNK), :] = jnp.concatenate(
            [ys[q][:, (gg * SSM_CHUNK + i) * gw:(gg * SSM_CHUNK + i + 1) * gw]
             for q in range(PAIRS_PER_BLOCK) for gg in range(2)], axis=1)


def _ssm(z, row0, rows, mats, s0_pairs, nb, nc):
    m_mat, wx, wc, pr, pi = mats
    npair, _, w = m_mat.shape
    pw = pr.shape[1]
    ppb = PAIRS_PER_BLOCK
    blk3 = lambda r, c: pl.BlockSpec((ppb, r, c), lambda j: (j, 0, 0))
    return pl.pallas_call(
        functools.partial(_ssm_kernel, nb=nb, nc=nc),
        grid=(npair // ppb,),
        in_specs=[pl.BlockSpec((rows, LANES), lambda j: (row0 // rows, C_U // LANES + j)),
                  blk3(w, w), blk3(w, w), blk3(w, w), blk3(pw, LANES), blk3(pw, LANES), blk3(nb, w)],
        out_specs=[pl.BlockSpec((rows, LANES), lambda j: (0, j)), blk3(nb, w)],
        out_shape=[jax.ShapeDtypeStruct((rows, npair * 2 * SSM_GROUP), F32),
                   jax.ShapeDtypeStruct((npair, nb, w), F32)],
        compiler_params=_cparams(("arbitrary",)),
        name="ssm",
    )(z, m_mat, wx, wc, pr, pi, s0_pairs)


def _ssm_matrices(lam_re, lam_im, log_dt, b_re, b_im, c_re, c_im, d_skip, n_steps):
    g, p, c = b_re.shape
    L = SSM_CHUNK
    dt = jnp.exp(log_dt)[:, None]
    zr, zi = lam_re * dt, lam_im * dt
    mag = jnp.exp(zr)
    ar, ai = mag * jnp.cos(zi), mag * jnp.sin(zi)
    den = lam_re * lam_re + lam_im * lam_im
    cr = ((ar - 1.0) * lam_re + ai * lam_im) / den
    ci = (ai * lam_re - (ar - 1.0) * lam_im) / den
    bbr = cr[..., None] * b_re - ci[..., None] * b_im
    bbi = cr[..., None] * b_im + ci[..., None] * b_re
    n = jnp.arange(L + 1, dtype=F32)[:, None, None]
    pm = jnp.exp(zr[None] * n)
    pwr, pwi = pm * jnp.cos(zi[None] * n), pm * jnp.sin(zi[None] * n)
    car = c_re[None] * pwr[:, :, None, :] - c_im[None] * pwi[:, :, None, :]
    cai = c_re[None] * pwi[:, :, None, :] + c_im[None] * pwr[:, :, None, :]
    kern = jnp.einsum('ngop,gpc->ngoc', car, bbr) - jnp.einsum('ngop,gpc->ngoc', cai, bbi)
    lag = jnp.arange(L)[None, :] - jnp.arange(L)[:, None]
    toe = jnp.where((lag >= 0)[:, :, None, None, None], kern[jnp.clip(lag, 0, L)], 0.0)
    eye = (lag == 0)[:, :, None, None, None] * (jnp.eye(c)[None, None, None] * d_skip[None, None, :, :, None])
    toe = toe + eye
    m_g = jnp.transpose(toe, (2, 0, 4, 1, 3)).reshape(g, L * c, L * c)
    rev = pwr[L - 1 - jnp.arange(L)], pwi[L - 1 - jnp.arange(L)]
    wxr = rev[0][..., None] * bbr[None] - rev[1][..., None] * bbi[None]
    wxi = rev[0][..., None] * bbi[None] + rev[1][..., None] * bbr[None]
    wxr = jnp.transpose(wxr, (1, 0, 3, 2)).reshape(g, L * c, p)
    wxi = jnp.transpose(wxi, (1, 0, 3, 2)).reshape(g, L * c, p)
    wcr = jnp.transpose(car[1:], (1, 3, 0, 2)).reshape(g, p, L * c)
    wci = -jnp.transpose(cai[1:], (1, 3, 0, 2)).reshape(g, p, L * c)

    def pair_blockdiag(a):
        a = a.reshape(g // 2, 2, a.shape[1], a.shape[2])
        z = jnp.zeros_like(a[:, 0])
        return jnp.concatenate([jnp.concatenate([a[:, 0], z], axis=2),
                                jnp.concatenate([z, a[:, 1]], axis=2)], axis=1)

    m_pair = pair_blockdiag(m_g)
    wx_pair = jnp.concatenate([pair_blockdiag(wxr), pair_blockdiag(wxi)], axis=2)
    wc_pair = jnp.concatenate([pair_blockdiag(wcr), pair_blockdiag(wci)], axis=1)
    qr, qi = [pwr[L]], [pwi[L]]
    for _ in range(n_steps - 1):
        qr, qi = qr + [qr[-1] * qr[-1] - qi[-1] * qi[-1]], qi + [2.0 * qr[-1] * qi[-1]]
    rows = -(-n_steps // SUBLANES) * SUBLANES
    padz = [jnp.zeros_like(qr[0])] * (rows - n_steps)
    pr = jnp.transpose(jnp.stack(qr + padz), (1, 0, 2)).reshape(g // 2, 2, rows, p)
    pi = jnp.transpose(jnp.stack(qi + padz), (1, 0, 2)).reshape(g // 2, 2, rows, p)
    pr = jnp.concatenate([pr[:, 0], pr[:, 1]], axis=2)
    pi = jnp.concatenate([pi[:, 0], pi[:, 1]], axis=2)
    return m_pair.astype(BF16), wx_pair.astype(BF16), wc_pair.astype(BF16), pr, pi


def _state_to_pairs(sr, si):
    nb, g, p = sr.shape
    f = lambda s: jnp.transpose(s.reshape(nb, g // 2, 2 * p), (1, 0, 2))
    return jnp.concatenate([f(sr), f(si)], axis=2)


def _state_from_pairs(s):
    npair, nb, w = s.shape
    f = lambda a: jnp.transpose(a, (1, 0, 2)).reshape(nb, npair * 2, w // 4)
    return f(s[:, :, :w // 2]), f(s[:, :, w // 2:])


def _glu_kernel(y_ref, w_ref, o_ref):
    y = jax.nn.gelu(y_ref[...])
    gate = jnp.dot(y.astype(BF16), w_ref[...], preferred_element_type=F32)
    o_ref[...] = y * jax.nn.sigmoid(gate)


def _glu(y, w, tm):
    m, n = y.shape
    return pl.pallas_call(
        _glu_kernel,
        grid=(m // tm,),
        in_specs=[pl.BlockSpec((tm, n), lambda i: (i, 0)), pl.BlockSpec((n, n), lambda i: (0, 0))],
        out_specs=pl.BlockSpec((tm, n), lambda i: (i, 0)),
        out_shape=jax.ShapeDtypeStruct((m, n), F32),
        compiler_params=_cparams(("arbitrary",)),
        name="glu",
    )(y, w)


def _merge_kernel(h_ref, a_ref, s_ref, m_ref, wga_ref, wgs_ref, wgm_ref, wa_ref, ws_ref, wm_ref, o_ref):
    h = h_ref[...]
    dot = lambda x, w: jnp.dot(x, w[...], preferred_element_type=F32)
    ga = jax.nn.sigmoid(dot(h, wga_ref))
    gs = jax.nn.sigmoid(dot(h, wgs_ref))
    gm = jax.nn.sigmoid(dot(h, wgm_ref))
    merged = (ga * dot(a_ref[...].astype(BF16), wa_ref) + gs * dot(s_ref[...].astype(BF16), ws_ref)
              + gm * dot(m_ref[...].astype(BF16), wm_ref))
    o_ref[...] = merged.astype(BF16)


def _merge(h, a, s, mo, wg, wa, ws, wm, tm, tn):
    m, d = h.shape
    nj = d // tn
    row = lambda w: pl.BlockSpec((tm, w), lambda i, j: (i, 0))
    col = lambda k, off: pl.BlockSpec((k, tn), lambda i, j: (0, off + j))
    return pl.pallas_call(
        _merge_kernel,
        grid=(m // tm, nj),
        in_specs=[row(d), row(a.shape[1]), row(s.shape[1]), row(mo.shape[1]),
                  col(d, 0), col(d, nj), col(d, 2 * nj),
                  col(a.shape[1], 0), col(s.shape[1], 0), col(mo.shape[1], 0)],
        out_specs=pl.BlockSpec((tm, tn), lambda i, j: (i, j)),
        out_shape=jax.ShapeDtypeStruct((m, d), BF16),
        compiler_params=_cparams(("arbitrary", "arbitrary")),
        name="merge",
    )(h, a, s, mo, wg, wg, wg, wa, ws, wm)


def _proj_res_kernel(x_ref, m_ref, w_ref, o_ref):
    o_ref[...] = x_ref[...] + jnp.dot(m_ref[...], w_ref[...], preferred_element_type=F32)


def _proj_res(x, merged, w, tm, tn):
    m, d = x.shape
    return pl.pallas_call(
        _proj_res_kernel,
        grid=(m // tm, d // tn),
        in_specs=[pl.BlockSpec((tm, tn), lambda i, j: (i, j)),
                  pl.BlockSpec((tm, d), lambda i, j: (i, 0)),
                  pl.BlockSpec((d, tn), lambda i, j: (0, j))],
        out_specs=pl.BlockSpec((tm, tn), lambda i, j: (i, j)),
        out_shape=jax.ShapeDtypeStruct((m, d), F32),
        compiler_params=_cparams(("arbitrary", "arbitrary")),
        name="proj_res",
    )(x, merged, w)


def _ffn_kernel(x_ref, g_ref, w1_ref, w3_ref, w2_ref, o_ref, h_ref):
    f = pl.program_id(1)

    @pl.when(f == 0)
    def _():
        x = x_ref[...]
        r = lax.rsqrt(jnp.mean(x * x, axis=-1, keepdims=True) + EPS)
        h_ref[...] = ((x * r) * g_ref[...]).astype(BF16)
        o_ref[...] = x

    h = h_ref[...]
    a = jnp.dot(h, w1_ref[...], preferred_element_type=F32)
    b = jnp.dot(h, w3_ref[...], preferred_element_type=F32)
    act = (jax.nn.silu(a) * b).astype(BF16)
    o_ref[...] += jnp.dot(act, w2_ref[...], preferred_element_type=F32)


def _ffn(x, g, w1, w3, w2, tm, tf):
    m, d = x.shape
    ff = w1.shape[1]
    return pl.pallas_call(
        _ffn_kernel,
        grid=(m // tm, ff // tf),
        in_specs=[pl.BlockSpec((tm, d), lambda i, f: (i, 0)),
                  pl.BlockSpec((1, d), lambda i, f: (0, 0)),
                  pl.BlockSpec((d, tf), lambda i, f: (0, f)),
                  pl.BlockSpec((d, tf), lambda i, f: (0, f)),
                  pl.BlockSpec((tf, d), lambda i, f: (f, 0))],
        out_specs=pl.BlockSpec((tm, d), lambda i, f: (i, 0)),
        out_shape=jax.ShapeDtypeStruct((m, d), F32),
        scratch_shapes=[pltpu.VMEM((tm, d), BF16)],
        compiler_params=_cparams(("arbitrary", "arbitrary")),
        name="ffn",
    )(x, g, w1, w3, w2)


def kernel(x_prompt, x_sample, mem_prompt, cache_k, cache_v, cache_kidx, cache_mem_k, cache_mem_v, state_ssm_re, state_ssm_im, page_table, norm1_g, w_in, q_norm_g, k_norm_g, ssm_lam_re, ssm_lam_im, ssm_log_dt, ssm_b_re, ssm_b_im, ssm_c_re, ssm_c_im, ssm_d, w_glu, mem_norm_g, w_mem_kv, mq_norm_g, mk_norm_g, w_br_attn, w_br_ssm, w_br_mem, w_o, norm2_g, w_ff1, w_ff3, w_ff2):
    depth = w_in.shape[0]
    bp, tp, d = x_prompt.shape
    bs, ts, _ = x_sample.shape
    rp, rs = bp * tp, bs * ts
    m_tok = mem_prompt.shape[1]
    mw = MEM_HEADS * HEAD_DIM
    kvw = N_KV_HEADS * HEAD_DIM
    groups, p_state = ssm_lam_re.shape[1], ssm_lam_re.shape[2]
    tm = _row_tile(rp + rs)

    x = jnp.concatenate([x_prompt.reshape(rp, d), x_sample.reshape(rs, d)], axis=0)
    mem = mem_prompt.reshape(bp * m_tok, d)
    zero_state = jnp.zeros((bp, groups, p_state), F32)
    cmk = cache_mem_k.reshape(depth, bs, m_tok * MEM_HEADS, HEAD_DIM)
    cmv = cache_mem_v.reshape(depth, bs, m_tok * MEM_HEADS, HEAD_DIM)
    outs = [[] for _ in range(12)]
    nc_p, nc_s = tp // SSM_CHUNK, ts // SSM_CHUNK
    n_steps = max(1, int(math.log2(nc_p)))

    for l in range(depth):
        wp, wg, uw = _pack_w_in(w_in, l)
        flag, gain = _epilogue_rows(q_norm_g[l], k_norm_g[l], mq_norm_g[l])
        z, h = _norm_matmul(x, norm1_g[l][None], wp, flag, gain, tm=tm, tn=1024, emit_h=True)

        mflag = jnp.concatenate([jnp.ones((mw,), F32), jnp.zeros((mw,), F32)])[None]
        mgain = jnp.concatenate([jnp.tile(mk_norm_g[l], MEM_HEADS), jnp.ones((mw,), F32)])[None]
        mkv = _norm_matmul(mem, mem_norm_g[l][None], w_mem_kv[l].astype(BF16), mflag, mgain,
                           tm=bp * m_tok, tn=2 * mw, emit_h=False)
        mk_p = mkv[:, :mw].reshape(bp, m_tok, MEM_HEADS, HEAD_DIM)
        mv_p = mkv[:, mw:].reshape(bp, m_tok, MEM_HEADS, HEAD_DIM)

        a_p = _dsa_prompt(z, bp, tp)
        a_s = _dsa_sample(z, rp, bs, ts, l, cache_k, cache_v, cache_kidx, page_table)
        a_out = jnp.concatenate([a_p, a_s], axis=0)

        mats = _ssm_matrices(ssm_lam_re[l], ssm_lam_im[l], ssm_log_dt[l], ssm_b_re[l], ssm_b_im[l],
                             ssm_c_re[l], ssm_c_im[l], ssm_d[l], n_steps)
        y_p, sf_p = _ssm(z, 0, rp, mats, _state_to_pairs(zero_state, zero_state), bp, nc_p)
        y_s, sf_s = _ssm(z, rp, rs, mats, _state_to_pairs(state_ssm_re[l], state_ssm_im[l]), bs, nc_s)
        s_out = _glu(jnp.concatenate([y_p, y_s], axis=0), w_glu[l].astype(BF16), tm)

        m_p = _mem_attn(z, 0, bp, tp, 512, mk_p.reshape(1, bp, m_tok * MEM_HEADS, HEAD_DIM),
                        mv_p.reshape(1, bp, m_tok * MEM_HEADS, HEAD_DIM), 0)
        m_s = _mem_attn(z, rp, bs, ts, ts, cmk, cmv, l)
        m_out = jnp.concatenate([m_p, m_s], axis=0)

        merged = _merge(h, a_out, s_out, m_out, wg, w_br_attn[l].astype(BF16), w_br_ssm[l].astype(BF16),
                        w_br_mem[l].astype(BF16), tm, min(512, d))
        x = _proj_res(x, merged, w_o[l].astype(BF16), tm, min(1024, d))
        x = _ffn(x, norm2_g[l][None], w_ff1[l].astype(BF16), w_ff3[l].astype(BF16), w_ff2[l].astype(BF16),
                 tm, 512)

        srp, sip = _state_from_pairs(sf_p)
        srs, sis = _state_from_pairs(sf_s)
        new = [
            z[:rp, C_K:C_K + kvw].reshape(bp, tp, N_KV_HEADS, HEAD_DIM),
            z[:rp, C_V:C_V + kvw].reshape(bp, tp, N_KV_HEADS, HEAD_DIM),
            z[:rp, C_KI:C_KI + IDX_DIM].reshape(bp, tp, IDX_DIM),
            mk_p, mv_p, srp, sip,
            z[rp:, C_K:C_K + kvw].reshape(bs, ts, N_KV_HEADS, HEAD_DIM),
            z[rp:, C_V:C_V + kvw].reshape(bs, ts, N_KV_HEADS, HEAD_DIM),
            z[rp:, C_KI:C_KI + IDX_DIM].reshape(bs, ts, IDX_DIM),
            srs, sis,
        ]
        for o, v in zip(outs, new):
            o.append(v)

    st = lambda xs: jnp.stack(xs, axis=0)
    return (x[:rp].reshape(bp, tp, d), x[rp:].reshape(bs, ts, d), *[st(o) for o in outs])
```

```python
import functools
import math

import jax
import jax.numpy as jnp
from jax import lax
from jax.experimental import pallas as pl
from jax.experimental.pallas import tpu as pltpu

F32 = jnp.float32
BF16 = jnp.bfloat16
I32 = jnp.int32

LANES = 128
SUBLANES = 8
BF16_SUBLANES = 16
EPS = 1e-6
HEAD_DIM = 128
N_HEADS = 6
N_KV_HEADS = 2
KV_GROUP = N_HEADS // N_KV_HEADS
IDX_HEADS = 4
IDX_DIM = 64
TOPK_MAX = 256
SSM_GROUP = 16
SSM_STATE = 64
SSM_CHUNK = SUBLANES
SSM_PAIR_W = 2 * SSM_CHUNK * SSM_GROUP
PAIRS_PER_BLOCK = LANES // (2 * SSM_GROUP)
MEM_HEADS = 4
KEY_MIN = -2 ** 31
NEG_INF = float("-inf")
NEG_BIAS = -1e30
M_INIT = -1e29
VMEM_LIMIT = 56 * 1024 * 1024
MAX_ROW_TILE = 768

C_Q, C_K, C_V, C_QI, C_MQ, C_KI, C_WI, C_U = 0, 768, 1024, 1280, 1536, 2048, 2112, 2176
Z_COLS = 3072


def _cparams(sem):
    return pltpu.CompilerParams(dimension_semantics=sem, vmem_limit_bytes=VMEM_LIMIT)


def _row_tile(m):
    return max(t for t in range(BF16_SUBLANES, MAX_ROW_TILE + 1, BF16_SUBLANES) if m % t == 0)


def _pack_kernel(w_ref, wp_ref, wg_ref, *, segs, zero_spans, o_g):
    for src, width, dst in segs:
        wp_ref[:, dst:dst + width] = w_ref[:, src:src + width].astype(BF16)
    for start, stop in zero_spans:
        wp_ref[:, start:stop] = jnp.zeros((wp_ref.shape[0], stop - start), BF16)
    wg_ref[...] = w_ref[:, o_g:].astype(BF16)


def _pack_w_in(w_in, l):
    _, d, cols = w_in.shape
    aw, kvw, iqw, mw = N_HEADS * HEAD_DIM, N_KV_HEADS * HEAD_DIM, IDX_HEADS * IDX_DIM, MEM_HEADS * HEAD_DIM
    o_ki = aw + 2 * kvw + iqw
    o_u = o_ki + IDX_DIM + IDX_HEADS
    uw = cols - o_u - mw - 3 * d
    o_mq = o_u + uw
    o_g = o_mq + mw
    segs = ((0, o_ki, C_Q), (o_mq, mw, C_MQ), (o_ki, o_u - o_ki, C_KI), (o_u, uw, C_U))
    zero_spans = ((C_WI + IDX_HEADS, C_U), (C_U + uw, Z_COLS))
    tk = 256
    wp, wg = pl.pallas_call(
        functools.partial(_pack_kernel, segs=segs, zero_spans=zero_spans, o_g=o_g),
        grid=(d // tk,),
        in_specs=[pl.BlockSpec((None, tk, cols), lambda i: (l, i, 0))],
        out_specs=[pl.BlockSpec((tk, Z_COLS), lambda i: (i, 0)), pl.BlockSpec((tk, 3 * d), lambda i: (i, 0))],
        out_shape=[jax.ShapeDtypeStruct((d, Z_COLS), BF16), jax.ShapeDtypeStruct((d, 3 * d), BF16)],
        compiler_params=_cparams(("arbitrary",)),
        name="pack_w_in",
    )(w_in)
    return wp, wg, uw


def _epilogue_rows(qn, kn, mqn):
    aw, kvw, iqw, mw = N_HEADS * HEAD_DIM, N_KV_HEADS * HEAD_DIM, IDX_HEADS * IDX_DIM, MEM_HEADS * HEAD_DIM
    ones = lambda n: jnp.ones((n,), F32)
    zer = lambda n: jnp.zeros((n,), F32)
    flag = jnp.concatenate([ones(aw + kvw), zer(kvw + iqw), ones(mw), zer(Z_COLS - C_KI)])[None]
    gain = jnp.concatenate([
        jnp.tile(qn, N_HEADS), jnp.tile(kn, N_KV_HEADS), ones(kvw + iqw), jnp.tile(mqn, MEM_HEADS),
        ones(IDX_DIM), jnp.full((IDX_HEADS,), IDX_HEADS ** -0.5, F32), ones(Z_COLS - C_WI - IDX_HEADS)])[None]
    return flag, gain


def _norm_matmul_kernel(x_ref, g_ref, w_ref, flag_ref, gain_ref, z_ref, *rest, emit_h):
    if emit_h:
        h_ref, hs_ref = rest
    else:
        (hs_ref,) = rest
    j = pl.program_id(1)

    @pl.when(j == 0)
    def _():
        x = x_ref[...]
        r = lax.rsqrt(jnp.mean(x * x, axis=-1, keepdims=True) + EPS)
        h = ((x * r) * g_ref[...]).astype(BF16)
        hs_ref[...] = h
        if emit_h:
            h_ref[...] = h

    acc = jnp.dot(hs_ref[...], w_ref[...], preferred_element_type=F32)
    for c in range(acc.shape[1] // LANES):
        sl = slice(c * LANES, (c + 1) * LANES)
        y = acc[:, sl]
        r = lax.rsqrt(jnp.mean(y * y, axis=-1, keepdims=True) + EPS)
        scale = jnp.where(flag_ref[:, sl] > 0.0, r, 1.0) * gain_ref[:, sl]
        z_ref[:, sl] = y * scale


def _norm_matmul(x, g, w, flag, gain, *, tm, tn, emit_h):
    m, d = x.shape
    n = w.shape[1]
    out_shape = [jax.ShapeDtypeStruct((m, n), F32)]
    out_specs = [pl.BlockSpec((tm, tn), lambda i, j: (i, j))]
    if emit_h:
        out_shape.append(jax.ShapeDtypeStruct((m, d), BF16))
        out_specs.append(pl.BlockSpec((tm, d), lambda i, j: (i, 0)))
    res = pl.pallas_call(
        functools.partial(_norm_matmul_kernel, emit_h=emit_h),
        grid=(m // tm, n // tn),
        in_specs=[
            pl.BlockSpec((tm, d), lambda i, j: (i, 0)),
            pl.BlockSpec((1, d), lambda i, j: (0, 0)),
            pl.BlockSpec((d, tn), lambda i, j: (0, j)),
            pl.BlockSpec((1, tn), lambda i, j: (0, j)),
            pl.BlockSpec((1, tn), lambda i, j: (0, j)),
        ],
        out_specs=out_specs,
        out_shape=out_shape,
        scratch_shapes=[pltpu.VMEM((tm, d), BF16)],
        compiler_params=_cparams(("arbitrary", "arbitrary")),
        name="norm_matmul",
    )(x, g, w, flag, gain)
    return res if emit_h else res[0]


def _ordered_to_f32(key):
    return pltpu.bitcast(jnp.where(key >= 0, key, key ^ 0x7FFFFFFF), F32)


def _count(score_ref, n_slab, slab_w, indicator):
    rows = score_ref.shape[0]

    def slab(s, acc):
        static = isinstance(s, int)
        base = s * slab_w if static else pl.multiple_of(s * slab_w, slab_w)
        parts = []
        for c in range(slab_w // LANES):
            off = base + c * LANES
            x = score_ref[:, off:off + LANES] if static else score_ref[:, pl.ds(off, LANES)]
            parts.append(indicator(x, off))
        while len(parts) > 1:
            parts = [a + b for a, b in zip(parts[0::2], parts[1::2])] + ([parts[-1]] if len(parts) % 2 else [])
        return acc + parts[0]

    acc = jnp.zeros((rows, LANES), F32)
    if isinstance(n_slab, int):
        for s in range(n_slab):
            acc = slab(s, acc)
    else:
        acc = lax.fori_loop(0, n_slab, slab, acc)
    return jnp.sum(acc, axis=1, keepdims=True)


def _kth_largest(score_ref, n_slab, slab_w, k):
    rows = score_ref.shape[0]
    count_ge = lambda cand: _count(score_ref, n_slab, slab_w, lambda x, off: jnp.where(x >= cand, 1.0, 0.0))
    lo = jnp.where(count_ge(jnp.zeros((rows, 1), F32)) >= k, 0, KEY_MIN)

    def body(it, lo):
        cand = lo | (1 << (30 - it))
        return jnp.where(count_ge(_ordered_to_f32(cand)) >= k, cand, lo)

    lo = lax.fori_loop(0, 31, body, lo)
    return jnp.where(lo == KEY_MIN, NEG_INF, _ordered_to_f32(lo))


def _stack_heads(x, first, count, width):
    return jnp.concatenate([x[:, (first + r) * width:(first + r + 1) * width] for r in range(count)], axis=0)


def _indexer_scores(qi_stack, w_cols, ki_c, rows, keys_on_lanes=False):
    dims = (((1,), (0,)), ((), ())) if keys_on_lanes else (((1,), (1,)), ((), ()))
    logits = lax.dot_general(qi_stack, ki_c, dims, preferred_element_type=F32)
    score = jnp.maximum(logits[0:rows], 0.0) * w_cols[0]
    for h in range(1, IDX_HEADS):
        score = score + jnp.maximum(logits[h * rows:(h + 1) * rows], 0.0) * w_cols[h]
    return score


TQ = 128
KC = 512
KC_ATT = 1024
ROW_SLAB = KC
LOG2_E = math.log2(math.e)


def _count_keys(score_ref, n_slab, indicator):
    lanes = score_ref.shape[1]

    def slab(s, acc):
        base = pl.multiple_of(s * ROW_SLAB, ROW_SLAB)
        parts = [indicator(score_ref[pl.ds(base + r * SUBLANES, SUBLANES), :]) for r in range(ROW_SLAB // SUBLANES)]
        while len(parts) > 1:
            parts = [a + b for a, b in zip(parts[0::2], parts[1::2])] + ([parts[-1]] if len(parts) % 2 else [])
        return acc + parts[0]

    acc = lax.fori_loop(0, n_slab, slab, jnp.zeros((SUBLANES, lanes), F32))
    return jnp.sum(acc, axis=0, keepdims=True)


def _kth_largest_keys(score_ref, n_slab, k):
    lanes = score_ref.shape[1]
    count_ge = lambda cand: _count_keys(score_ref, n_slab, lambda x: jnp.where(x >= cand, 1.0, 0.0))
    lo = jnp.where(count_ge(jnp.zeros((1, lanes), F32)) >= k, 0, KEY_MIN)

    def body(it, lo):
        cand = lo | (1 << (30 - it))
        return jnp.where(count_ge(_ordered_to_f32(cand)) >= k, cand, lo)

    lo = lax.fori_loop(0, 31, body, lo)
    return jnp.where(lo == KEY_MIN, NEG_INF, _ordered_to_f32(lo))


def _bias_keys(score_ref, bias_ref, n_slab, k, thr):
    lanes = score_ref.shape[1]
    need = k - _count_keys(score_ref, n_slab, lambda x: jnp.where(x > thr, 1.0, 0.0))
    tri = (lax.broadcasted_iota(I32, (LANES, LANES), 0)
           >= lax.broadcasted_iota(I32, (LANES, LANES), 1)).astype(BF16)

    def body(s, seen):
        base = pl.multiple_of(s * ROW_SLAB, ROW_SLAB)
        xs = [score_ref[pl.ds(base + j * LANES, LANES), :] for j in range(ROW_SLAB // LANES)]
        eqs = [jnp.where(x == thr, 1.0, 0.0) for x in xs]
        prefixes = [jnp.dot(tri, e.astype(BF16), preferred_element_type=F32) for e in eqs]
        for j, (x, eqf, prefix) in enumerate(zip(xs, eqs, prefixes)):
            keep = jnp.where(x > thr, 1.0, jnp.where(seen + prefix <= need, eqf, 0.0))
            bias_ref[pl.ds(base + j * LANES, LANES), :] = jnp.where(
                keep > 0.0, jnp.where(x > NEG_INF, 0.0, NEG_BIAS), NEG_BIAS)
            seen = seen + prefix[LANES - 1:LANES, :]
        return seen

    lax.fori_loop(0, n_slab, body, jnp.zeros((1, lanes), F32))


def _dsa_prompt_kernel(q_ref, qi_ref, kiwq_ref, k_ref, v_ref, kiw_ref, o_ref,
                       kbf_ref, vt_ref, kibf_ref, score_ref, bias_ref, *, topk, kc_att):
    i = pl.program_id(1)
    seq = k_ref.shape[0]

    @pl.when(i == 0)
    def _():
        kbf_ref[...] = k_ref[...].astype(BF16)
        kibf_ref[...] = kiw_ref[...].astype(BF16)
        for c in range(seq // KC):
            vt_ref[:, c * KC:(c + 1) * KC] = v_ref[c * KC:(c + 1) * KC, :].T.astype(BF16)

    n_kc = (i * TQ + TQ + KC - 1) // KC
    n_att = (n_kc * KC + kc_att - 1) // kc_att
    qi_stack = _stack_heads(qi_ref[...], 0, IDX_HEADS, IDX_DIM).astype(BF16)
    w_t = kiwq_ref[...].T
    scale_i = IDX_DIM ** -0.5
    w_rows = [w_t[IDX_DIM + h:IDX_DIM + h + 1, :] * scale_i for h in range(IDX_HEADS)]
    qpos = i * TQ + lax.broadcasted_iota(I32, (KC, TQ), 1)

    def score_body(c, _):
        off = pl.multiple_of(c * KC, KC)
        logits = lax.dot_general(kibf_ref[pl.ds(off, KC), 0:IDX_DIM], qi_stack, (((1,), (1,)), ((), ())),
                                 preferred_element_type=F32)
        score = jnp.maximum(logits[:, 0:TQ], 0.0) * w_rows[0]
        for h in range(1, IDX_HEADS):
            score = score + jnp.maximum(logits[:, h * TQ:(h + 1) * TQ], 0.0) * w_rows[h]
        kpos = off + lax.broadcasted_iota(I32, (KC, TQ), 0)
        score_ref[pl.ds(off, KC), :] = jnp.where(kpos > qpos, NEG_INF, score)
        return 0

    lax.fori_loop(0, n_kc, score_body, 0)
    thr = _kth_largest_keys(score_ref, n_kc, topk)
    _bias_keys(score_ref, bias_ref, n_kc, topk, thr)

    @pl.when(n_att * kc_att > n_kc * KC)
    def _():
        bias_ref[pl.ds(pl.multiple_of(n_kc * KC, KC), KC), :] = jnp.full((KC, TQ), NEG_BIAS, F32)

    q = q_ref[...] * (HEAD_DIM ** -0.5 * LOG2_E)
    q_gs = [_stack_heads(q, g * KV_GROUP, KV_GROUP, HEAD_DIM).astype(BF16) for g in range(N_KV_HEADS)]
    gq = KV_GROUP * TQ

    def att_body(c, carries):
        off = pl.multiple_of(c * kc_att, kc_att)
        b = bias_ref[pl.ds(off, kc_att), :]
        bias = jnp.concatenate([b] * KV_GROUP, axis=1)
        out = []
        for g in range(N_KV_HEADS):
            gs = slice(g * HEAD_DIM, (g + 1) * HEAD_DIM)
            m, l, acc = carries[g]
            s = lax.dot_general(kbf_ref[pl.ds(off, kc_att), gs], q_gs[g], (((1,), (1,)), ((), ())),
                                preferred_element_type=F32) + bias
            m_new = jnp.maximum(m, jnp.max(s, axis=0, keepdims=True))
            alpha = jnp.exp2(m - m_new)
            p = jnp.exp2(s - m_new)
            l = alpha * l + jnp.sum(p, axis=0, keepdims=True)
            acc = alpha * acc + jnp.dot(vt_ref[gs, pl.ds(off, kc_att)], p.astype(BF16),
                                        preferred_element_type=F32)
            out.append((m_new, l, acc))
        return tuple(out)

    init = (jnp.full((1, gq), M_INIT, F32), jnp.zeros((1, gq), F32), jnp.zeros((HEAD_DIM, gq), F32))
    carries = lax.fori_loop(0, n_att, att_body, (init,) * N_KV_HEADS)
    for g, (_, l, acc) in enumerate(carries):
        o_t = acc / l
        for r in range(KV_GROUP):
            hh = g * KV_GROUP + r
            o_ref[:, hh * HEAD_DIM:(hh + 1) * HEAD_DIM] = o_t[:, r * TQ:(r + 1) * TQ].T


def _dsa_prompt(z, batch, seq):
    nq = seq // TQ
    topk = min(TOPK_MAX, seq // 4)
    aw, kvw, iqw = N_HEADS * HEAD_DIM, N_KV_HEADS * HEAD_DIM, IDX_HEADS * IDX_DIM
    return pl.pallas_call(
        functools.partial(_dsa_prompt_kernel, topk=topk, kc_att=min(KC_ATT, seq)),
        grid=(batch, nq),
        in_specs=[
            pl.BlockSpec((TQ, aw), lambda b, i: (b * nq + i, C_Q // aw)),
            pl.BlockSpec((TQ, iqw), lambda b, i: (b * nq + i, C_QI // iqw)),
            pl.BlockSpec((TQ, LANES), lambda b, i: (b * nq + i, C_KI // LANES)),
            pl.BlockSpec((seq, kvw), lambda b, i: (b, C_K // kvw)),
            pl.BlockSpec((seq, kvw), lambda b, i: (b, C_V // kvw)),
            pl.BlockSpec((seq, LANES), lambda b, i: (b, C_KI // LANES)),
        ],
        out_specs=pl.BlockSpec((TQ, aw), lambda b, i: (b * nq + i, 0)),
        out_shape=jax.ShapeDtypeStruct((batch * seq, aw), F32),
        scratch_shapes=[
            pltpu.VMEM((seq, kvw), BF16),
            pltpu.VMEM((kvw, seq), BF16),
            pltpu.VMEM((seq, LANES), BF16),
            pltpu.VMEM((seq, TQ), F32),
            pltpu.VMEM((seq, TQ), F32),
        ],
        compiler_params=_cparams(("arbitrary", "arbitrary")),
        name="dsa_prompt",
    )(z, z, z, z, z, z)


KC_S = 1024


def _tie_cut(score_ref, n128, k, thr):
    rows = score_ref.shape[0]
    width = n128 * LANES
    need = k - _count(score_ref, 1, width, lambda x, off: jnp.where(x > thr, 1.0, 0.0))
    ties = _count(score_ref, 1, width, lambda x, off: jnp.where(x == thr, 1.0, 0.0))
    lane = lax.broadcasted_iota(I32, (rows, LANES), 1)
    everything = jnp.full((rows, 1), width - 1, I32)

    def bisect():
        def ties_upto(j):
            return _count(score_ref, 1, width,
                          lambda x, off: jnp.where(x == thr, jnp.where(lane + off <= j, 1.0, 0.0), 0.0))

        def body(_, c):
            lo, hi = c
            mid = (lo + hi) >> 1
            ok = ties_upto(mid) >= need
            return jnp.where(ok, lo, mid), jnp.where(ok, mid, hi)

        steps = int(math.ceil(math.log2(width + 1)))
        return lax.fori_loop(0, steps, body, (jnp.full((rows, 1), -1, I32), everything))[1]

    return lax.cond(jnp.max(ties - need) > 0.0, bisect, lambda: everything)


def _write_bias(score_ref, bias_ref, n128, thr, cut):
    rows = score_ref.shape[0]
    lane = lax.broadcasted_iota(I32, (rows, LANES), 1)
    for c in range(n128):
        x = score_ref[:, c * LANES:(c + 1) * LANES]
        tie = jnp.where(x == thr, jnp.where(lane + c * LANES <= cut, 1.0, 0.0), 0.0)
        keep = jnp.where(x > thr, 1.0, tie)
        bias_ref[:, c * LANES:(c + 1) * LANES] = jnp.where(keep > 0.0, jnp.where(x > NEG_INF, 0.0, NEG_BIAS),
                                                           NEG_BIAS)


def _lane_fold(x, op):
    parts = [x[:, c * LANES:(c + 1) * LANES] for c in range(x.shape[1] // LANES)]
    while len(parts) > 1:
        parts = [op(a, b) for a, b in zip(parts[0::2], parts[1::2])] + ([parts[-1]] if len(parts) % 2 else [])
    return parts[0]


def _dsa_sample_kernel(pt_ref, q_ref, qi_ref, kiw_ref, kn_ref, vn_ref, ck_ref, cv_ref, cki_ref, o_ref,
                       kbuf, vbuf, kibuf, score_ref, bias_ref, s_ref, sem, *, layer, n_pages, page, tq, topk):
    b = pl.program_id(0)
    nb = pl.num_programs(0)
    past = n_pages * page
    slot = b % 2
    streams = ((ck_ref, kbuf, page * N_KV_HEADS), (cv_ref, vbuf, page * N_KV_HEADS), (cki_ref, kibuf, IDX_DIM))

    def page_copy(which, bb, sl, p):
        src, dst, rpp = streams[which]
        return pltpu.make_async_copy(src.at[layer, pt_ref[bb * n_pages + p]],
                                     dst.at[sl, pl.ds(p * rpp, rpp)], sem.at[which, sl])

    def start_fetch(bb, sl):
        def body(p, _):
            for which in range(len(streams)):
                page_copy(which, bb, sl, p).start()
            return 0
        lax.fori_loop(0, n_pages, body, 0)

    def wait_fetch(bb, sl, which):
        def body(p, _):
            page_copy(which, bb, sl, p).wait()
            return 0
        lax.fori_loop(0, n_pages, body, 0)

    @pl.when(b == 0)
    def _():
        start_fetch(0, 0)

    @pl.when(b + 1 < nb)
    def _():
        start_fetch(b + 1, 1 - slot)

    qi_stack = _stack_heads(qi_ref[...], 0, IDX_HEADS, IDX_DIM).astype(BF16)
    kiw = kiw_ref[...]
    scale_i = IDX_DIM ** -0.5
    w_cols = [kiw[:, IDX_DIM + h:IDX_DIM + h + 1] * scale_i for h in range(IDX_HEADS)]
    n128 = past // LANES + 1
    ppc = KC // page

    wait_fetch(b, slot, 2)
    for c in range(past // KC):
        ki_t = jnp.concatenate([kibuf[slot, (c * ppc + j) * IDX_DIM:(c * ppc + j + 1) * IDX_DIM, :]
                                for j in range(ppc)], axis=1).astype(BF16)
        score_ref[:, c * KC:(c + 1) * KC] = _indexer_scores(qi_stack, w_cols, ki_t, tq, keys_on_lanes=True)
    pad = jnp.zeros((LANES - tq, LANES), F32)
    ki_new = jnp.concatenate([kiw, pad], axis=0)[:, 0:IDX_DIM].astype(BF16)
    score_new = _indexer_scores(qi_stack, w_cols, ki_new, tq)
    newer = lax.broadcasted_iota(I32, (tq, LANES), 1) > lax.broadcasted_iota(I32, (tq, LANES), 0)
    score_ref[:, past:past + LANES] = jnp.where(newer, NEG_INF, score_new)

    thr = _kth_largest(score_ref, 1, n128 * LANES, topk)
    _write_bias(score_ref, bias_ref, n128, thr, _tie_cut(score_ref, n128, topk, thr))

    wait_fetch(b, slot, 0)
    wait_fetch(b, slot, 1)
    q = q_ref[...] * (HEAD_DIM ** -0.5)
    q_gs = [_stack_heads(q, g * KV_GROUP, KV_GROUP, HEAD_DIM).astype(BF16) for g in range(N_KV_HEADS)]
    kv_pad = jnp.zeros((LANES - tq, N_KV_HEADS * HEAD_DIM), F32)
    k_new = jnp.concatenate([kn_ref[...], kv_pad], axis=0).astype(BF16)
    v_new = jnp.concatenate([vn_ref[...], kv_pad], axis=0).astype(BF16)
    chunks = [(c * KC_S, KC_S) for c in range(past // KC_S)] + [(past, LANES)]

    def keys_of(buf, new, g, start, size):
        if start == past:
            return new[:, g * HEAD_DIM:(g + 1) * HEAD_DIM]
        return buf[slot, pl.ds(start * N_KV_HEADS + g, size, stride=N_KV_HEADS), :].astype(BF16)

    m_acc = [jnp.full((KV_GROUP * tq, LANES), M_INIT, F32) for _ in range(N_KV_HEADS)]
    for start, size in chunks:
        bias = jnp.concatenate([bias_ref[:, start:start + size]] * KV_GROUP, axis=0)
        for g in range(N_KV_HEADS):
            s = lax.dot_general(q_gs[g], keys_of(kbuf, k_new, g, start, size), (((1,), (1,)), ((), ())),
                                preferred_element_type=F32) + bias
            s_ref[g, :, start:start + size] = s
            m_acc[g] = jnp.maximum(m_acc[g], _lane_fold(s, jnp.maximum))
    m = [jnp.max(a, axis=1, keepdims=True) for a in m_acc]
    l_acc = [jnp.zeros((KV_GROUP * tq, LANES), F32) for _ in range(N_KV_HEADS)]
    acc = [jnp.zeros((KV_GROUP * tq, HEAD_DIM), F32) for _ in range(N_KV_HEADS)]
    for start, size in chunks:
        for g in range(N_KV_HEADS):
            p = jnp.exp(s_ref[g, :, start:start + size] - m[g])
            l_acc[g] = l_acc[g] + _lane_fold(p, jnp.add)
            acc[g] = acc[g] + jnp.dot(p.astype(BF16), keys_of(vbuf, v_new, g, start, size),
                                      preferred_element_type=F32)
    for g in range(N_KV_HEADS):
        o = acc[g] / jnp.sum(l_acc[g], axis=1, keepdims=True)
        for r in range(KV_GROUP):
            hh = g * KV_GROUP + r
            o_ref[:, hh * HEAD_DIM:(hh + 1) * HEAD_DIM] = o[r * tq:(r + 1) * tq]


def _dsa_sample(z, row0, nb, tq, layer, cache_k, cache_v, cache_kidx, page_table):
    depth, n_pool, page = cache_k.shape[:3]
    n_pages = page_table.shape[1]
    past = n_pages * page
    topk = min(TOPK_MAX, (past + tq) // 4)
    aw, kvw, iqw = N_HEADS * HEAD_DIM, N_KV_HEADS * HEAD_DIM, IDX_HEADS * IDX_DIM
    rb = row0 // tq
    grid_spec = pltpu.PrefetchScalarGridSpec(
        num_scalar_prefetch=1,
        grid=(nb,),
        in_specs=[
            pl.BlockSpec((tq, aw), lambda b, pt: (rb + b, C_Q // aw)),
            pl.BlockSpec((tq, iqw), lambda b, pt: (rb + b, C_QI // iqw)),
            pl.BlockSpec((tq, LANES), lambda b, pt: (rb + b, C_KI // LANES)),
            pl.BlockSpec((tq, kvw), lambda b, pt: (rb + b, C_K // kvw)),
            pl.BlockSpec((tq, kvw), lambda b, pt: (rb + b, C_V // kvw)),
            pl.BlockSpec(memory_space=pl.ANY),
            pl.BlockSpec(memory_space=pl.ANY),
            pl.BlockSpec(memory_space=pl.ANY),
        ],
        out_specs=pl.BlockSpec((tq, aw), lambda b, pt: (b, 0)),
        scratch_shapes=[
            pltpu.VMEM((2, past * N_KV_HEADS, HEAD_DIM), F32),
            pltpu.VMEM((2, past * N_KV_HEADS, HEAD_DIM), F32),
            pltpu.VMEM((2, n_pages * IDX_DIM, page), F32),
            pltpu.VMEM((tq, past + LANES), F32),
            pltpu.VMEM((tq, past + LANES), F32),
            pltpu.VMEM((N_KV_HEADS, KV_GROUP * tq, past + LANES), F32),
            pltpu.SemaphoreType.DMA((3, 2)),
        ],
    )
    ck = cache_k.reshape(depth, n_pool, page * N_KV_HEADS, HEAD_DIM)
    cv = cache_v.reshape(depth, n_pool, page * N_KV_HEADS, HEAD_DIM)
    return pl.pallas_call(
        functools.partial(_dsa_sample_kernel, layer=layer, n_pages=n_pages, page=page, tq=tq, topk=topk),
        grid_spec=grid_spec,
        out_shape=jax.ShapeDtypeStruct((nb * tq, aw), F32),
        compiler_params=_cparams(("arbitrary",)),
        name="dsa_sample",
    )(page_table.reshape(-1), z, z, z, z, z, ck, cv, jnp.swapaxes(cache_kidx, 2, 3))


def _mem_attn_kernel(q_ref, k_ref, v_ref, o_ref):
    q = q_ref[...] * (HEAD_DIM ** -0.5)
    m_tok = k_ref.shape[0] // MEM_HEADS
    for h in range(MEM_HEADS):
        hs = slice(h * HEAD_DIM, (h + 1) * HEAD_DIM)
        rows = pl.ds(h, m_tok, stride=MEM_HEADS)
        s = lax.dot_general(q[:, hs].astype(BF16), k_ref[rows, :].astype(BF16),
                            (((1,), (1,)), ((), ())), preferred_element_type=F32)
        m = jnp.max(s, axis=1, keepdims=True)
        p = jnp.exp(s - m)
        l = jnp.sum(p, axis=1, keepdims=True)
        o = jnp.dot(p.astype(BF16), v_ref[rows, :].astype(BF16), preferred_element_type=F32)
        o_ref[:, hs] = o / l


def _mem_attn(z, row0, nb, t, tq, mk, mv, layer):
    mw = MEM_HEADS * HEAD_DIM
    nq = t // tq
    rb = row0 // tq
    kv_spec = pl.BlockSpec((None, None) + mk.shape[2:], lambda b, i: (layer, b, 0, 0))
    return pl.pallas_call(
        _mem_attn_kernel,
        grid=(nb, nq),
        in_specs=[pl.BlockSpec((tq, mw), lambda b, i: (rb + b * nq + i, C_MQ // mw)), kv_spec, kv_spec],
        out_specs=pl.BlockSpec((tq, mw), lambda b, i: (b * nq + i, 0)),
        out_shape=jax.ShapeDtypeStruct((nb * t, mw), F32),
        compiler_params=_cparams(("arbitrary", "arbitrary")),
        name="mem_attn",
    )(z, mk, mv)


def _shift_rows(x, sh):
    rows = lax.broadcasted_iota(I32, x.shape, 0)
    return jnp.where(rows >= sh, pltpu.roll(x, sh, axis=0), 0.0)


def _ssm_pair(u, m_w, wx_w, wc_w, pr, pi, s0, nb, nc):
    half = SSM_PAIR_W // 2
    y_in = jnp.dot(u, m_w, preferred_element_type=F32)
    x = jnp.dot(u, wx_w, preferred_element_type=F32)
    xr, xi = x[:, :half], x[:, half:]
    s0r, s0i = s0[:, :half], s0[:, half:]
    ar, ai = pr[0:1], pi[0:1]
    if nc == 1:
        sr = xr + ar * s0r - ai * s0i
        si = xi + ar * s0i + ai * s0r
        pvr, pvi = s0r, s0i
        sfin = jnp.concatenate([sr, si], axis=1)
    else:
        prev_r, prev_i, fin = [], [], []
        first = lax.broadcasted_iota(I32, (nc, half), 0) == 0
        for b in range(nb):
            rs = slice(b * nc, (b + 1) * nc)
            b0r, b0i = s0r[b:b + 1], s0i[b:b + 1]
            sr = xr[rs] + jnp.where(first, ar * b0r - ai * b0i, 0.0)
            si = xi[rs] + jnp.where(first, ar * b0i + ai * b0r, 0.0)
            for mstep in range(int(math.log2(nc))):
                sh = 2 ** mstep
                qr, qi = pr[mstep:mstep + 1], pi[mstep:mstep + 1]
                tr, ti = _shift_rows(sr, sh), _shift_rows(si, sh)
                sr, si = sr + qr * tr - qi * ti, si + qr * ti + qi * tr
            fin.append(jnp.concatenate([sr[nc - 1:nc], si[nc - 1:nc]], axis=1))
            prev_r.append(jnp.where(first, b0r, pltpu.roll(sr, 1, axis=0)))
            prev_i.append(jnp.where(first, b0i, pltpu.roll(si, 1, axis=0)))
        pvr = jnp.concatenate(prev_r, axis=0)
        pvi = jnp.concatenate(prev_i, axis=0)
        sfin = jnp.concatenate(fin, axis=0)
    prev = jnp.concatenate([pvr, pvi], axis=1).astype(BF16)
    return y_in + jnp.dot(prev, wc_w, preferred_element_type=F32), sfin


def _ssm_kernel(u_ref, m_ref, wx_ref, wc_ref, pr_ref, pi_ref, s0_ref, y_ref, sf_ref, *, nb, nc):
    nch = nb * nc
    gw = SSM_GROUP
    tok = [u_ref[pl.ds(i, nch, stride=SSM_CHUNK), :] for i in range(SSM_CHUNK)]
    ys = []
    for q in range(PAIRS_PER_BLOCK):
        base = q * 2 * gw
        u = jnp.concatenate([tok[i][:, base + gg * gw:base + (gg + 1) * gw]
                             for gg in range(2) for i in range(SSM_CHUNK)], axis=1).astype(BF16)
        y, sfin = _ssm_pair(u, m_ref[q], wx_ref[q], wc_ref[q], pr_ref[q], pi_ref[q], s0_ref[q], nb, nc)
        sf_ref[q] = sfin
        ys.append(y)
    for i in range(SSM_CHUNK):
        y_ref[pl.ds(i, nch, stride=SSM_CHUNK), :] = jnp.concatenate(
            [ys[q][:, (gg * SSM_CHUNK + i) * gw:(gg * SSM_CHUNK + i + 1) * gw]
             for q in range(PAIRS_PER_BLOCK) for gg in range(2)], axis=1)


def _ssm(z, row0, rows, mats, s0_pairs, nb, nc):
    m_mat, wx, wc, pr, pi = mats
    npair, _, w = m_mat.shape
    pw = pr.shape[1]
    ppb = PAIRS_PER_BLOCK
    blk3 = lambda r, c: pl.BlockSpec((ppb, r, c), lambda j: (j, 0, 0))
    return pl.pallas_call(
        functools.partial(_ssm_kernel, nb=nb, nc=nc),
        grid=(npair // ppb,),
        in_specs=[pl.BlockSpec((rows, LANES), lambda j: (row0 // rows, C_U // LANES + j)),
                  blk3(w, w), blk3(w, w), blk3(w, w), blk3(pw, LANES), blk3(pw, LANES), blk3(nb, w)],
        out_specs=[pl.BlockSpec((rows, LANES), lambda j: (0, j)), blk3(nb, w)],
        out_shape=[jax.ShapeDtypeStruct((rows, npair * 2 * SSM_GROUP), F32),
                   jax.ShapeDtypeStruct((npair, nb, w), F32)],
        compiler_params=_cparams(("arbitrary",)),
        name="ssm",
    )(z, m_mat, wx, wc, pr, pi, s0_pairs)


def _ssm_matrices(lam_re, lam_im, log_dt, b_re, b_im, c_re, c_im, d_skip, n_steps):
    g, p, c = b_re.shape
    L = SSM_CHUNK
    dt = jnp.exp(log_dt)[:, None]
    zr, zi = lam_re * dt, lam_im * dt
    mag = jnp.exp(zr)
    ar, ai = mag * jnp.cos(zi), mag * jnp.sin(zi)
    den = lam_re * lam_re + lam_im * lam_im
    cr = ((ar - 1.0) * lam_re + ai * lam_im) / den
    ci = (ai * lam_re - (ar - 1.0) * lam_im) / den
    bbr = cr[..., None] * b_re - ci[..., None] * b_im
    bbi = cr[..., None] * b_im + ci[..., None] * b_re
    n = jnp.arange(L + 1, dtype=F32)[:, None, None]
    pm = jnp.exp(zr[None] * n)
    pwr, pwi = pm * jnp.cos(zi[None] * n), pm * jnp.sin(zi[None] * n)
    car = c_re[None] * pwr[:, :, None, :] - c_im[None] * pwi[:, :, None, :]
    cai = c_re[None] * pwi[:, :, None, :] + c_im[None] * pwr[:, :, None, :]
    kern = jnp.einsum('ngop,gpc->ngoc', car, bbr) - jnp.einsum('ngop,gpc->ngoc', cai, bbi)
    lag = jnp.arange(L)[None, :] - jnp.arange(L)[:, None]
    toe = jnp.where((lag >= 0)[:, :, None, None, None], kern[jnp.clip(lag, 0, L)], 0.0)
    eye = (lag == 0)[:, :, None, None, None] * (jnp.eye(c)[None, None, None] * d_skip[None, None, :, :, None])
    toe = toe + eye
    m_g = jnp.transpose(toe, (2, 0, 4, 1, 3)).reshape(g, L * c, L * c)
    rev = pwr[L - 1 - jnp.arange(L)], pwi[L - 1 - jnp.arange(L)]
    wxr = rev[0][..., None] * bbr[None] - rev[1][..., None] * bbi[None]
    wxi = rev[0][..., None] * bbi[None] + rev[1][..., None] * bbr[None]
    wxr = jnp.transpose(wxr, (1, 0, 3, 2)).reshape(g, L * c, p)
    wxi = jnp.transpose(wxi, (1, 0, 3, 2)).reshape(g, L * c, p)
    wcr = jnp.transpose(car[1:], (1, 3, 0, 2)).reshape(g, p, L * c)
    wci = -jnp.transpose(cai[1:], (1, 3, 0, 2)).reshape(g, p, L * c)

    def pair_blockdiag(a):
        a = a.reshape(g // 2, 2, a.shape[1], a.shape[2])
        z = jnp.zeros_like(a[:, 0])
        return jnp.concatenate([jnp.concatenate([a[:, 0], z], axis=2),
                                jnp.concatenate([z, a[:, 1]], axis=2)], axis=1)

    m_pair = pair_blockdiag(m_g)
    wx_pair = jnp.concatenate([pair_blockdiag(wxr), pair_blockdiag(wxi)], axis=2)
    wc_pair = jnp.concatenate([pair_blockdiag(wcr), pair_blockdiag(wci)], axis=1)
    qr, qi = [pwr[L]], [pwi[L]]
    for _ in range(n_steps - 1):
        qr, qi = qr + [qr[-1] * qr[-1] - qi[-1] * qi[-1]], qi + [2.0 * qr[-1] * qi[-1]]
    rows = -(-n_steps // SUBLANES) * SUBLANES
    padz = [jnp.zeros_like(qr[0])] * (rows - n_steps)
    pr = jnp.transpose(jnp.stack(qr + padz), (1, 0, 2)).reshape(g // 2, 2, rows, p)
    pi = jnp.transpose(jnp.stack(qi + padz), (1, 0, 2)).reshape(g // 2, 2, rows, p)
    pr = jnp.concatenate([pr[:, 0], pr[:, 1]], axis=2)
    pi = jnp.concatenate([pi[:, 0], pi[:, 1]], axis=2)
    return m_pair.astype(BF16), wx_pair.astype(BF16), wc_pair.astype(BF16), pr, pi


def _state_to_pairs(sr, si):
    nb, g, p = sr.shape
    f = lambda s: jnp.transpose(s.reshape(nb, g // 2, 2 * p), (1, 0, 2))
    return jnp.concatenate([f(sr), f(si)], axis=2)


def _state_from_pairs(s):
    npair, nb, w = s.shape
    f = lambda a: jnp.transpose(a, (1, 0, 2)).reshape(nb, npair * 2, w // 4)
    return f(s[:, :, :w // 2]), f(s[:, :, w // 2:])


def _glu_kernel(y_ref, w_ref, o_ref):
    y = jax.nn.gelu(y_ref[...])
    gate = jnp.dot(y.astype(BF16), w_ref[...], preferred_element_type=F32)
    o_ref[...] = y * jax.nn.sigmoid(gate)


def _glu(y, w, tm):
    m, n = y.shape
    return pl.pallas_call(
        _glu_kernel,
        grid=(m // tm,),
        in_specs=[pl.BlockSpec((tm, n), lambda i: (i, 0)), pl.BlockSpec((n, n), lambda i: (0, 0))],
        out_specs=pl.BlockSpec((tm, n), lambda i: (i, 0)),
        out_shape=jax.ShapeDtypeStruct((m, n), F32),
        compiler_params=_cparams(("arbitrary",)),
        name="glu",
    )(y, w)


def _merge_kernel(h_ref, a_ref, s_ref, m_ref, wga_ref, wgs_ref, wgm_ref, wa_ref, ws_ref, wm_ref, o_ref):
    h = h_ref[...]
    dot = lambda x, w: jnp.dot(x, w[...], preferred_element_type=F32)
    ga = jax.nn.sigmoid(dot(h, wga_ref))
    gs = jax.nn.sigmoid(dot(h, wgs_ref))
    gm = jax.nn.sigmoid(dot(h, wgm_ref))
    merged = (ga * dot(a_ref[...].astype(BF16), wa_ref) + gs * dot(s_ref[...].astype(BF16), ws_ref)
              + gm * dot(m_ref[...].astype(BF16), wm_ref))
    o_ref[...] = merged.astype(BF16)


def _merge(h, a, s, mo, wg, wa, ws, wm, tm, tn):
    m, d = h.shape
    nj = d // tn
    row = lambda w: pl.BlockSpec((tm, w), lambda i, j: (i, 0))
    col = lambda k, off: pl.BlockSpec((k, tn), lambda i, j: (0, off + j))
    return pl.pallas_call(
        _merge_kernel,
        grid=(m // tm, nj),
        in_specs=[row(d), row(a.shape[1]), row(s.shape[1]), row(mo.shape[1]),
                  col(d, 0), col(d, nj), col(d, 2 * nj),
                  col(a.shape[1], 0), col(s.shape[1], 0), col(mo.shape[1], 0)],
        out_specs=pl.BlockSpec((tm, tn), lambda i, j: (i, j)),
        out_shape=jax.ShapeDtypeStruct((m, d), BF16),
        compiler_params=_cparams(("arbitrary", "arbitrary")),
        name="merge",
    )(h, a, s, mo, wg, wg, wg, wa, ws, wm)


def _proj_res_kernel(x_ref, m_ref, w_ref, o_ref):
    o_ref[...] = x_ref[...] + jnp.dot(m_ref[...], w_ref[...], preferred_element_type=F32)


def _proj_res(x, merged, w, tm, tn):
    m, d = x.shape
    return pl.pallas_call(
        _proj_res_kernel,
        grid=(m // tm, d // tn),
        in_specs=[pl.BlockSpec((tm, tn), lambda i, j: (i, j)),
                  pl.BlockSpec((tm, d), lambda i, j: (i, 0)),
                  pl.BlockSpec((d, tn), lambda i, j: (0, j))],
        out_specs=pl.BlockSpec((tm, tn), lambda i, j: (i, j)),
        out_shape=jax.ShapeDtypeStruct((m, d), F32),
        compiler_params=_cparams(("arbitrary", "arbitrary")),
        name="proj_res",
    )(x, merged, w)


def _ffn_kernel(x_ref, g_ref, w1_ref, w3_ref, w2_ref, o_ref, h_ref):
    f = pl.program_id(1)

    @pl.when(f == 0)
    def _():
        x = x_ref[...]
        r = lax.rsqrt(jnp.mean(x * x, axis=-1, keepdims=True) + EPS)
        h_ref[...] = ((x * r) * g_ref[...]).astype(BF16)
        o_ref[...] = x

    h = h_ref[...]
    a = jnp.dot(h, w1_ref[...], preferred_element_type=F32)
    b = jnp.dot(h, w3_ref[...], preferred_element_type=F32)
    act = (jax.nn.silu(a) * b).astype(BF16)
    o_ref[...] += jnp.dot(act, w2_ref[...], preferred_element_type=F32)


def _ffn(x, g, w1, w3, w2, tm, tf):
    m, d = x.shape
    ff = w1.shape[1]
    return pl.pallas_call(
        _ffn_kernel,
        grid=(m // tm, ff // tf),
        in_specs=[pl.BlockSpec((tm, d), lambda i, f: (i, 0)),
                  pl.BlockSpec((1, d), lambda i, f: (0, 0)),
                  pl.BlockSpec((d, tf), lambda i, f: (0, f)),
                  pl.BlockSpec((d, tf), lambda i, f: (0, f)),
                  pl.BlockSpec((tf, d), lambda i, f: (f, 0))],
        out_specs=pl.BlockSpec((tm, d), lambda i, f: (i, 0)),
        out_shape=jax.ShapeDtypeStruct((m, d), F32),
        scratch_shapes=[pltpu.VMEM((tm, d), BF16)],
        compiler_params=_cparams(("arbitrary", "arbitrary")),
        name="ffn",
    )(x, g, w1, w3, w2)


def kernel(x_prompt, x_sample, mem_prompt, cache_k, cache_v, cache_kidx, cache_mem_k, cache_mem_v, state_ssm_re, state_ssm_im, page_table, norm1_g, w_in, q_norm_g, k_norm_g, ssm_lam_re, ssm_lam_im, ssm_log_dt, ssm_b_re, ssm_b_im, ssm_c_re, ssm_c_im, ssm_d, w_glu, mem_norm_g, w_mem_kv, mq_norm_g, mk_norm_g, w_br_attn, w_br_ssm, w_br_mem, w_o, norm2_g, w_ff1, w_ff3, w_ff2):
    depth = w_in.shape[0]
    bp, tp, d = x_prompt.shape
    bs, ts, _ = x_sample.shape
    rp, rs = bp * tp, bs * ts
    m_tok = mem_prompt.shape[1]
    mw = MEM_HEADS * HEAD_DIM
    kvw = N_KV_HEADS * HEAD_DIM
    groups, p_state = ssm_lam_re.shape[1], ssm_lam_re.shape[2]
    tm = _row_tile(rp + rs)

    x = jnp.concatenate([x_prompt.reshape(rp, d), x_sample.reshape(rs, d)], axis=0)
    mem = mem_prompt.reshape(bp * m_tok, d)
    zero_state = jnp.zeros((bp, groups, p_state), F32)
    cmk = cache_mem_k.reshape(depth, bs, m_tok * MEM_HEADS, HEAD_DIM)
    cmv = cache_mem_v.reshape(depth, bs, m_tok * MEM_HEADS, HEAD_DIM)
    outs = [[] for _ in range(12)]
    nc_p, nc_s = tp // SSM_CHUNK, ts // SSM_CHUNK
    n_steps = max(1, int(math.log2(nc_p)))

    for l in range(depth):
        wp, wg, uw = _pack_w_in(w_in, l)
        flag, gain = _epilogue_rows(q_norm_g[l], k_norm_g[l], mq_norm_g[l])
        z, h = _norm_matmul(x, norm1_g[l][None], wp, flag, gain, tm=tm, tn=1024, emit_h=True)

        mflag = jnp.concatenate([jnp.ones((mw,), F32), jnp.zeros((mw,), F32)])[None]
        mgain = jnp.concatenate([jnp.tile(mk_norm_g[l], MEM_HEADS), jnp.ones((mw,), F32)])[None]
        mkv = _norm_matmul(mem, mem_norm_g[l][None], w_mem_kv[l].astype(BF16), mflag, mgain,
                           tm=bp * m_tok, tn=2 * mw, emit_h=False)
        mk_p = mkv[:, :mw].reshape(bp, m_tok, MEM_HEADS, HEAD_DIM)
        mv_p = mkv[:, mw:].reshape(bp, m_tok, MEM_HEADS, HEAD_DIM)

        a_p = _dsa_prompt(z, bp, tp)
        a_s = _dsa_sample(z, rp, bs, ts, l, cache_k, cache_v, cache_kidx, page_table)
        a_out = jnp.concatenate([a_p, a_s], axis=0)

        mats = _ssm_matrices(ssm_lam_re[l], ssm_lam_im[l], ssm_log_dt[l], ssm_b_re[l], ssm_b_im[l],
                             ssm_c_re[l], ssm_c_im[l], ssm_d[l], n_steps)
        y_p, sf_p = _ssm(z, 0, rp, mats, _state_to_pairs(zero_state, zero_state), bp, nc_p)
        y_s, sf_s = _ssm(z, rp, rs, mats, _state_to_pairs(state_ssm_re[l], state_ssm_im[l]), bs, nc_s)
        s_out = _glu(jnp.concatenate([y_p, y_s], axis=0), w_glu[l].astype(BF16), tm)

        m_p = _mem_attn(z, 0, bp, tp, 512, mk_p.reshape(1, bp, m_tok * MEM_HEADS, HEAD_DIM),
                        mv_p.reshape(1, bp, m_tok * MEM_HEADS, HEAD_DIM), 0)
        m_s = _mem_attn(z, rp, bs, ts, ts, cmk, cmv, l)
        m_out = jnp.concatenate([m_p, m_s], axis=0)

        merged = _merge(h, a_out, s_out, m_out, wg, w_br_attn[l].astype(BF16), w_br_ssm[l].astype(BF16),
                        w_br_mem[l].astype(BF16), tm, min(512, d))
        x = _proj_res(x, merged, w_o[l].astype(BF16), tm, min(1024, d))
        x = _ffn(x, norm2_g[l][None], w_ff1[l].astype(BF16), w_ff3[l].astype(BF16), w_ff2[l].astype(BF16),
                 tm, 512)

        srp, sip = _state_from_pairs(sf_p)
        srs, sis = _state_from_pairs(sf_s)
        new = [
            z[:rp, C_K:C_K + kvw].reshape(bp, tp, N_KV_HEADS, HEAD_DIM),
            z[:rp, C_V:C_V + kvw].reshape(bp, tp, N_KV_HEADS, HEAD_DIM),
            z[:rp, C_KI:C_KI + IDX_DIM].reshape(bp, tp, IDX_DIM),
            mk_p, mv_p, srp, sip,
            z[rp:, C_K:C_K + kvw].reshape(bs, ts, N_KV_HEADS, HEAD_DIM),
            z[rp:, C_V:C_V + kvw].reshape(bs, ts, N_KV_HEADS, HEAD_DIM),
            z[rp:, C_KI:C_KI + IDX_DIM].reshape(bs, ts, IDX_DIM),
            srs, sis,
        ]
        for o, v in zip(outs, new):
            o.append(v)

    st = lambda xs: jnp.stack(xs, axis=0)
    return (x[:rp].reshape(bp, tp, d), x[rp:].reshape(bs, ts, d), *[st(o) for o in outs])
```

```python
import functools
import math

import jax
import jax.numpy as jnp
from jax import lax
from jax.experimental import pallas as pl
from jax.experimental.pallas import tpu as pltpu

F32 = jnp.float32
BF16 = jnp.bfloat16
I32 = jnp.int32

LANES = 128
SUBLANES = 8
BF16_SUBLANES = 16
EPS = 1e-6
HEAD_DIM = 128
N_HEADS = 6
N_KV_HEADS = 2
KV_GROUP = N_HEADS // N_KV_HEADS
IDX_HEADS = 4
IDX_DIM = 64
TOPK_MAX = 256
SSM_GROUP = 16
SSM_STATE = 64
SSM_CHUNK = SUBLANES
SSM_PAIR_W = 2 * SSM_CHUNK * SSM_GROUP
PAIRS_PER_BLOCK = LANES // (2 * SSM_GROUP)
MEM_HEADS = 4
KEY_MIN = -2 ** 31
NEG_INF = float("-inf")
NEG_BIAS = -1e30
M_INIT = -1e29
VMEM_LIMIT = 56 * 1024 * 1024
MAX_ROW_TILE = 768

C_Q, C_K, C_V, C_QI, C_MQ, C_KI, C_WI, C_U = 0, 768, 1024, 1280, 1536, 2048, 2112, 2176
Z_COLS = 3072


def _cparams(sem):
    return pltpu.CompilerParams(dimension_semantics=sem, vmem_limit_bytes=VMEM_LIMIT)


def _layer_spec(w, layer, block, index_map):
    assert w.ndim == len(block) + 1
    return pl.BlockSpec((None,) + tuple(block), lambda *idx: (layer,) + tuple(index_map(*idx)))


def _row_tile(m):
    return max(t for t in range(BF16_SUBLANES, MAX_ROW_TILE + 1, BF16_SUBLANES) if m % t == 0)


def _pack_kernel(w_ref, wp_ref, wg_ref, *, segs, zero_spans, o_g):
    for src, width, dst in segs:
        wp_ref[:, dst:dst + width] = w_ref[:, src:src + width].astype(BF16)
    for start, stop in zero_spans:
        wp_ref[:, start:stop] = jnp.zeros((wp_ref.shape[0], stop - start), BF16)
    wg_ref[...] = w_ref[:, o_g:].astype(BF16)


def _pack_w_in(w_in, l):
    _, d, cols = w_in.shape
    aw, kvw, iqw, mw = N_HEADS * HEAD_DIM, N_KV_HEADS * HEAD_DIM, IDX_HEADS * IDX_DIM, MEM_HEADS * HEAD_DIM
    o_ki = aw + 2 * kvw + iqw
    o_u = o_ki + IDX_DIM + IDX_HEADS
    uw = cols - o_u - mw - 3 * d
    o_mq = o_u + uw
    o_g = o_mq + mw
    segs = ((0, o_ki, C_Q), (o_mq, mw, C_MQ), (o_ki, o_u - o_ki, C_KI), (o_u, uw, C_U))
    zero_spans = ((C_WI + IDX_HEADS, C_U), (C_U + uw, Z_COLS))
    tk = 256
    wp, wg = pl.pallas_call(
        functools.partial(_pack_kernel, segs=segs, zero_spans=zero_spans, o_g=o_g),
        grid=(d // tk,),
        in_specs=[pl.BlockSpec((None, tk, cols), lambda i: (l, i, 0))],
        out_specs=[pl.BlockSpec((tk, Z_COLS), lambda i: (i, 0)), pl.BlockSpec((tk, 3 * d), lambda i: (i, 0))],
        out_shape=[jax.ShapeDtypeStruct((d, Z_COLS), BF16), jax.ShapeDtypeStruct((d, 3 * d), BF16)],
        compiler_params=_cparams(("arbitrary",)),
        name="pack_w_in",
    )(w_in)
    return wp, wg, uw


def _epilogue_rows(qn, kn, mqn):
    aw, kvw, iqw, mw = N_HEADS * HEAD_DIM, N_KV_HEADS * HEAD_DIM, IDX_HEADS * IDX_DIM, MEM_HEADS * HEAD_DIM
    ones = lambda n: jnp.ones((n,), F32)
    zer = lambda n: jnp.zeros((n,), F32)
    flag = jnp.concatenate([ones(aw + kvw), zer(kvw + iqw), ones(mw), zer(Z_COLS - C_KI)])[None]
    gain = jnp.concatenate([
        jnp.tile(qn, N_HEADS), jnp.tile(kn, N_KV_HEADS), ones(kvw + iqw), jnp.tile(mqn, MEM_HEADS),
        ones(IDX_DIM), jnp.full((IDX_HEADS,), IDX_HEADS ** -0.5, F32), ones(Z_COLS - C_WI - IDX_HEADS)])[None]
    return flag, gain


def _norm_matmul_kernel(x_ref, g_ref, w_ref, flag_ref, gain_ref, z_ref, *rest, emit_h):
    if emit_h:
        h_ref, hs_ref = rest
    else:
        (hs_ref,) = rest
    j = pl.program_id(1)

    @pl.when(j == 0)
    def _():
        x = x_ref[...]
        r = lax.rsqrt(jnp.mean(x * x, axis=-1, keepdims=True) + EPS)
        h = ((x * r) * g_ref[...]).astype(BF16)
        hs_ref[...] = h
        if emit_h:
            h_ref[...] = h

    acc = jnp.dot(hs_ref[...], w_ref[...], preferred_element_type=F32)
    for c in range(acc.shape[1] // LANES):
        sl = slice(c * LANES, (c + 1) * LANES)
        y = acc[:, sl]
        r = lax.rsqrt(jnp.mean(y * y, axis=-1, keepdims=True) + EPS)
        scale = jnp.where(flag_ref[:, sl] > 0.0, r, 1.0) * gain_ref[:, sl]
        z_ref[:, sl] = y * scale


def _norm_matmul(x, g, w, flag, gain, *, tm, tn, emit_h, layer=None):
    m, d = x.shape
    n = w.shape[-1]
    w_spec = (pl.BlockSpec((d, tn), lambda i, j: (0, j)) if layer is None
              else _layer_spec(w, layer, (d, tn), lambda i, j: (0, j)))
    out_shape = [jax.ShapeDtypeStruct((m, n), F32)]
    out_specs = [pl.BlockSpec((tm, tn), lambda i, j: (i, j))]
    if emit_h:
        out_shape.append(jax.ShapeDtypeStruct((m, d), BF16))
        out_specs.append(pl.BlockSpec((tm, d), lambda i, j: (i, 0)))
    res = pl.pallas_call(
        functools.partial(_norm_matmul_kernel, emit_h=emit_h),
        grid=(m // tm, n // tn),
        in_specs=[
            pl.BlockSpec((tm, d), lambda i, j: (i, 0)),
            pl.BlockSpec((1, d), lambda i, j: (0, 0)),
            w_spec,
            pl.BlockSpec((1, tn), lambda i, j: (0, j)),
            pl.BlockSpec((1, tn), lambda i, j: (0, j)),
        ],
        out_specs=out_specs,
        out_shape=out_shape,
        scratch_shapes=[pltpu.VMEM((tm, d), BF16)],
        compiler_params=_cparams(("arbitrary", "arbitrary")),
        name="norm_matmul",
    )(x, g, w, flag, gain)
    return res if emit_h else res[0]


def _ordered_to_f32(key):
    return pltpu.bitcast(jnp.where(key >= 0, key, key ^ 0x7FFFFFFF), F32)


def _count(score_ref, n_slab, slab_w, indicator):
    rows = score_ref.shape[0]

    def slab(s, acc):
        static = isinstance(s, int)
        base = s * slab_w if static else pl.multiple_of(s * slab_w, slab_w)
        parts = []
        for c in range(slab_w // LANES):
            off = base + c * LANES
            x = score_ref[:, off:off + LANES] if static else score_ref[:, pl.ds(off, LANES)]
            parts.append(indicator(x, off))
        while len(parts) > 1:
            parts = [a + b for a, b in zip(parts[0::2], parts[1::2])] + ([parts[-1]] if len(parts) % 2 else [])
        return acc + parts[0]

    acc = jnp.zeros((rows, LANES), F32)
    if isinstance(n_slab, int):
        for s in range(n_slab):
            acc = slab(s, acc)
    else:
        acc = lax.fori_loop(0, n_slab, slab, acc)
    return jnp.sum(acc, axis=1, keepdims=True)


def _kth_largest(score_ref, n_slab, slab_w, k):
    rows = score_ref.shape[0]
    count_ge = lambda cand: _count(score_ref, n_slab, slab_w, lambda x, off: jnp.where(x >= cand, 1.0, 0.0))
    lo = jnp.where(count_ge(jnp.zeros((rows, 1), F32)) >= k, 0, KEY_MIN)

    def body(it, lo):
        cand = lo | (1 << (30 - it))
        return jnp.where(count_ge(_ordered_to_f32(cand)) >= k, cand, lo)

    lo = lax.fori_loop(0, 31, body, lo)
    return jnp.where(lo == KEY_MIN, NEG_INF, _ordered_to_f32(lo))


def _stack_heads(x, first, count, width):
    return jnp.concatenate([x[:, (first + r) * width:(first + r + 1) * width] for r in range(count)], axis=0)


def _indexer_scores(qi_stack, w_cols, ki_c, rows, keys_on_lanes=False):
    dims = (((1,), (0,)), ((), ())) if keys_on_lanes else (((1,), (1,)), ((), ()))
    logits = lax.dot_general(qi_stack, ki_c, dims, preferred_element_type=F32)
    score = jnp.maximum(logits[0:rows], 0.0) * w_cols[0]
    for h in range(1, IDX_HEADS):
        score = score + jnp.maximum(logits[h * rows:(h + 1) * rows], 0.0) * w_cols[h]
    return score


TQ = 128
KC = 512
KC_ATT = 1024
ROW_SLAB = KC
LOG2_E = math.log2(math.e)


def _count_keys(score_ref, n_slab, indicator):
    lanes = score_ref.shape[1]

    def slab(s, acc):
        base = pl.multiple_of(s * ROW_SLAB, ROW_SLAB)
        parts = [indicator(score_ref[pl.ds(base + r * SUBLANES, SUBLANES), :]) for r in range(ROW_SLAB // SUBLANES)]
        while len(parts) > 1:
            parts = [a + b for a, b in zip(parts[0::2], parts[1::2])] + ([parts[-1]] if len(parts) % 2 else [])
        return acc + parts[0]

    acc = lax.fori_loop(0, n_slab, slab, jnp.zeros((SUBLANES, lanes), F32))
    return jnp.sum(acc, axis=0, keepdims=True)


def _kth_largest_keys(score_ref, n_slab, k):
    lanes = score_ref.shape[1]
    count_ge = lambda cand: _count_keys(score_ref, n_slab, lambda x: jnp.where(x >= cand, 1.0, 0.0))
    lo = jnp.where(count_ge(jnp.zeros((1, lanes), F32)) >= k, 0, KEY_MIN)

    def body(it, lo):
        cand = lo | (1 << (30 - it))
        return jnp.where(count_ge(_ordered_to_f32(cand)) >= k, cand, lo)

    lo = lax.fori_loop(0, 31, body, lo)
    return jnp.where(lo == KEY_MIN, NEG_INF, _ordered_to_f32(lo))


def _bias_keys(score_ref, bias_ref, n_slab, k, thr):
    lanes = score_ref.shape[1]
    need = k - _count_keys(score_ref, n_slab, lambda x: jnp.where(x > thr, 1.0, 0.0))
    tri = (lax.broadcasted_iota(I32, (LANES, LANES), 0)
           >= lax.broadcasted_iota(I32, (LANES, LANES), 1)).astype(BF16)

    def body(s, seen):
        base = pl.multiple_of(s * ROW_SLAB, ROW_SLAB)
        xs = [score_ref[pl.ds(base + j * LANES, LANES), :] for j in range(ROW_SLAB // LANES)]
        eqs = [jnp.where(x == thr, 1.0, 0.0) for x in xs]
        prefixes = [jnp.dot(tri, e.astype(BF16), preferred_element_type=F32) for e in eqs]
        for j, (x, eqf, prefix) in enumerate(zip(xs, eqs, prefixes)):
            keep = jnp.where(x > thr, 1.0, jnp.where(seen + prefix <= need, eqf, 0.0))
            bias_ref[pl.ds(base + j * LANES, LANES), :] = jnp.where(
                keep > 0.0, jnp.where(x > NEG_INF, 0.0, NEG_BIAS), NEG_BIAS)
            seen = seen + prefix[LANES - 1:LANES, :]
        return seen

    lax.fori_loop(0, n_slab, body, jnp.zeros((1, lanes), F32))


def _dsa_prompt_kernel(q_ref, qi_ref, kiwq_ref, k_ref, v_ref, kiw_ref, o_ref,
                       kbf_ref, vt_ref, kibf_ref, score_ref, bias_ref, *, topk, kc_att):
    i = pl.program_id(1)
    seq = k_ref.shape[0]

    @pl.when(i == 0)
    def _():
        kbf_ref[...] = k_ref[...].astype(BF16)
        kibf_ref[...] = kiw_ref[...].astype(BF16)
        for c in range(seq // KC):
            vt_ref[:, c * KC:(c + 1) * KC] = v_ref[c * KC:(c + 1) * KC, :].T.astype(BF16)

    n_kc = (i * TQ + TQ + KC - 1) // KC
    n_att = (n_kc * KC + kc_att - 1) // kc_att
    qi_stack = _stack_heads(qi_ref[...], 0, IDX_HEADS, IDX_DIM).astype(BF16)
    w_t = kiwq_ref[...].T
    scale_i = IDX_DIM ** -0.5
    w_rows = [w_t[IDX_DIM + h:IDX_DIM + h + 1, :] * scale_i for h in range(IDX_HEADS)]
    qpos = i * TQ + lax.broadcasted_iota(I32, (KC, TQ), 1)

    def score_body(c, _):
        off = pl.multiple_of(c * KC, KC)
        logits = lax.dot_general(kibf_ref[pl.ds(off, KC), 0:IDX_DIM], qi_stack, (((1,), (1,)), ((), ())),
                                 preferred_element_type=F32)
        score = jnp.maximum(logits[:, 0:TQ], 0.0) * w_rows[0]
        for h in range(1, IDX_HEADS):
            score = score + jnp.maximum(logits[:, h * TQ:(h + 1) * TQ], 0.0) * w_rows[h]
        kpos = off + lax.broadcasted_iota(I32, (KC, TQ), 0)
        score_ref[pl.ds(off, KC), :] = jnp.where(kpos > qpos, NEG_INF, score)
        return 0

    lax.fori_loop(0, n_kc, score_body, 0)
    thr = _kth_largest_keys(score_ref, n_kc, topk)
    _bias_keys(score_ref, bias_ref, n_kc, topk, thr)

    @pl.when(n_att * kc_att > n_kc * KC)
    def _():
        bias_ref[pl.ds(pl.multiple_of(n_kc * KC, KC), KC), :] = jnp.full((KC, TQ), NEG_BIAS, F32)

    q = q_ref[...] * (HEAD_DIM ** -0.5 * LOG2_E)
    q_gs = [_stack_heads(q, g * KV_GROUP, KV_GROUP, HEAD_DIM).astype(BF16) for g in range(N_KV_HEADS)]
    gq = KV_GROUP * TQ

    def att_body(c, carries):
        off = pl.multiple_of(c * kc_att, kc_att)
        b = bias_ref[pl.ds(off, kc_att), :]
        bias = jnp.concatenate([b] * KV_GROUP, axis=1)
        out = []
        for g in range(N_KV_HEADS):
            gs = slice(g * HEAD_DIM, (g + 1) * HEAD_DIM)
            m, l, acc = carries[g]
            s = lax.dot_general(kbf_ref[pl.ds(off, kc_att), gs], q_gs[g], (((1,), (1,)), ((), ())),
                                preferred_element_type=F32) + bias
            m_new = jnp.maximum(m, jnp.max(s, axis=0, keepdims=True))
            alpha = jnp.exp2(m - m_new)
            p = jnp.exp2(s - m_new)
            l = alpha * l + jnp.sum(p, axis=0, keepdims=True)
            acc = alpha * acc + jnp.dot(vt_ref[gs, pl.ds(off, kc_att)], p.astype(BF16),
                                        preferred_element_type=F32)
            out.append((m_new, l, acc))
        return tuple(out)

    init = (jnp.full((1, gq), M_INIT, F32), jnp.zeros((1, gq), F32), jnp.zeros((HEAD_DIM, gq), F32))
    carries = lax.fori_loop(0, n_att, att_body, (init,) * N_KV_HEADS)
    for g, (_, l, acc) in enumerate(carries):
        o_t = acc / l
        for r in range(KV_GROUP):
            hh = g * KV_GROUP + r
            o_ref[:, hh * HEAD_DIM:(hh + 1) * HEAD_DIM] = o_t[:, r * TQ:(r + 1) * TQ].T


def _dsa_prompt(z, batch, seq):
    nq = seq // TQ
    topk = min(TOPK_MAX, seq // 4)
    aw, kvw, iqw = N_HEADS * HEAD_DIM, N_KV_HEADS * HEAD_DIM, IDX_HEADS * IDX_DIM
    return pl.pallas_call(
        functools.partial(_dsa_prompt_kernel, topk=topk, kc_att=min(KC_ATT, seq)),
        grid=(batch, nq),
        in_specs=[
            pl.BlockSpec((TQ, aw), lambda b, i: (b * nq + i, C_Q // aw)),
            pl.BlockSpec((TQ, iqw), lambda b, i: (b * nq + i, C_QI // iqw)),
            pl.BlockSpec((TQ, LANES), lambda b, i: (b * nq + i, C_KI // LANES)),
            pl.BlockSpec((seq, kvw), lambda b, i: (b, C_K // kvw)),
            pl.BlockSpec((seq, kvw), lambda b, i: (b, C_V // kvw)),
            pl.BlockSpec((seq, LANES), lambda b, i: (b, C_KI // LANES)),
        ],
        out_specs=pl.BlockSpec((TQ, aw), lambda b, i: (b * nq + i, 0)),
        out_shape=jax.ShapeDtypeStruct((batch * seq, aw), F32),
        scratch_shapes=[
            pltpu.VMEM((seq, kvw), BF16),
            pltpu.VMEM((kvw, seq), BF16),
            pltpu.VMEM((seq, LANES), BF16),
            pltpu.VMEM((seq, TQ), F32),
            pltpu.VMEM((seq, TQ), F32),
        ],
        compiler_params=_cparams(("arbitrary", "arbitrary")),
        name="dsa_prompt",
    )(z, z, z, z, z, z)


KC_S = 1024


def _tie_cut(score_ref, n128, k, thr):
    rows = score_ref.shape[0]
    width = n128 * LANES
    need = k - _count(score_ref, 1, width, lambda x, off: jnp.where(x > thr, 1.0, 0.0))
    ties = _count(score_ref, 1, width, lambda x, off: jnp.where(x == thr, 1.0, 0.0))
    lane = lax.broadcasted_iota(I32, (rows, LANES), 1)
    everything = jnp.full((rows, 1), width - 1, I32)

    def bisect():
        def ties_upto(j):
            return _count(score_ref, 1, width,
                          lambda x, off: jnp.where(x == thr, jnp.where(lane + off <= j, 1.0, 0.0), 0.0))

        def body(_, c):
            lo, hi = c
            mid = (lo + hi) >> 1
            ok = ties_upto(mid) >= need
            return jnp.where(ok, lo, mid), jnp.where(ok, mid, hi)

        steps = int(math.ceil(math.log2(width + 1)))
        return lax.fori_loop(0, steps, body, (jnp.full((rows, 1), -1, I32), everything))[1]

    return lax.cond(jnp.max(ties - need) > 0.0, bisect, lambda: everything)


def _write_bias(score_ref, bias_ref, n128, thr, cut):
    rows = score_ref.shape[0]
    lane = lax.broadcasted_iota(I32, (rows, LANES), 1)
    for c in range(n128):
        x = score_ref[:, c * LANES:(c + 1) * LANES]
        tie = jnp.where(x == thr, jnp.where(lane + c * LANES <= cut, 1.0, 0.0), 0.0)
        keep = jnp.where(x > thr, 1.0, tie)
        bias_ref[:, c * LANES:(c + 1) * LANES] = jnp.where(keep > 0.0, jnp.where(x > NEG_INF, 0.0, NEG_BIAS),
                                                           NEG_BIAS)


def _lane_fold(x, op):
    parts = [x[:, c * LANES:(c + 1) * LANES] for c in range(x.shape[1] // LANES)]
    while len(parts) > 1:
        parts = [op(a, b) for a, b in zip(parts[0::2], parts[1::2])] + ([parts[-1]] if len(parts) % 2 else [])
    return parts[0]


def _dsa_sample_kernel(pt_ref, q_ref, qi_ref, kiw_ref, kn_ref, vn_ref, ck_ref, cv_ref, cki_ref, o_ref,
                       kbuf, vbuf, kibuf, score_ref, bias_ref, s_ref, sem, *, layer, n_pages, page, tq, topk):
    b = pl.program_id(0)
    nb = pl.num_programs(0)
    past = n_pages * page
    slot = b % 2
    streams = ((ck_ref, kbuf, page * N_KV_HEADS), (cv_ref, vbuf, page * N_KV_HEADS), (cki_ref, kibuf, IDX_DIM))

    def page_copy(which, bb, sl, p):
        src, dst, rpp = streams[which]
        return pltpu.make_async_copy(src.at[layer, pt_ref[bb * n_pages + p]],
                                     dst.at[sl, pl.ds(p * rpp, rpp)], sem.at[which, sl])

    def start_fetch(bb, sl):
        def body(p, _):
            for which in range(len(streams)):
                page_copy(which, bb, sl, p).start()
            return 0
        lax.fori_loop(0, n_pages, body, 0)

    def wait_fetch(bb, sl, which):
        def body(p, _):
            page_copy(which, bb, sl, p).wait()
            return 0
        lax.fori_loop(0, n_pages, body, 0)

    @pl.when(b == 0)
    def _():
        start_fetch(0, 0)

    @pl.when(b + 1 < nb)
    def _():
        start_fetch(b + 1, 1 - slot)

    qi_stack = _stack_heads(qi_ref[...], 0, IDX_HEADS, IDX_DIM).astype(BF16)
    kiw = kiw_ref[...]
    scale_i = IDX_DIM ** -0.5
    w_cols = [kiw[:, IDX_DIM + h:IDX_DIM + h + 1] * scale_i for h in range(IDX_HEADS)]
    n128 = past // LANES + 1
    ppc = KC // page

    wait_fetch(b, slot, 2)
    for c in range(past // KC):
        ki_t = jnp.concatenate([kibuf[slot, (c * ppc + j) * IDX_DIM:(c * ppc + j + 1) * IDX_DIM, :]
                                for j in range(ppc)], axis=1).astype(BF16)
        score_ref[:, c * KC:(c + 1) * KC] = _indexer_scores(qi_stack, w_cols, ki_t, tq, keys_on_lanes=True)
    pad = jnp.zeros((LANES - tq, LANES), F32)
    ki_new = jnp.concatenate([kiw, pad], axis=0)[:, 0:IDX_DIM].astype(BF16)
    score_new = _indexer_scores(qi_stack, w_cols, ki_new, tq)
    newer = lax.broadcasted_iota(I32, (tq, LANES), 1) > lax.broadcasted_iota(I32, (tq, LANES), 0)
    score_ref[:, past:past + LANES] = jnp.where(newer, NEG_INF, score_new)

    thr = _kth_largest(score_ref, 1, n128 * LANES, topk)
    _write_bias(score_ref, bias_ref, n128, thr, _tie_cut(score_ref, n128, topk, thr))

    wait_fetch(b, slot, 0)
    wait_fetch(b, slot, 1)
    q = q_ref[...] * (HEAD_DIM ** -0.5)
    q_gs = [_stack_heads(q, g * KV_GROUP, KV_GROUP, HEAD_DIM).astype(BF16) for g in range(N_KV_HEADS)]
    kv_pad = jnp.zeros((LANES - tq, N_KV_HEADS * HEAD_DIM), F32)
    k_new = jnp.concatenate([kn_ref[...], kv_pad], axis=0).astype(BF16)
    v_new = jnp.concatenate([vn_ref[...], kv_pad], axis=0).astype(BF16)
    chunks = [(c * KC_S, KC_S) for c in range(past // KC_S)] + [(past, LANES)]

    def keys_of(buf, new, g, start, size):
        if start == past:
            return new[:, g * HEAD_DIM:(g + 1) * HEAD_DIM]
        return buf[slot, pl.ds(start * N_KV_HEADS + g, size, stride=N_KV_HEADS), :].astype(BF16)

    m_acc = [jnp.full((KV_GROUP * tq, LANES), M_INIT, F32) for _ in range(N_KV_HEADS)]
    for start, size in chunks:
        bias = jnp.concatenate([bias_ref[:, start:start + size]] * KV_GROUP, axis=0)
        for g in range(N_KV_HEADS):
            s = lax.dot_general(q_gs[g], keys_of(kbuf, k_new, g, start, size), (((1,), (1,)), ((), ())),
                                preferred_element_type=F32) + bias
            s_ref[g, :, start:start + size] = s
            m_acc[g] = jnp.maximum(m_acc[g], _lane_fold(s, jnp.maximum))
    m = [jnp.max(a, axis=1, keepdims=True) for a in m_acc]
    l_acc = [jnp.zeros((KV_GROUP * tq, LANES), F32) for _ in range(N_KV_HEADS)]
    acc = [jnp.zeros((KV_GROUP * tq, HEAD_DIM), F32) for _ in range(N_KV_HEADS)]
    for start, size in chunks:
        for g in range(N_KV_HEADS):
            p = jnp.exp(s_ref[g, :, start:start + size] - m[g])
            l_acc[g] = l_acc[g] + _lane_fold(p, jnp.add)
            acc[g] = acc[g] + jnp.dot(p.astype(BF16), keys_of(vbuf, v_new, g, start, size),
                                      preferred_element_type=F32)
    for g in range(N_KV_HEADS):
        o = acc[g] / jnp.sum(l_acc[g], axis=1, keepdims=True)
        for r in range(KV_GROUP):
            hh = g * KV_GROUP + r
            o_ref[:, hh * HEAD_DIM:(hh + 1) * HEAD_DIM] = o[r * tq:(r + 1) * tq]


def _dsa_sample(z, row0, nb, tq, layer, cache_k, cache_v, cache_kidx, page_table):
    depth, n_pool, page = cache_k.shape[:3]
    n_pages = page_table.shape[1]
    past = n_pages * page
    topk = min(TOPK_MAX, (past + tq) // 4)
    aw, kvw, iqw = N_HEADS * HEAD_DIM, N_KV_HEADS * HEAD_DIM, IDX_HEADS * IDX_DIM
    rb = row0 // tq
    grid_spec = pltpu.PrefetchScalarGridSpec(
        num_scalar_prefetch=1,
        grid=(nb,),
        in_specs=[
            pl.BlockSpec((tq, aw), lambda b, pt: (rb + b, C_Q // aw)),
            pl.BlockSpec((tq, iqw), lambda b, pt: (rb + b, C_QI // iqw)),
            pl.BlockSpec((tq, LANES), lambda b, pt: (rb + b, C_KI // LANES)),
            pl.BlockSpec((tq, kvw), lambda b, pt: (rb + b, C_K // kvw)),
            pl.BlockSpec((tq, kvw), lambda b, pt: (rb + b, C_V // kvw)),
            pl.BlockSpec(memory_space=pl.ANY),
            pl.BlockSpec(memory_space=pl.ANY),
            pl.BlockSpec(memory_space=pl.ANY),
        ],
        out_specs=pl.BlockSpec((tq, aw), lambda b, pt: (b, 0)),
        scratch_shapes=[
            pltpu.VMEM((2, past * N_KV_HEADS, HEAD_DIM), F32),
            pltpu.VMEM((2, past * N_KV_HEADS, HEAD_DIM), F32),
            pltpu.VMEM((2, n_pages * IDX_DIM, page), F32),
            pltpu.VMEM((tq, past + LANES), F32),
            pltpu.VMEM((tq, past + LANES), F32),
            pltpu.VMEM((N_KV_HEADS, KV_GROUP * tq, past + LANES), F32),
            pltpu.SemaphoreType.DMA((3, 2)),
        ],
    )
    ck = cache_k.reshape(depth, n_pool, page * N_KV_HEADS, HEAD_DIM)
    cv = cache_v.reshape(depth, n_pool, page * N_KV_HEADS, HEAD_DIM)
    return pl.pallas_call(
        functools.partial(_dsa_sample_kernel, layer=layer, n_pages=n_pages, page=page, tq=tq, topk=topk),
        grid_spec=grid_spec,
        out_shape=jax.ShapeDtypeStruct((nb * tq, aw), F32),
        compiler_params=_cparams(("arbitrary",)),
        name="dsa_sample",
    )(page_table.reshape(-1), z, z, z, z, z, ck, cv, jnp.swapaxes(cache_kidx, 2, 3))


def _mem_attn_kernel(q_ref, k_ref, v_ref, o_ref):
    q = q_ref[...] * (HEAD_DIM ** -0.5)
    m_tok = k_ref.shape[0] // MEM_HEADS
    for h in range(MEM_HEADS):
        hs = slice(h * HEAD_DIM, (h + 1) * HEAD_DIM)
        rows = pl.ds(h, m_tok, stride=MEM_HEADS)
        s = lax.dot_general(q[:, hs].astype(BF16), k_ref[rows, :].astype(BF16),
                            (((1,), (1,)), ((), ())), preferred_element_type=F32)
        m = jnp.max(s, axis=1, keepdims=True)
        p = jnp.exp(s - m)
        l = jnp.sum(p, axis=1, keepdims=True)
        o = jnp.dot(p.astype(BF16), v_ref[rows, :].astype(BF16), preferred_element_type=F32)
        o_ref[:, hs] = o / l


def _mem_attn(z, row0, nb, t, tq, mk, mv, layer):
    mw = MEM_HEADS * HEAD_DIM
    nq = t // tq
    rb = row0 // tq
    kv_spec = pl.BlockSpec((None, None) + mk.shape[2:], lambda b, i: (layer, b, 0, 0))
    return pl.pallas_call(
        _mem_attn_kernel,
        grid=(nb, nq),
        in_specs=[pl.BlockSpec((tq, mw), lambda b, i: (rb + b * nq + i, C_MQ // mw)), kv_spec, kv_spec],
        out_specs=pl.BlockSpec((tq, mw), lambda b, i: (b * nq + i, 0)),
        out_shape=jax.ShapeDtypeStruct((nb * t, mw), F32),
        compiler_params=_cparams(("arbitrary", "arbitrary")),
        name="mem_attn",
    )(z, mk, mv)


def _shift_rows(x, sh):
    rows = lax.broadcasted_iota(I32, x.shape, 0)
    return jnp.where(rows >= sh, pltpu.roll(x, sh, axis=0), 0.0)


def _ssm_pair(u, m_w, wx_w, wc_w, pr, pi, s0, nb, nc):
    half = SSM_PAIR_W // 2
    y_in = jnp.dot(u, m_w, preferred_element_type=F32)
    x = jnp.dot(u, wx_w, preferred_element_type=F32)
    xr, xi = x[:, :half], x[:, half:]
    s0r, s0i = s0[:, :half], s0[:, half:]
    ar, ai = pr[0:1], pi[0:1]
    if nc == 1:
        sr = xr + ar * s0r - ai * s0i
        si = xi + ar * s0i + ai * s0r
        pvr, pvi = s0r, s0i
        sfin = jnp.concatenate([sr, si], axis=1)
    else:
        prev_r, prev_i, fin = [], [], []
        first = lax.broadcasted_iota(I32, (nc, half), 0) == 0
        for b in range(nb):
            rs = slice(b * nc, (b + 1) * nc)
            b0r, b0i = s0r[b:b + 1], s0i[b:b + 1]
            sr = xr[rs] + jnp.where(first, ar * b0r - ai * b0i, 0.0)
            si = xi[rs] + jnp.where(first, ar * b0i + ai * b0r, 0.0)
            for mstep in range(int(math.log2(nc))):
                sh = 2 ** mstep
                qr, qi = pr[mstep:mstep + 1], pi[mstep:mstep + 1]
                tr, ti = _shift_rows(sr, sh), _shift_rows(si, sh)
                sr, si = sr + qr * tr - qi * ti, si + qr * ti + qi * tr
            fin.append(jnp.concatenate([sr[nc - 1:nc], si[nc - 1:nc]], axis=1))
            prev_r.append(jnp.where(first, b0r, pltpu.roll(sr, 1, axis=0)))
            prev_i.append(jnp.where(first, b0i, pltpu.roll(si, 1, axis=0)))
        pvr = jnp.concatenate(prev_r, axis=0)
        pvi = jnp.concatenate(prev_i, axis=0)
        sfin = jnp.concatenate(fin, axis=0)
    prev = jnp.concatenate([pvr, pvi], axis=1).astype(BF16)
    return y_in + jnp.dot(prev, wc_w, preferred_element_type=F32), sfin


def _ssm_kernel(u_ref, m_ref, wx_ref, wc_ref, pr_ref, pi_ref, s0_ref, y_ref, sf_ref, *, nb, nc):
    nch = nb * nc
    gw = SSM_GROUP
    tok = [u_ref[pl.ds(i, nch, stride=SSM_CHUNK), :] for i in range(SSM_CHUNK)]
    ys = []
    for q in range(PAIRS_PER_BLOCK):
        base = q * 2 * gw
        u = jnp.concatenate([tok[i][:, base + gg * gw:base + (gg + 1) * gw]
                             for gg in range(2) for i in range(SSM_CHUNK)], axis=1).astype(BF16)
        y, sfin = _ssm_pair(u, m_ref[q], wx_ref[q], wc_ref[q], pr_ref[q], pi_ref[q], s0_ref[q], nb, nc)
        sf_ref[q] = sfin
        ys.append(y)
    for i in range(SSM_CHUNK):
        y_ref[pl.ds(i, nch, stride=SSM_CHUNK), :] = jnp.concatenate(
            [ys[q][:, (gg * SSM_CHUNK + i) * gw:(gg * SSM_CHUNK + i + 1) * gw]
             for q in range(PAIRS_PER_BLOCK) for gg in range(2)], axis=1)


def _ssm(z, row0, rows, mats, s0_pairs, layer, nb, nc):
    m_mat, wx, wc, pr, pi = mats
    _, npair, _, w = m_mat.shape
    pw = pr.shape[2]
    ppb = PAIRS_PER_BLOCK
    blk3 = lambda r, c: pl.BlockSpec((ppb, r, c), lambda j: (j, 0, 0))
    lay3 = lambda a, r, c: _layer_spec(a, layer, (ppb, r, c), lambda j: (j, 0, 0))
    return pl.pallas_call(
        functools.partial(_ssm_kernel, nb=nb, nc=nc),
        grid=(npair // ppb,),
        in_specs=[pl.BlockSpec((rows, LANES), lambda j: (row0 // rows, C_U // LANES + j)),
                  lay3(m_mat, w, w), lay3(wx, w, w), lay3(wc, w, w), lay3(pr, pw, LANES), lay3(pi, pw, LANES),
                  lay3(s0_pairs, nb, w)],
        out_specs=[pl.BlockSpec((rows, LANES), lambda j: (0, j)), blk3(nb, w)],
        out_shape=[jax.ShapeDtypeStruct((rows, npair * 2 * SSM_GROUP), F32),
                   jax.ShapeDtypeStruct((npair, nb, w), F32)],
        compiler_params=_cparams(("arbitrary",)),
        name="ssm",
    )(z, m_mat, wx, wc, pr, pi, s0_pairs)


def _ssm_matrices(lam_re, lam_im, log_dt, b_re, b_im, c_re, c_im, d_skip, n_steps):
    g, p, c = b_re.shape
    L = SSM_CHUNK
    dt = jnp.exp(log_dt)[:, None]
    zr, zi = lam_re * dt, lam_im * dt
    mag = jnp.exp(zr)
    ar, ai = mag * jnp.cos(zi), mag * jnp.sin(zi)
    den = lam_re * lam_re + lam_im * lam_im
    cr = ((ar - 1.0) * lam_re + ai * lam_im) / den
    ci = (ai * lam_re - (ar - 1.0) * lam_im) / den
    bbr = cr[..., None] * b_re - ci[..., None] * b_im
    bbi = cr[..., None] * b_im + ci[..., None] * b_re
    n = jnp.arange(L + 1, dtype=F32)[:, None, None]
    pm = jnp.exp(zr[None] * n)
    pwr, pwi = pm * jnp.cos(zi[None] * n), pm * jnp.sin(zi[None] * n)
    car = c_re[None] * pwr[:, :, None, :] - c_im[None] * pwi[:, :, None, :]
    cai = c_re[None] * pwi[:, :, None, :] + c_im[None] * pwr[:, :, None, :]
    kern = jnp.einsum('ngop,gpc->ngoc', car, bbr) - jnp.einsum('ngop,gpc->ngoc', cai, bbi)
    lag = jnp.arange(L)[None, :] - jnp.arange(L)[:, None]
    toe = jnp.where((lag >= 0)[:, :, None, None, None], kern[jnp.clip(lag, 0, L)], 0.0)
    eye = (lag == 0)[:, :, None, None, None] * (jnp.eye(c)[None, None, None] * d_skip[None, None, :, :, None])
    toe = toe + eye
    m_g = jnp.transpose(toe, (2, 0, 4, 1, 3)).reshape(g, L * c, L * c)
    rev = pwr[L - 1 - jnp.arange(L)], pwi[L - 1 - jnp.arange(L)]
    wxr = rev[0][..., None] * bbr[None] - rev[1][..., None] * bbi[None]
    wxi = rev[0][..., None] * bbi[None] + rev[1][..., None] * bbr[None]
    wxr = jnp.transpose(wxr, (1, 0, 3, 2)).reshape(g, L * c, p)
    wxi = jnp.transpose(wxi, (1, 0, 3, 2)).reshape(g, L * c, p)
    wcr = jnp.transpose(car[1:], (1, 3, 0, 2)).reshape(g, p, L * c)
    wci = -jnp.transpose(cai[1:], (1, 3, 0, 2)).reshape(g, p, L * c)

    def pair_blockdiag(a):
        a = a.reshape(g // 2, 2, a.shape[1], a.shape[2])
        z = jnp.zeros_like(a[:, 0])
        return jnp.concatenate([jnp.concatenate([a[:, 0], z], axis=2),
                                jnp.concatenate([z, a[:, 1]], axis=2)], axis=1)

    m_pair = pair_blockdiag(m_g)
    wx_pair = jnp.concatenate([pair_blockdiag(wxr), pair_blockdiag(wxi)], axis=2)
    wc_pair = jnp.concatenate([pair_blockdiag(wcr), pair_blockdiag(wci)], axis=1)
    qr, qi = [pwr[L]], [pwi[L]]
    for _ in range(n_steps - 1):
        qr, qi = qr + [qr[-1] * qr[-1] - qi[-1] * qi[-1]], qi + [2.0 * qr[-1] * qi[-1]]
    rows = -(-n_steps // SUBLANES) * SUBLANES
    padz = [jnp.zeros_like(qr[0])] * (rows - n_steps)
    pr = jnp.transpose(jnp.stack(qr + padz), (1, 0, 2)).reshape(g // 2, 2, rows, p)
    pi = jnp.transpose(jnp.stack(qi + padz), (1, 0, 2)).reshape(g // 2, 2, rows, p)
    pr = jnp.concatenate([pr[:, 0], pr[:, 1]], axis=2)
    pi = jnp.concatenate([pi[:, 0], pi[:, 1]], axis=2)
    return m_pair.astype(BF16), wx_pair.astype(BF16), wc_pair.astype(BF16), pr, pi


def _state_to_pairs(sr, si):
    nb, g, p = sr.shape
    f = lambda s: jnp.transpose(s.reshape(nb, g // 2, 2 * p), (1, 0, 2))
    return jnp.concatenate([f(sr), f(si)], axis=2)


def _state_from_pairs(s):
    npair, nb, w = s.shape
    f = lambda a: jnp.transpose(a, (1, 0, 2)).reshape(nb, npair * 2, w // 4)
    return f(s[:, :, :w // 2]), f(s[:, :, w // 2:])


def _glu_kernel(y_ref, w_ref, o_ref):
    y = jax.nn.gelu(y_ref[...])
    gate = jnp.dot(y.astype(BF16), w_ref[...], preferred_element_type=F32)
    o_ref[...] = y * jax.nn.sigmoid(gate)


def _glu(y, w, layer, tm):
    m, n = y.shape
    return pl.pallas_call(
        _glu_kernel,
        grid=(m // tm,),
        in_specs=[pl.BlockSpec((tm, n), lambda i: (i, 0)), _layer_spec(w, layer, (n, n), lambda i: (0, 0))],
        out_specs=pl.BlockSpec((tm, n), lambda i: (i, 0)),
        out_shape=jax.ShapeDtypeStruct((m, n), F32),
        compiler_params=_cparams(("arbitrary",)),
        name="glu",
    )(y, w)


def _merge_kernel(h_ref, a_ref, s_ref, m_ref, wga_ref, wgs_ref, wgm_ref, wa_ref, ws_ref, wm_ref, o_ref):
    h = h_ref[...]
    dot = lambda x, w: jnp.dot(x, w[...], preferred_element_type=F32)
    ga = jax.nn.sigmoid(dot(h, wga_ref))
    gs = jax.nn.sigmoid(dot(h, wgs_ref))
    gm = jax.nn.sigmoid(dot(h, wgm_ref))
    merged = (ga * dot(a_ref[...].astype(BF16), wa_ref) + gs * dot(s_ref[...].astype(BF16), ws_ref)
              + gm * dot(m_ref[...].astype(BF16), wm_ref))
    o_ref[...] = merged.astype(BF16)


def _merge(h, a, s, mo, wg, wa, ws, wm, layer, tm, tn):
    m, d = h.shape
    nj = d // tn
    row = lambda w: pl.BlockSpec((tm, w), lambda i, j: (i, 0))
    col = lambda k, off: pl.BlockSpec((k, tn), lambda i, j: (0, off + j))
    lcol = lambda w: _layer_spec(w, layer, (w.shape[1], tn), lambda i, j: (0, j))
    return pl.pallas_call(
        _merge_kernel,
        grid=(m // tm, nj),
        in_specs=[row(d), row(a.shape[1]), row(s.shape[1]), row(mo.shape[1]),
                  col(d, 0), col(d, nj), col(d, 2 * nj), lcol(wa), lcol(ws), lcol(wm)],
        out_specs=pl.BlockSpec((tm, tn), lambda i, j: (i, j)),
        out_shape=jax.ShapeDtypeStruct((m, d), BF16),
        compiler_params=_cparams(("arbitrary", "arbitrary")),
        name="merge",
    )(h, a, s, mo, wg, wg, wg, wa, ws, wm)


def _proj_res_kernel(x_ref, m_ref, w_ref, o_ref):
    o_ref[...] = x_ref[...] + jnp.dot(m_ref[...], w_ref[...], preferred_element_type=F32)


def _proj_res(x, merged, w, layer, tm, tn):
    m, d = x.shape
    return pl.pallas_call(
        _proj_res_kernel,
        grid=(m // tm, d // tn),
        in_specs=[pl.BlockSpec((tm, tn), lambda i, j: (i, j)),
                  pl.BlockSpec((tm, d), lambda i, j: (i, 0)),
                  _layer_spec(w, layer, (d, tn), lambda i, j: (0, j))],
        out_specs=pl.BlockSpec((tm, tn), lambda i, j: (i, j)),
        out_shape=jax.ShapeDtypeStruct((m, d), F32),
        compiler_params=_cparams(("arbitrary", "arbitrary")),
        name="proj_res",
    )(x, merged, w)


def _ffn_kernel(x_ref, g_ref, w1_ref, w3_ref, w2_ref, o_ref, h_ref):
    f = pl.program_id(1)

    @pl.when(f == 0)
    def _():
        x = x_ref[...]
        r = lax.rsqrt(jnp.mean(x * x, axis=-1, keepdims=True) + EPS)
        h_ref[...] = ((x * r) * g_ref[...]).astype(BF16)
        o_ref[...] = x

    h = h_ref[...]
    a = jnp.dot(h, w1_ref[...], preferred_element_type=F32)
    b = jnp.dot(h, w3_ref[...], preferred_element_type=F32)
    act = (jax.nn.silu(a) * b).astype(BF16)
    o_ref[...] += jnp.dot(act, w2_ref[...], preferred_element_type=F32)


def _ffn(x, g, w1, w3, w2, layer, tm, tf):
    m, d = x.shape
    ff = w1.shape[2]
    return pl.pallas_call(
        _ffn_kernel,
        grid=(m // tm, ff // tf),
        in_specs=[pl.BlockSpec((tm, d), lambda i, f: (i, 0)),
                  _layer_spec(g, layer, (1, d), lambda i, f: (0, 0)),
                  _layer_spec(w1, layer, (d, tf), lambda i, f: (0, f)),
                  _layer_spec(w3, layer, (d, tf), lambda i, f: (0, f)),
                  _layer_spec(w2, layer, (tf, d), lambda i, f: (f, 0))],
        out_specs=pl.BlockSpec((tm, d), lambda i, f: (i, 0)),
        out_shape=jax.ShapeDtypeStruct((m, d), F32),
        scratch_shapes=[pltpu.VMEM((tm, d), BF16)],
        compiler_params=_cparams(("arbitrary", "arbitrary")),
        name="ffn",
    )(x, g, w1, w3, w2)


def kernel(x_prompt, x_sample, mem_prompt, cache_k, cache_v, cache_kidx, cache_mem_k, cache_mem_v, state_ssm_re, state_ssm_im, page_table, norm1_g, w_in, q_norm_g, k_norm_g, ssm_lam_re, ssm_lam_im, ssm_log_dt, ssm_b_re, ssm_b_im, ssm_c_re, ssm_c_im, ssm_d, w_glu, mem_norm_g, w_mem_kv, mq_norm_g, mk_norm_g, w_br_attn, w_br_ssm, w_br_mem, w_o, norm2_g, w_ff1, w_ff3, w_ff2):
    depth = w_in.shape[0]
    bp, tp, d = x_prompt.shape
    bs, ts, _ = x_sample.shape
    rp, rs = bp * tp, bs * ts
    m_tok = mem_prompt.shape[1]
    mw = MEM_HEADS * HEAD_DIM
    kvw = N_KV_HEADS * HEAD_DIM
    groups = ssm_lam_re.shape[1]
    tm = _row_tile(rp + rs)

    x = jnp.concatenate([x_prompt.reshape(rp, d), x_sample.reshape(rs, d)], axis=0)
    mem = mem_prompt.reshape(bp * m_tok, d)
    cmk = cache_mem_k.reshape(depth, bs, m_tok * MEM_HEADS, HEAD_DIM)
    cmv = cache_mem_v.reshape(depth, bs, m_tok * MEM_HEADS, HEAD_DIM)
    outs = [[] for _ in range(12)]
    nc_p, nc_s = tp // SSM_CHUNK, ts // SSM_CHUNK
    n_steps = max(1, int(math.log2(nc_p)))

    bf = lambda w: w.astype(BF16)
    w_mem_kv_b, w_glu_b, w_o_b = bf(w_mem_kv), bf(w_glu), bf(w_o)
    w_ba_b, w_bs_b, w_bm_b = bf(w_br_attn), bf(w_br_ssm), bf(w_br_mem)
    w_ff1_b, w_ff3_b, w_ff2_b = bf(w_ff1), bf(w_ff3), bf(w_ff2)
    mats = jax.vmap(functools.partial(_ssm_matrices, n_steps=n_steps))(
        ssm_lam_re, ssm_lam_im, ssm_log_dt, ssm_b_re, ssm_b_im, ssm_c_re, ssm_c_im, ssm_d)
    s0_prompt = jnp.zeros((depth, groups // 2, bp, SSM_PAIR_W), F32)
    s0_sample = jax.vmap(_state_to_pairs)(state_ssm_re, state_ssm_im)
    flags, gains = jax.vmap(_epilogue_rows)(q_norm_g, k_norm_g, mq_norm_g)
    mflag = jnp.concatenate([jnp.ones((mw,), F32), jnp.zeros((mw,), F32)])[None]
    mgains = jnp.concatenate([jnp.tile(mk_norm_g, (1, MEM_HEADS)), jnp.ones((depth, mw), F32)], axis=1)[:, None]
    g1, g2, gm = norm1_g[:, None], norm2_g[:, None], mem_norm_g[:, None]

    for l in range(depth):
        wp, wg, uw = _pack_w_in(w_in, l)
        z, h = _norm_matmul(x, g1[l], wp, flags[l], gains[l], tm=tm, tn=1024, emit_h=True)

        mkv = _norm_matmul(mem, gm[l], w_mem_kv_b, mflag, mgains[l],
                           tm=bp * m_tok, tn=2 * mw, emit_h=False, layer=l)
        mk_p = mkv[:, :mw].reshape(bp, m_tok, MEM_HEADS, HEAD_DIM)
        mv_p = mkv[:, mw:].reshape(bp, m_tok, MEM_HEADS, HEAD_DIM)

        a_p = _dsa_prompt(z, bp, tp)
        a_s = _dsa_sample(z, rp, bs, ts, l, cache_k, cache_v, cache_kidx, page_table)
        a_out = jnp.concatenate([a_p, a_s], axis=0)

        y_p, sf_p = _ssm(z, 0, rp, mats, s0_prompt, l, bp, nc_p)
        y_s, sf_s = _ssm(z, rp, rs, mats, s0_sample, l, bs, nc_s)
        s_out = _glu(jnp.concatenate([y_p, y_s], axis=0), w_glu_b, l, tm)

        m_p = _mem_attn(z, 0, bp, tp, 512, mk_p.reshape(1, bp, m_tok * MEM_HEADS, HEAD_DIM),
                        mv_p.reshape(1, bp, m_tok * MEM_HEADS, HEAD_DIM), 0)
        m_s = _mem_attn(z, rp, bs, ts, ts, cmk, cmv, l)
        m_out = jnp.concatenate([m_p, m_s], axis=0)

        merged = _merge(h, a_out, s_out, m_out, wg, w_ba_b, w_bs_b, w_bm_b, l, tm, min(512, d))
        x = _proj_res(x, merged, w_o_b, l, tm, d)
        x = _ffn(x, g2, w_ff1_b, w_ff3_b, w_ff2_b, l, tm, 512)

        srp, sip = _state_from_pairs(sf_p)
        srs, sis = _state_from_pairs(sf_s)
        new = [
            z[:rp, C_K:C_K + kvw].reshape(bp, tp, N_KV_HEADS, HEAD_DIM),
            z[:rp, C_V:C_V + kvw].reshape(bp, tp, N_KV_HEADS, HEAD_DIM),
            z[:rp, C_KI:C_KI + IDX_DIM].reshape(bp, tp, IDX_DIM),
            mk_p, mv_p, srp, sip,
            z[rp:, C_K:C_K + kvw].reshape(bs, ts, N_KV_HEADS, HEAD_DIM),
            z[rp:, C_V:C_V + kvw].reshape(bs, ts, N_KV_HEADS, HEAD_DIM),
            z[rp:, C_KI:C_KI + IDX_DIM].reshape(bs, ts, IDX_DIM),
            srs, sis,
        ]
        for o, v in zip(outs, new):
            o.append(v)

    st = lambda xs: jnp.stack(xs, axis=0)
    return (x[:rp].reshape(bp, tp, d), x[rp:].reshape(bs, ts, d), *[st(o) for o in outs])
```

```python
import functools
import math

import jax
import jax.numpy as jnp
from jax import lax
from jax.experimental import pallas as pl
from jax.experimental.pallas import tpu as pltpu

F32 = jnp.float32
BF16 = jnp.bfloat16
I32 = jnp.int32

LANES = 128
SUBLANES = 8
BF16_SUBLANES = 16
EPS = 1e-6
HEAD_DIM = 128
N_HEADS = 6
N_KV_HEADS = 2
KV_GROUP = N_HEADS // N_KV_HEADS
IDX_HEADS = 4
IDX_DIM = 64
TOPK_MAX = 256
SSM_GROUP = 16
SSM_STATE = 64
SSM_CHUNK = SUBLANES
SSM_PAIR_W = 2 * SSM_CHUNK * SSM_GROUP
PAIRS_PER_BLOCK = LANES // (2 * SSM_GROUP)
MEM_HEADS = 4
KEY_MIN = -2 ** 31
NEG_INF = float("-inf")
NEG_BIAS = -1e30
M_INIT = -1e29
VMEM_LIMIT = 56 * 1024 * 1024
MAX_ROW_TILE = 1024

C_Q, C_K, C_V, C_QI, C_MQ, C_KI, C_WI, C_U = 0, 768, 1024, 1280, 1536, 2048, 2112, 2176
Z_COLS = 3072


def _cparams(sem):
    return pltpu.CompilerParams(dimension_semantics=sem, vmem_limit_bytes=VMEM_LIMIT)


def _layer_spec(w, layer, block, index_map):
    assert w.ndim == len(block) + 1
    return pl.BlockSpec((None,) + tuple(block), lambda *idx: (layer,) + tuple(index_map(*idx)))


def _row_tile(m):
    return max(t for t in range(BF16_SUBLANES, MAX_ROW_TILE + 1, BF16_SUBLANES) if m % t == 0)


def _pack_w_in(w_in):
    depth, d, cols = w_in.shape
    aw, kvw, iqw, mw = N_HEADS * HEAD_DIM, N_KV_HEADS * HEAD_DIM, IDX_HEADS * IDX_DIM, MEM_HEADS * HEAD_DIM
    o_ki = aw + 2 * kvw + iqw
    o_u = o_ki + IDX_DIM + IDX_HEADS
    uw = cols - o_u - mw - 3 * d
    o_mq = o_u + uw
    o_g = o_mq + mw
    w_t = jnp.swapaxes(w_in, 1, 2)
    zeros = lambda n: jnp.zeros((depth, n, d), w_in.dtype)
    wp = jnp.concatenate([w_t[:, 0:o_ki], w_t[:, o_mq:o_g], w_t[:, o_ki:o_u], zeros(C_U - C_WI - IDX_HEADS),
                          w_t[:, o_u:o_mq], zeros(Z_COLS - C_U - uw)], axis=1)
    return wp.astype(BF16), w_t[:, o_g:].astype(BF16), uw


def _epilogue_rows(qn, kn, mqn):
    aw, kvw, iqw, mw = N_HEADS * HEAD_DIM, N_KV_HEADS * HEAD_DIM, IDX_HEADS * IDX_DIM, MEM_HEADS * HEAD_DIM
    ones = lambda n: jnp.ones((n,), F32)
    zer = lambda n: jnp.zeros((n,), F32)
    flag = jnp.concatenate([ones(aw + kvw), zer(kvw + iqw), ones(mw), zer(Z_COLS - C_KI)])[None]
    gain = jnp.concatenate([
        jnp.tile(qn, N_HEADS), jnp.tile(kn, N_KV_HEADS), ones(kvw + iqw), jnp.tile(mqn, MEM_HEADS),
        ones(IDX_DIM), jnp.full((IDX_HEADS,), IDX_HEADS ** -0.5, F32), ones(Z_COLS - C_WI - IDX_HEADS)])[None]
    return flag, gain


NT_DIMS = (((1,), (1,)), ((), ()))


def _norm_matmul_kernel(x_ref, g_ref, w_ref, flag_ref, gain_ref, z_ref, *rest, emit_h, w_rows_out):
    if emit_h:
        h_ref, hs_ref = rest
    else:
        (hs_ref,) = rest
    j = pl.program_id(1)

    @pl.when(j == 0)
    def _():
        x = x_ref[...]
        r = lax.rsqrt(jnp.mean(x * x, axis=-1, keepdims=True) + EPS)
        h = ((x * r) * g_ref[...]).astype(BF16)
        hs_ref[...] = h
        if emit_h:
            h_ref[...] = h

    if w_rows_out:
        acc = lax.dot_general(hs_ref[...], w_ref[...], NT_DIMS, preferred_element_type=F32)
    else:
        acc = jnp.dot(hs_ref[...], w_ref[...], preferred_element_type=F32)
    for c in range(acc.shape[1] // LANES):
        sl = slice(c * LANES, (c + 1) * LANES)
        y = acc[:, sl]
        r = lax.rsqrt(jnp.mean(y * y, axis=-1, keepdims=True) + EPS)
        scale = jnp.where(flag_ref[:, sl] > 0.0, r, 1.0) * gain_ref[:, sl]
        z_ref[:, sl] = y * scale


def _norm_matmul(x, g, w, flag, gain, *, layer, tm, tn, emit_h, w_rows_out):
    m, d = x.shape
    if w_rows_out:
        n = w.shape[1]
        w_spec = _layer_spec(w, layer, (tn, d), lambda i, j: (j, 0))
    else:
        n = w.shape[2]
        w_spec = _layer_spec(w, layer, (d, tn), lambda i, j: (0, j))
    out_shape = [jax.ShapeDtypeStruct((m, n), F32)]
    out_specs = [pl.BlockSpec((tm, tn), lambda i, j: (i, j))]
    if emit_h:
        out_shape.append(jax.ShapeDtypeStruct((m, d), BF16))
        out_specs.append(pl.BlockSpec((tm, d), lambda i, j: (i, 0)))
    res = pl.pallas_call(
        functools.partial(_norm_matmul_kernel, emit_h=emit_h, w_rows_out=w_rows_out),
        grid=(m // tm, n // tn),
        in_specs=[
            pl.BlockSpec((tm, d), lambda i, j: (i, 0)),
            pl.BlockSpec((1, d), lambda i, j: (0, 0)),
            w_spec,
            pl.BlockSpec((1, tn), lambda i, j: (0, j)),
            pl.BlockSpec((1, tn), lambda i, j: (0, j)),
        ],
        out_specs=out_specs,
        out_shape=out_shape,
        scratch_shapes=[pltpu.VMEM((tm, d), BF16)],
        compiler_params=_cparams(("arbitrary", "arbitrary")),
        name="norm_matmul",
    )(x, g, w, flag, gain)
    return res if emit_h else res[0]


def _ordered_to_f32(key):
    return pltpu.bitcast(jnp.where(key >= 0, key, key ^ 0x7FFFFFFF), F32)


def _count(score_ref, n_slab, slab_w, indicator):
    rows = score_ref.shape[0]

    def slab(s, acc):
        static = isinstance(s, int)
        base = s * slab_w if static else pl.multiple_of(s * slab_w, slab_w)
        parts = []
        for c in range(slab_w // LANES):
            off = base + c * LANES
            x = score_ref[:, off:off + LANES] if static else score_ref[:, pl.ds(off, LANES)]
            parts.append(indicator(x, off))
        while len(parts) > 1:
            parts = [a + b for a, b in zip(parts[0::2], parts[1::2])] + ([parts[-1]] if len(parts) % 2 else [])
        return acc + parts[0]

    acc = jnp.zeros((rows, LANES), F32)
    if isinstance(n_slab, int):
        for s in range(n_slab):
            acc = slab(s, acc)
    else:
        acc = lax.fori_loop(0, n_slab, slab, acc)
    return jnp.sum(acc, axis=1, keepdims=True)


def _kth_largest(score_ref, n_slab, slab_w, k):
    rows = score_ref.shape[0]
    count_ge = lambda cand: _count(score_ref, n_slab, slab_w, lambda x, off: jnp.where(x >= cand, 1.0, 0.0))
    lo = jnp.where(count_ge(jnp.zeros((rows, 1), F32)) >= k, 0, KEY_MIN)

    def body(it, lo):
        cand = lo | (1 << (30 - it))
        return jnp.where(count_ge(_ordered_to_f32(cand)) >= k, cand, lo)

    lo = lax.fori_loop(0, 31, body, lo)
    return jnp.where(lo == KEY_MIN, NEG_INF, _ordered_to_f32(lo))


def _stack_heads(x, first, count, width):
    return jnp.concatenate([x[:, (first + r) * width:(first + r + 1) * width] for r in range(count)], axis=0)


def _indexer_scores(qi_stack, w_cols, ki_c, rows, keys_on_lanes=False):
    dims = (((1,), (0,)), ((), ())) if keys_on_lanes else (((1,), (1,)), ((), ()))
    logits = lax.dot_general(qi_stack, ki_c, dims, preferred_element_type=F32)
    score = jnp.maximum(logits[0:rows], 0.0) * w_cols[0]
    for h in range(1, IDX_HEADS):
        score = score + jnp.maximum(logits[h * rows:(h + 1) * rows], 0.0) * w_cols[h]
    return score


TQ = 128
KC = 512
KC_ATT = 1024
ROW_SLAB = KC
LOG2_E = math.log2(math.e)


def _count_keys(score_ref, n_slab, indicator):
    lanes = score_ref.shape[1]

    def slab(s, acc):
        base = pl.multiple_of(s * ROW_SLAB, ROW_SLAB)
        parts = [indicator(score_ref[pl.ds(base + r * SUBLANES, SUBLANES), :]) for r in range(ROW_SLAB // SUBLANES)]
        while len(parts) > 1:
            parts = [a + b for a, b in zip(parts[0::2], parts[1::2])] + ([parts[-1]] if len(parts) % 2 else [])
        return acc + parts[0]

    acc = lax.fori_loop(0, n_slab, slab, jnp.zeros((SUBLANES, lanes), F32))
    return jnp.sum(acc, axis=0, keepdims=True)


def _kth_largest_keys(score_ref, n_slab, k):
    lanes = score_ref.shape[1]
    count_ge = lambda cand: _count_keys(score_ref, n_slab, lambda x: jnp.where(x >= cand, 1.0, 0.0))
    lo = jnp.where(count_ge(jnp.zeros((1, lanes), F32)) >= k, 0, KEY_MIN)

    def body(it, lo):
        cand = lo | (1 << (30 - it))
        return jnp.where(count_ge(_ordered_to_f32(cand)) >= k, cand, lo)

    lo = lax.fori_loop(0, 31, body, lo)
    return jnp.where(lo == KEY_MIN, NEG_INF, _ordered_to_f32(lo))


def _bias_keys(score_ref, bias_ref, n_slab, k, thr):
    lanes = score_ref.shape[1]
    need = k - _count_keys(score_ref, n_slab, lambda x: jnp.where(x > thr, 1.0, 0.0))
    tri = (lax.broadcasted_iota(I32, (LANES, LANES), 0)
           >= lax.broadcasted_iota(I32, (LANES, LANES), 1)).astype(BF16)

    def body(s, seen):
        base = pl.multiple_of(s * ROW_SLAB, ROW_SLAB)
        xs = [score_ref[pl.ds(base + j * LANES, LANES), :] for j in range(ROW_SLAB // LANES)]
        eqs = [jnp.where(x == thr, 1.0, 0.0) for x in xs]
        prefixes = [jnp.dot(tri, e.astype(BF16), preferred_element_type=F32) for e in eqs]
        for j, (x, eqf, prefix) in enumerate(zip(xs, eqs, prefixes)):
            keep = jnp.where(x > thr, 1.0, jnp.where(seen + prefix <= need, eqf, 0.0))
            bias_ref[pl.ds(base + j * LANES, LANES), :] = jnp.where(
                keep > 0.0, jnp.where(x > NEG_INF, 0.0, NEG_BIAS), NEG_BIAS)
            seen = seen + prefix[LANES - 1:LANES, :]
        return seen

    lax.fori_loop(0, n_slab, body, jnp.zeros((1, lanes), F32))


def _dsa_prompt_kernel(q_ref, qi_ref, kiwq_ref, k_ref, v_ref, kiw_ref, o_ref,
                       kbf_ref, vt_ref, kibf_ref, score_ref, bias_ref, *, topk, kc_att):
    i = pl.program_id(1)
    seq = k_ref.shape[0]

    @pl.when(i == 0)
    def _():
        kbf_ref[...] = k_ref[...].astype(BF16)
        kibf_ref[...] = kiw_ref[...].astype(BF16)
        for c in range(seq // KC):
            vt_ref[:, c * KC:(c + 1) * KC] = v_ref[c * KC:(c + 1) * KC, :].T.astype(BF16)

    n_kc = (i * TQ + TQ + KC - 1) // KC
    n_att = (n_kc * KC + kc_att - 1) // kc_att
    qi_stack = _stack_heads(qi_ref[...], 0, IDX_HEADS, IDX_DIM).astype(BF16)
    w_t = kiwq_ref[...].T
    scale_i = IDX_DIM ** -0.5
    w_rows = [w_t[IDX_DIM + h:IDX_DIM + h + 1, :] * scale_i for h in range(IDX_HEADS)]
    qpos = i * TQ + lax.broadcasted_iota(I32, (KC, TQ), 1)

    def score_body(c, _):
        off = pl.multiple_of(c * KC, KC)
        logits = lax.dot_general(kibf_ref[pl.ds(off, KC), 0:IDX_DIM], qi_stack, (((1,), (1,)), ((), ())),
                                 preferred_element_type=F32)
        score = jnp.maximum(logits[:, 0:TQ], 0.0) * w_rows[0]
        for h in range(1, IDX_HEADS):
            score = score + jnp.maximum(logits[:, h * TQ:(h + 1) * TQ], 0.0) * w_rows[h]
        kpos = off + lax.broadcasted_iota(I32, (KC, TQ), 0)
        score_ref[pl.ds(off, KC), :] = jnp.where(kpos > qpos, NEG_INF, score)
        return 0

    lax.fori_loop(0, n_kc, score_body, 0)
    thr = _kth_largest_keys(score_ref, n_kc, topk)
    _bias_keys(score_ref, bias_ref, n_kc, topk, thr)

    @pl.when(n_att * kc_att > n_kc * KC)
    def _():
        bias_ref[pl.ds(pl.multiple_of(n_kc * KC, KC), KC), :] = jnp.full((KC, TQ), NEG_BIAS, F32)

    q = q_ref[...] * (HEAD_DIM ** -0.5 * LOG2_E)
    q_gs = [_stack_heads(q, g * KV_GROUP, KV_GROUP, HEAD_DIM).astype(BF16) for g in range(N_KV_HEADS)]
    gq = KV_GROUP * TQ

    def att_body(c, carries):
        off = pl.multiple_of(c * kc_att, kc_att)
        b = bias_ref[pl.ds(off, kc_att), :]
        bias = jnp.concatenate([b] * KV_GROUP, axis=1)
        out = []
        for g in range(N_KV_HEADS):
            gs = slice(g * HEAD_DIM, (g + 1) * HEAD_DIM)
            m, l, acc = carries[g]
            s = lax.dot_general(kbf_ref[pl.ds(off, kc_att), gs], q_gs[g], (((1,), (1,)), ((), ())),
                                preferred_element_type=F32) + bias
            m_new = jnp.maximum(m, jnp.max(s, axis=0, keepdims=True))
            alpha = jnp.exp2(m - m_new)
            p = jnp.exp2(s - m_new)
            l = alpha * l + jnp.sum(p, axis=0, keepdims=True)
            acc = alpha * acc + jnp.dot(vt_ref[gs, pl.ds(off, kc_att)], p.astype(BF16),
                                        preferred_element_type=F32)
            out.append((m_new, l, acc))
        return tuple(out)

    init = (jnp.full((1, gq), M_INIT, F32), jnp.zeros((1, gq), F32), jnp.zeros((HEAD_DIM, gq), F32))
    carries = lax.fori_loop(0, n_att, att_body, (init,) * N_KV_HEADS)
    for g, (_, l, acc) in enumerate(carries):
        o_t = acc / l
        for r in range(KV_GROUP):
            hh = g * KV_GROUP + r
            o_ref[:, hh * HEAD_DIM:(hh + 1) * HEAD_DIM] = o_t[:, r * TQ:(r + 1) * TQ].T


def _dsa_prompt(z, batch, seq):
    nq = seq // TQ
    topk = min(TOPK_MAX, seq // 4)
    aw, kvw, iqw = N_HEADS * HEAD_DIM, N_KV_HEADS * HEAD_DIM, IDX_HEADS * IDX_DIM
    return pl.pallas_call(
        functools.partial(_dsa_prompt_kernel, topk=topk, kc_att=min(KC_ATT, seq)),
        grid=(batch, nq),
        in_specs=[
            pl.BlockSpec((TQ, aw), lambda b, i: (b * nq + i, C_Q // aw)),
            pl.BlockSpec((TQ, iqw), lambda b, i: (b * nq + i, C_QI // iqw)),
            pl.BlockSpec((TQ, LANES), lambda b, i: (b * nq + i, C_KI // LANES)),
            pl.BlockSpec((seq, kvw), lambda b, i: (b, C_K // kvw)),
            pl.BlockSpec((seq, kvw), lambda b, i: (b, C_V // kvw)),
            pl.BlockSpec((seq, LANES), lambda b, i: (b, C_KI // LANES)),
        ],
        out_specs=pl.BlockSpec((TQ, aw), lambda b, i: (b * nq + i, 0)),
        out_shape=jax.ShapeDtypeStruct((batch * seq, aw), F32),
        scratch_shapes=[
            pltpu.VMEM((seq, kvw), BF16),
            pltpu.VMEM((kvw, seq), BF16),
            pltpu.VMEM((seq, LANES), BF16),
            pltpu.VMEM((seq, TQ), F32),
            pltpu.VMEM((seq, TQ), F32),
        ],
        compiler_params=_cparams(("arbitrary", "arbitrary")),
        name="dsa_prompt",
    )(z, z, z, z, z, z)


KC_S = 1024


def _tie_cut(score_ref, n128, k, thr):
    rows = score_ref.shape[0]
    width = n128 * LANES
    need = k - _count(score_ref, 1, width, lambda x, off: jnp.where(x > thr, 1.0, 0.0))
    ties = _count(score_ref, 1, width, lambda x, off: jnp.where(x == thr, 1.0, 0.0))
    lane = lax.broadcasted_iota(I32, (rows, LANES), 1)
    everything = jnp.full((rows, 1), width - 1, I32)

    def bisect():
        def ties_upto(j):
            return _count(score_ref, 1, width,
                          lambda x, off: jnp.where(x == thr, jnp.where(lane + off <= j, 1.0, 0.0), 0.0))

        def body(_, c):
            lo, hi = c
            mid = (lo + hi) >> 1
            ok = ties_upto(mid) >= need
            return jnp.where(ok, lo, mid), jnp.where(ok, mid, hi)

        steps = int(math.ceil(math.log2(width + 1)))
        return lax.fori_loop(0, steps, body, (jnp.full((rows, 1), -1, I32), everything))[1]

    return lax.cond(jnp.max(ties - need) > 0.0, bisect, lambda: everything)


def _write_bias(score_ref, bias_ref, n128, thr, cut):
    rows = score_ref.shape[0]
    lane = lax.broadcasted_iota(I32, (rows, LANES), 1)
    for c in range(n128):
        x = score_ref[:, c * LANES:(c + 1) * LANES]
        tie = jnp.where(x == thr, jnp.where(lane + c * LANES <= cut, 1.0, 0.0), 0.0)
        keep = jnp.where(x > thr, 1.0, tie)
        bias_ref[:, c * LANES:(c + 1) * LANES] = jnp.where(keep > 0.0, jnp.where(x > NEG_INF, 0.0, NEG_BIAS),
                                                           NEG_BIAS)


def _lane_fold(x, op):
    parts = [x[:, c * LANES:(c + 1) * LANES] for c in range(x.shape[1] // LANES)]
    while len(parts) > 1:
        parts = [op(a, b) for a, b in zip(parts[0::2], parts[1::2])] + ([parts[-1]] if len(parts) % 2 else [])
    return parts[0]


def _dsa_sample_kernel(pt_ref, q_ref, qi_ref, kiw_ref, kn_ref, vn_ref, ck_ref, cv_ref, cki_ref, o_ref,
                       kbuf, vbuf, kibuf, score_ref, bias_ref, s_ref, sem, *, layer, n_pages, page, tq, topk):
    b = pl.program_id(0)
    nb = pl.num_programs(0)
    past = n_pages * page
    slot = b % 2
    streams = ((ck_ref, kbuf, page * N_KV_HEADS), (cv_ref, vbuf, page * N_KV_HEADS), (cki_ref, kibuf, IDX_DIM))

    def page_copy(which, bb, sl, p):
        src, dst, rpp = streams[which]
        return pltpu.make_async_copy(src.at[layer, pt_ref[bb * n_pages + p]],
                                     dst.at[sl, pl.ds(p * rpp, rpp)], sem.at[which, sl])

    def start_fetch(bb, sl):
        def body(p, _):
            for which in range(len(streams)):
                page_copy(which, bb, sl, p).start()
            return 0
        lax.fori_loop(0, n_pages, body, 0)

    def wait_fetch(bb, sl, which):
        def body(p, _):
            page_copy(which, bb, sl, p).wait()
            return 0
        lax.fori_loop(0, n_pages, body, 0)

    @pl.when(b == 0)
    def _():
        start_fetch(0, 0)

    @pl.when(b + 1 < nb)
    def _():
        start_fetch(b + 1, 1 - slot)

    qi_stack = _stack_heads(qi_ref[...], 0, IDX_HEADS, IDX_DIM).astype(BF16)
    kiw = kiw_ref[...]
    scale_i = IDX_DIM ** -0.5
    w_cols = [kiw[:, IDX_DIM + h:IDX_DIM + h + 1] * scale_i for h in range(IDX_HEADS)]
    n128 = past // LANES + 1
    ppc = KC // page

    wait_fetch(b, slot, 2)
    for c in range(past // KC):
        ki_t = jnp.concatenate([kibuf[slot, (c * ppc + j) * IDX_DIM:(c * ppc + j + 1) * IDX_DIM, :]
                                for j in range(ppc)], axis=1).astype(BF16)
        score_ref[:, c * KC:(c + 1) * KC] = _indexer_scores(qi_stack, w_cols, ki_t, tq, keys_on_lanes=True)
    pad = jnp.zeros((LANES - tq, LANES), F32)
    ki_new = jnp.concatenate([kiw, pad], axis=0)[:, 0:IDX_DIM].astype(BF16)
    score_new = _indexer_scores(qi_stack, w_cols, ki_new, tq)
    newer = lax.broadcasted_iota(I32, (tq, LANES), 1) > lax.broadcasted_iota(I32, (tq, LANES), 0)
    score_ref[:, past:past + LANES] = jnp.where(newer, NEG_INF, score_new)

    thr = _kth_largest(score_ref, 1, n128 * LANES, topk)
    _write_bias(score_ref, bias_ref, n128, thr, _tie_cut(score_ref, n128, topk, thr))

    wait_fetch(b, slot, 0)
    wait_fetch(b, slot, 1)
    q = q_ref[...] * (HEAD_DIM ** -0.5)
    q_gs = [_stack_heads(q, g * KV_GROUP, KV_GROUP, HEAD_DIM).astype(BF16) for g in range(N_KV_HEADS)]
    kv_pad = jnp.zeros((LANES - tq, N_KV_HEADS * HEAD_DIM), F32)
    k_new = jnp.concatenate([kn_ref[...], kv_pad], axis=0).astype(BF16)
    v_new = jnp.concatenate([vn_ref[...], kv_pad], axis=0).astype(BF16)
    chunks = [(c * KC_S, KC_S) for c in range(past // KC_S)] + [(past, LANES)]

    def keys_of(buf, new, g, start, size):
        if start == past:
            return new[:, g * HEAD_DIM:(g + 1) * HEAD_DIM]
        return buf[slot, pl.ds(start * N_KV_HEADS + g, size, stride=N_KV_HEADS), :].astype(BF16)

    m_acc = [jnp.full((KV_GROUP * tq, LANES), M_INIT, F32) for _ in range(N_KV_HEADS)]
    for start, size in chunks:
        bias = jnp.concatenate([bias_ref[:, start:start + size]] * KV_GROUP, axis=0)
        for g in range(N_KV_HEADS):
            s = lax.dot_general(q_gs[g], keys_of(kbuf, k_new, g, start, size), (((1,), (1,)), ((), ())),
                                preferred_element_type=F32) + bias
            s_ref[g, :, start:start + size] = s
            m_acc[g] = jnp.maximum(m_acc[g], _lane_fold(s, jnp.maximum))
    m = [jnp.max(a, axis=1, keepdims=True) for a in m_acc]
    l_acc = [jnp.zeros((KV_GROUP * tq, LANES), F32) for _ in range(N_KV_HEADS)]
    acc = [jnp.zeros((KV_GROUP * tq, HEAD_DIM), F32) for _ in range(N_KV_HEADS)]
    for start, size in chunks:
        for g in range(N_KV_HEADS):
            p = jnp.exp(s_ref[g, :, start:start + size] - m[g])
            l_acc[g] = l_acc[g] + _lane_fold(p, jnp.add)
            acc[g] = acc[g] + jnp.dot(p.astype(BF16), keys_of(vbuf, v_new, g, start, size),
                                      preferred_element_type=F32)
    for g in range(N_KV_HEADS):
        o = acc[g] / jnp.sum(l_acc[g], axis=1, keepdims=True)
        for r in range(KV_GROUP):
            hh = g * KV_GROUP + r
            o_ref[:, hh * HEAD_DIM:(hh + 1) * HEAD_DIM] = o[r * tq:(r + 1) * tq]


def _dsa_sample(z, row0, nb, tq, layer, cache_k, cache_v, cache_kidx, page_table):
    depth, n_pool, page = cache_k.shape[:3]
    n_pages = page_table.shape[1]
    past = n_pages * page
    topk = min(TOPK_MAX, (past + tq) // 4)
    aw, kvw, iqw = N_HEADS * HEAD_DIM, N_KV_HEADS * HEAD_DIM, IDX_HEADS * IDX_DIM
    rb = row0 // tq
    grid_spec = pltpu.PrefetchScalarGridSpec(
        num_scalar_prefetch=1,
        grid=(nb,),
        in_specs=[
            pl.BlockSpec((tq, aw), lambda b, pt: (rb + b, C_Q // aw)),
            pl.BlockSpec((tq, iqw), lambda b, pt: (rb + b, C_QI // iqw)),
            pl.BlockSpec((tq, LANES), lambda b, pt: (rb + b, C_KI // LANES)),
            pl.BlockSpec((tq, kvw), lambda b, pt: (rb + b, C_K // kvw)),
            pl.BlockSpec((tq, kvw), lambda b, pt: (rb + b, C_V // kvw)),
            pl.BlockSpec(memory_space=pl.ANY),
            pl.BlockSpec(memory_space=pl.ANY),
            pl.BlockSpec(memory_space=pl.ANY),
        ],
        out_specs=pl.BlockSpec((tq, aw), lambda b, pt: (b, 0)),
        scratch_shapes=[
            pltpu.VMEM((2, past * N_KV_HEADS, HEAD_DIM), F32),
            pltpu.VMEM((2, past * N_KV_HEADS, HEAD_DIM), F32),
            pltpu.VMEM((2, n_pages * IDX_DIM, page), F32),
            pltpu.VMEM((tq, past + LANES), F32),
            pltpu.VMEM((tq, past + LANES), F32),
            pltpu.VMEM((N_KV_HEADS, KV_GROUP * tq, past + LANES), F32),
            pltpu.SemaphoreType.DMA((3, 2)),
        ],
    )
    ck = cache_k.reshape(depth, n_pool, page * N_KV_HEADS, HEAD_DIM)
    cv = cache_v.reshape(depth, n_pool, page * N_KV_HEADS, HEAD_DIM)
    return pl.pallas_call(
        functools.partial(_dsa_sample_kernel, layer=layer, n_pages=n_pages, page=page, tq=tq, topk=topk),
        grid_spec=grid_spec,
        out_shape=jax.ShapeDtypeStruct((nb * tq, aw), F32),
        compiler_params=_cparams(("arbitrary",)),
        name="dsa_sample",
    )(page_table.reshape(-1), z, z, z, z, z, ck, cv, jnp.swapaxes(cache_kidx, 2, 3))


def _mem_attn_kernel(q_ref, k_ref, v_ref, o_ref):
    q = q_ref[...] * (HEAD_DIM ** -0.5)
    m_tok = k_ref.shape[0] // MEM_HEADS
    for h in range(MEM_HEADS):
        hs = slice(h * HEAD_DIM, (h + 1) * HEAD_DIM)
        rows = pl.ds(h, m_tok, stride=MEM_HEADS)
        s = lax.dot_general(q[:, hs].astype(BF16), k_ref[rows, :].astype(BF16),
                            (((1,), (1,)), ((), ())), preferred_element_type=F32)
        m = jnp.max(s, axis=1, keepdims=True)
        p = jnp.exp(s - m)
        l = jnp.sum(p, axis=1, keepdims=True)
        o = jnp.dot(p.astype(BF16), v_ref[rows, :].astype(BF16), preferred_element_type=F32)
        o_ref[:, hs] = o / l


def _mem_attn(z, row0, nb, t, tq, mk, mv, layer):
    mw = MEM_HEADS * HEAD_DIM
    nq = t // tq
    rb = row0 // tq
    kv_spec = pl.BlockSpec((None, None) + mk.shape[2:], lambda b, i: (layer, b, 0, 0))
    return pl.pallas_call(
        _mem_attn_kernel,
        grid=(nb, nq),
        in_specs=[pl.BlockSpec((tq, mw), lambda b, i: (rb + b * nq + i, C_MQ // mw)), kv_spec, kv_spec],
        out_specs=pl.BlockSpec((tq, mw), lambda b, i: (b * nq + i, 0)),
        out_shape=jax.ShapeDtypeStruct((nb * t, mw), F32),
        compiler_params=_cparams(("arbitrary", "arbitrary")),
        name="mem_attn",
    )(z, mk, mv)


def _shift_rows(x, sh):
    rows = lax.broadcasted_iota(I32, x.shape, 0)
    return jnp.where(rows >= sh, pltpu.roll(x, sh, axis=0), 0.0)


def _ssm_pair(u, m_w, wx_w, wc_w, pr, pi, s0, nb, nc):
    half = SSM_PAIR_W // 2
    y_in = jnp.dot(u, m_w, preferred_element_type=F32)
    x = jnp.dot(u, wx_w, preferred_element_type=F32)
    xr, xi = x[:, :half], x[:, half:]
    s0r, s0i = s0[:, :half], s0[:, half:]
    ar, ai = pr[0:1], pi[0:1]
    if nc == 1:
        sr = xr + ar * s0r - ai * s0i
        si = xi + ar * s0i + ai * s0r
        pvr, pvi = s0r, s0i
        sfin = jnp.concatenate([sr, si], axis=1)
    else:
        prev_r, prev_i, fin = [], [], []
        first = lax.broadcasted_iota(I32, (nc, half), 0) == 0
        for b in range(nb):
            rs = slice(b * nc, (b + 1) * nc)
            b0r, b0i = s0r[b:b + 1], s0i[b:b + 1]
            sr = xr[rs] + jnp.where(first, ar * b0r - ai * b0i, 0.0)
            si = xi[rs] + jnp.where(first, ar * b0i + ai * b0r, 0.0)
            for mstep in range(int(math.log2(nc))):
                sh = 2 ** mstep
                qr, qi = pr[mstep:mstep + 1], pi[mstep:mstep + 1]
                tr, ti = _shift_rows(sr, sh), _shift_rows(si, sh)
                sr, si = sr + qr * tr - qi * ti, si + qr * ti + qi * tr
            fin.append(jnp.concatenate([sr[nc - 1:nc], si[nc - 1:nc]], axis=1))
            prev_r.append(jnp.where(first, b0r, pltpu.roll(sr, 1, axis=0)))
            prev_i.append(jnp.where(first, b0i, pltpu.roll(si, 1, axis=0)))
        pvr = jnp.concatenate(prev_r, axis=0)
        pvi = jnp.concatenate(prev_i, axis=0)
        sfin = jnp.concatenate(fin, axis=0)
    prev = jnp.concatenate([pvr, pvi], axis=1).astype(BF16)
    return y_in + jnp.dot(prev, wc_w, preferred_element_type=F32), sfin


def _ssm_kernel(u_ref, m_ref, wx_ref, wc_ref, pr_ref, pi_ref, s0_ref, y_ref, sf_ref, *, nb, nc):
    nch = nb * nc
    gw = SSM_GROUP
    tok = [u_ref[pl.ds(i, nch, stride=SSM_CHUNK), :] for i in range(SSM_CHUNK)]
    ys = []
    for q in range(PAIRS_PER_BLOCK):
        base = q * 2 * gw
        u = jnp.concatenate([tok[i][:, base + gg * gw:base + (gg + 1) * gw]
                             for gg in range(2) for i in range(SSM_CHUNK)], axis=1).astype(BF16)
        y, sfin = _ssm_pair(u, m_ref[q], wx_ref[q], wc_ref[q], pr_ref[q], pi_ref[q], s0_ref[q], nb, nc)
        sf_ref[q] = sfin
        ys.append(y)
    for i in range(SSM_CHUNK):
        y_ref[pl.ds(i, nch, stride=SSM_CHUNK), :] = jnp.concatenate(
            [ys[q][:, (gg * SSM_CHUNK + i) * gw:(gg * SSM_CHUNK + i + 1) * gw]
             for q in range(PAIRS_PER_BLOCK) for gg in range(2)], axis=1)


def _ssm(z, row0, rows, mats, s0_pairs, layer, nb, nc):
    m_mat, wx, wc, pr, pi = mats
    _, npair, _, w = m_mat.shape
    pw = pr.shape[2]
    ppb = PAIRS_PER_BLOCK
    blk3 = lambda r, c: pl.BlockSpec((ppb, r, c), lambda j: (j, 0, 0))
    lay3 = lambda a, r, c: _layer_spec(a, layer, (ppb, r, c), lambda j: (j, 0, 0))
    return pl.pallas_call(
        functools.partial(_ssm_kernel, nb=nb, nc=nc),
        grid=(npair // ppb,),
        in_specs=[pl.BlockSpec((rows, LANES), lambda j: (row0 // rows, C_U // LANES + j)),
                  lay3(m_mat, w, w), lay3(wx, w, w), lay3(wc, w, w), lay3(pr, pw, LANES), lay3(pi, pw, LANES),
                  lay3(s0_pairs, nb, w)],
        out_specs=[pl.BlockSpec((rows, LANES), lambda j: (0, j)), blk3(nb, w)],
        out_shape=[jax.ShapeDtypeStruct((rows, npair * 2 * SSM_GROUP), F32),
                   jax.ShapeDtypeStruct((npair, nb, w), F32)],
        compiler_params=_cparams(("arbitrary",)),
        name="ssm",
    )(z, m_mat, wx, wc, pr, pi, s0_pairs)


def _ssm_matrices(lam_re, lam_im, log_dt, b_re, b_im, c_re, c_im, d_skip, n_steps):
    g, p, c = b_re.shape
    L = SSM_CHUNK
    dt = jnp.exp(log_dt)[:, None]
    zr, zi = lam_re * dt, lam_im * dt
    mag = jnp.exp(zr)
    ar, ai = mag * jnp.cos(zi), mag * jnp.sin(zi)
    den = lam_re * lam_re + lam_im * lam_im
    cr = ((ar - 1.0) * lam_re + ai * lam_im) / den
    ci = (ai * lam_re - (ar - 1.0) * lam_im) / den
    bbr = cr[..., None] * b_re - ci[..., None] * b_im
    bbi = cr[..., None] * b_im + ci[..., None] * b_re
    n = jnp.arange(L + 1, dtype=F32)[:, None, None]
    pm = jnp.exp(zr[None] * n)
    pwr, pwi = pm * jnp.cos(zi[None] * n), pm * jnp.sin(zi[None] * n)
    car = c_re[None] * pwr[:, :, None, :] - c_im[None] * pwi[:, :, None, :]
    cai = c_re[None] * pwi[:, :, None, :] + c_im[None] * pwr[:, :, None, :]
    kern = jnp.einsum('ngop,gpc->ngoc', car, bbr) - jnp.einsum('ngop,gpc->ngoc', cai, bbi)
    lag = jnp.arange(L)[None, :] - jnp.arange(L)[:, None]
    toe = jnp.where((lag >= 0)[:, :, None, None, None], kern[jnp.clip(lag, 0, L)], 0.0)
    eye = (lag == 0)[:, :, None, None, None] * (jnp.eye(c)[None, None, None] * d_skip[None, None, :, :, None])
    toe = toe + eye
    m_g = jnp.transpose(toe, (2, 0, 4, 1, 3)).reshape(g, L * c, L * c)
    rev = pwr[L - 1 - jnp.arange(L)], pwi[L - 1 - jnp.arange(L)]
    wxr = rev[0][..., None] * bbr[None] - rev[1][..., None] * bbi[None]
    wxi = rev[0][..., None] * bbi[None] + rev[1][..., None] * bbr[None]
    wxr = jnp.transpose(wxr, (1, 0, 3, 2)).reshape(g, L * c, p)
    wxi = jnp.transpose(wxi, (1, 0, 3, 2)).reshape(g, L * c, p)
    wcr = jnp.transpose(car[1:], (1, 3, 0, 2)).reshape(g, p, L * c)
    wci = -jnp.transpose(cai[1:], (1, 3, 0, 2)).reshape(g, p, L * c)

    def pair_blockdiag(a):
        a = a.reshape(g // 2, 2, a.shape[1], a.shape[2])
        z = jnp.zeros_like(a[:, 0])
        return jnp.concatenate([jnp.concatenate([a[:, 0], z], axis=2),
                                jnp.concatenate([z, a[:, 1]], axis=2)], axis=1)

    m_pair = pair_blockdiag(m_g)
    wx_pair = jnp.concatenate([pair_blockdiag(wxr), pair_blockdiag(wxi)], axis=2)
    wc_pair = jnp.concatenate([pair_blockdiag(wcr), pair_blockdiag(wci)], axis=1)
    qr, qi = [pwr[L]], [pwi[L]]
    for _ in range(n_steps - 1):
        qr, qi = qr + [qr[-1] * qr[-1] - qi[-1] * qi[-1]], qi + [2.0 * qr[-1] * qi[-1]]
    rows = -(-n_steps // SUBLANES) * SUBLANES
    padz = [jnp.zeros_like(qr[0])] * (rows - n_steps)
    pr = jnp.transpose(jnp.stack(qr + padz), (1, 0, 2)).reshape(g // 2, 2, rows, p)
    pi = jnp.transpose(jnp.stack(qi + padz), (1, 0, 2)).reshape(g // 2, 2, rows, p)
    pr = jnp.concatenate([pr[:, 0], pr[:, 1]], axis=2)
    pi = jnp.concatenate([pi[:, 0], pi[:, 1]], axis=2)
    return m_pair.astype(BF16), wx_pair.astype(BF16), wc_pair.astype(BF16), pr, pi


def _state_to_pairs(sr, si):
    nb, g, p = sr.shape
    f = lambda s: jnp.transpose(s.reshape(nb, g // 2, 2 * p), (1, 0, 2))
    return jnp.concatenate([f(sr), f(si)], axis=2)


def _state_from_pairs(s):
    npair, nb, w = s.shape
    f = lambda a: jnp.transpose(a, (1, 0, 2)).reshape(nb, npair * 2, w // 4)
    return f(s[:, :, :w // 2]), f(s[:, :, w // 2:])


def _glu_kernel(y_ref, w_ref, o_ref):
    y = jax.nn.gelu(y_ref[...])
    gate = jnp.dot(y.astype(BF16), w_ref[...], preferred_element_type=F32)
    o_ref[...] = y * jax.nn.sigmoid(gate)


def _glu(y, w, layer, tm):
    m, n = y.shape
    return pl.pallas_call(
        _glu_kernel,
        grid=(m // tm,),
        in_specs=[pl.BlockSpec((tm, n), lambda i: (i, 0)), _layer_spec(w, layer, (n, n), lambda i: (0, 0))],
        out_specs=pl.BlockSpec((tm, n), lambda i: (i, 0)),
        out_shape=jax.ShapeDtypeStruct((m, n), F32),
        compiler_params=_cparams(("arbitrary",)),
        name="glu",
    )(y, w)


def _merge_kernel(h_ref, a_ref, s_ref, m_ref, wga_ref, wgs_ref, wgm_ref, wa_ref, ws_ref, wm_ref, o_ref):
    h = h_ref[...]
    dot = lambda x, w: jnp.dot(x, w[...], preferred_element_type=F32)
    gate = lambda w: jax.nn.sigmoid(lax.dot_general(h, w[...], NT_DIMS, preferred_element_type=F32))
    ga, gs, gm = gate(wga_ref), gate(wgs_ref), gate(wgm_ref)
    merged = (ga * dot(a_ref[...].astype(BF16), wa_ref) + gs * dot(s_ref[...].astype(BF16), ws_ref)
              + gm * dot(m_ref[...].astype(BF16), wm_ref))
    o_ref[...] = merged.astype(BF16)


def _merge(h, a, s, mo, wg, wa, ws, wm, layer, tm, tn):
    m, d = h.shape
    nj = d // tn
    row = lambda w: pl.BlockSpec((tm, w), lambda i, j: (i, 0))
    gate = lambda off: _layer_spec(wg, layer, (tn, d), lambda i, j: (off + j, 0))
    lcol = lambda w: _layer_spec(w, layer, (w.shape[1], tn), lambda i, j: (0, j))
    return pl.pallas_call(
        _merge_kernel,
        grid=(m // tm, nj),
        in_specs=[row(d), row(a.shape[1]), row(s.shape[1]), row(mo.shape[1]),
                  gate(0), gate(nj), gate(2 * nj), lcol(wa), lcol(ws), lcol(wm)],
        out_specs=pl.BlockSpec((tm, tn), lambda i, j: (i, j)),
        out_shape=jax.ShapeDtypeStruct((m, d), BF16),
        compiler_params=_cparams(("arbitrary", "arbitrary")),
        name="merge",
    )(h, a, s, mo, wg, wg, wg, wa, ws, wm)


def _proj_res_kernel(x_ref, m_ref, w_ref, o_ref):
    o_ref[...] = x_ref[...] + jnp.dot(m_ref[...], w_ref[...], preferred_element_type=F32)


def _proj_res(x, merged, w, layer, tm):
    m, d = x.shape
    return pl.pallas_call(
        _proj_res_kernel,
        grid=(m // tm,),
        in_specs=[pl.BlockSpec((tm, d), lambda i: (i, 0)),
                  pl.BlockSpec((tm, d), lambda i: (i, 0)),
                  _layer_spec(w, layer, (d, d), lambda i: (0, 0))],
        out_specs=pl.BlockSpec((tm, d), lambda i: (i, 0)),
        out_shape=jax.ShapeDtypeStruct((m, d), F32),
        compiler_params=_cparams(("arbitrary",)),
        name="proj_res",
    )(x, merged, w)


def _ffn_kernel(x_ref, g_ref, w1_ref, w3_ref, w2_ref, o_ref, h_ref):
    f = pl.program_id(1)

    @pl.when(f == 0)
    def _():
        x = x_ref[...]
        r = lax.rsqrt(jnp.mean(x * x, axis=-1, keepdims=True) + EPS)
        h_ref[...] = ((x * r) * g_ref[...]).astype(BF16)
        o_ref[...] = x

    h = h_ref[...]
    a = jnp.dot(h, w1_ref[...], preferred_element_type=F32)
    b = jnp.dot(h, w3_ref[...], preferred_element_type=F32)
    act = (jax.nn.silu(a) * b).astype(BF16)
    o_ref[...] += jnp.dot(act, w2_ref[...], preferred_element_type=F32)


def _ffn(x, g, w1, w3, w2, layer, tm, tf):
    m, d = x.shape
    ff = w1.shape[2]
    return pl.pallas_call(
        _ffn_kernel,
        grid=(m // tm, ff // tf),
        in_specs=[pl.BlockSpec((tm, d), lambda i, f: (i, 0)),
                  _layer_spec(g, layer, (1, d), lambda i, f: (0, 0)),
                  _layer_spec(w1, layer, (d, tf), lambda i, f: (0, f)),
                  _layer_spec(w3, layer, (d, tf), lambda i, f: (0, f)),
                  _layer_spec(w2, layer, (tf, d), lambda i, f: (f, 0))],
        out_specs=pl.BlockSpec((tm, d), lambda i, f: (i, 0)),
        out_shape=jax.ShapeDtypeStruct((m, d), F32),
        scratch_shapes=[pltpu.VMEM((tm, d), BF16)],
        compiler_params=_cparams(("arbitrary", "arbitrary")),
        name="ffn",
    )(x, g, w1, w3, w2)


def kernel(x_prompt, x_sample, mem_prompt, cache_k, cache_v, cache_kidx, cache_mem_k, cache_mem_v, state_ssm_re, state_ssm_im, page_table, norm1_g, w_in, q_norm_g, k_norm_g, ssm_lam_re, ssm_lam_im, ssm_log_dt, ssm_b_re, ssm_b_im, ssm_c_re, ssm_c_im, ssm_d, w_glu, mem_norm_g, w_mem_kv, mq_norm_g, mk_norm_g, w_br_attn, w_br_ssm, w_br_mem, w_o, norm2_g, w_ff1, w_ff3, w_ff2):
    depth = w_in.shape[0]
    bp, tp, d = x_prompt.shape
    bs, ts, _ = x_sample.shape
    rp, rs = bp * tp, bs * ts
    m_tok = mem_prompt.shape[1]
    mw = MEM_HEADS * HEAD_DIM
    kvw = N_KV_HEADS * HEAD_DIM
    groups = ssm_lam_re.shape[1]
    tm_p, tm_s = _row_tile(rp), _row_tile(rs)

    x_p, x_s = x_prompt.reshape(rp, d), x_sample.reshape(rs, d)
    mem = mem_prompt.reshape(bp * m_tok, d)
    cmk = cache_mem_k.reshape(depth, bs, m_tok * MEM_HEADS, HEAD_DIM)
    cmv = cache_mem_v.reshape(depth, bs, m_tok * MEM_HEADS, HEAD_DIM)
    outs = [[] for _ in range(12)]
    nc_p, nc_s = tp // SSM_CHUNK, ts // SSM_CHUNK
    n_steps = max(1, int(math.log2(nc_p)))

    bf = lambda w: w.astype(BF16)
    w_mem_kv_b, w_glu_b, w_o_b = bf(w_mem_kv), bf(w_glu), bf(w_o)
    w_ba_b, w_bs_b, w_bm_b = bf(w_br_attn), bf(w_br_ssm), bf(w_br_mem)
    w_ff1_b, w_ff3_b, w_ff2_b = bf(w_ff1), bf(w_ff3), bf(w_ff2)
    mats = jax.vmap(functools.partial(_ssm_matrices, n_steps=n_steps))(
        ssm_lam_re, ssm_lam_im, ssm_log_dt, ssm_b_re, ssm_b_im, ssm_c_re, ssm_c_im, ssm_d)
    s0_prompt = jnp.zeros((depth, groups // 2, bp, SSM_PAIR_W), F32)
    s0_sample = jax.vmap(_state_to_pairs)(state_ssm_re, state_ssm_im)
    flags, gains = jax.vmap(_epilogue_rows)(q_norm_g, k_norm_g, mq_norm_g)
    mflag = jnp.concatenate([jnp.ones((mw,), F32), jnp.zeros((mw,), F32)])[None]
    mgains = jnp.concatenate([jnp.tile(mk_norm_g, (1, MEM_HEADS)), jnp.ones((depth, mw), F32)], axis=1)[:, None]
    g1, g2, gm = norm1_g[:, None], norm2_g[:, None], mem_norm_g[:, None]
    wp, wg, uw = _pack_w_in(w_in)

    def dense_tail(x, h, a, y, m, l, tm):
        s = _glu(y, w_glu_b, l, tm)
        merged = _merge(h, a, s, m, wg, w_ba_b, w_bs_b, w_bm_b, l, tm, min(512, d))
        x = _proj_res(x, merged, w_o_b, l, min(tm, 512))
        return _ffn(x, g2, w_ff1_b, w_ff3_b, w_ff2_b, l, tm, 512)

    for l in range(depth):
        z_p, h_p = _norm_matmul(x_p, g1[l], wp, flags[l], gains[l], layer=l, tm=tm_p, tn=1024, emit_h=True,
                                w_rows_out=True)
        z_s, h_s = _norm_matmul(x_s, g1[l], wp, flags[l], gains[l], layer=l, tm=tm_s, tn=1024, emit_h=True,
                                w_rows_out=True)

        mkv = _norm_matmul(mem, gm[l], w_mem_kv_b, mflag, mgains[l], layer=l, tm=bp * m_tok, tn=2 * mw,
                           emit_h=False, w_rows_out=False)
        mk_p = mkv[:, :mw].reshape(bp, m_tok, MEM_HEADS, HEAD_DIM)
        mv_p = mkv[:, mw:].reshape(bp, m_tok, MEM_HEADS, HEAD_DIM)

        a_p = _dsa_prompt(z_p, bp, tp)
        a_s = _dsa_sample(z_s, 0, bs, ts, l, cache_k, cache_v, cache_kidx, page_table)

        y_p, sf_p = _ssm(z_p, 0, rp, mats, s0_prompt, l, bp, nc_p)
        y_s, sf_s = _ssm(z_s, 0, rs, mats, s0_sample, l, bs, nc_s)

        m_p = _mem_attn(z_p, 0, bp, tp, 512, mk_p.reshape(1, bp, m_tok * MEM_HEADS, HEAD_DIM),
                        mv_p.reshape(1, bp, m_tok * MEM_HEADS, HEAD_DIM), 0)
        m_s = _mem_attn(z_s, 0, bs, ts, ts, cmk, cmv, l)

        x_p = dense_tail(x_p, h_p, a_p, y_p, m_p, l, tm_p)
        x_s = dense_tail(x_s, h_s, a_s, y_s, m_s, l, tm_s)

        srp, sip = _state_from_pairs(sf_p)
        srs, sis = _state_from_pairs(sf_s)
        new = [
            z_p[:, C_K:C_K + kvw].reshape(bp, tp, N_KV_HEADS, HEAD_DIM),
            z_p[:, C_V:C_V + kvw].reshape(bp, tp, N_KV_HEADS, HEAD_DIM),
            z_p[:, C_KI:C_KI + IDX_DIM].reshape(bp, tp, IDX_DIM),
            mk_p, mv_p, srp, sip,
            z_s[:, C_K:C_K + kvw].reshape(bs, ts, N_KV_HEADS, HEAD_DIM),
            z_s[:, C_V:C_V + kvw].reshape(bs, ts, N_KV_HEADS, HEAD_DIM),
            z_s[:, C_KI:C_KI + IDX_DIM].reshape(bs, ts, IDX_DIM),
            srs, sis,
        ]
        for o, v in zip(outs, new):
            o.append(v)

    st = lambda xs: jnp.stack(xs, axis=0)
    return (x_p.reshape(bp, tp, d), x_s.reshape(bs, ts, d), *[st(o) for o in outs])
```

```python
import functools
import math

import jax
import jax.numpy as jnp
from jax import lax
from jax.experimental import pallas as pl
from jax.experimental.pallas import tpu as pltpu

F32 = jnp.float32
BF16 = jnp.bfloat16
I32 = jnp.int32

LANES = 128
SUBLANES = 8
BF16_SUBLANES = 16
EPS = 1e-6
HEAD_DIM = 128
N_HEADS = 6
N_KV_HEADS = 2
KV_GROUP = N_HEADS // N_KV_HEADS
IDX_HEADS = 4
IDX_DIM = 64
TOPK_MAX = 256
SSM_GROUP = 16
SSM_STATE = 64
SSM_CHUNK = SUBLANES
SSM_PAIR_W = 2 * SSM_CHUNK * SSM_GROUP
PAIRS_PER_BLOCK = LANES // (2 * SSM_GROUP)
MEM_HEADS = 4
KEY_MIN = -2 ** 31
SEARCH_PASSES = 32
NEG_INF = float("-inf")
NEG_BIAS = -1e30
M_INIT = -1e29
VMEM_LIMIT = 56 * 1024 * 1024
MAX_ROW_TILE = 1024

C_Q, C_K, C_V, C_QI, C_MQ, C_KI, C_WI, C_U = 0, 768, 1024, 1280, 1536, 2048, 2112, 2176
Z_COLS = 3072


def _cparams(sem):
    return pltpu.CompilerParams(dimension_semantics=sem, vmem_limit_bytes=VMEM_LIMIT)


def _layer_spec(w, layer, block, index_map):
    assert w.ndim == len(block) + 1
    return pl.BlockSpec((None,) + tuple(block), lambda *idx: (layer,) + tuple(index_map(*idx)))


def _row_tile(m):
    return max(t for t in range(BF16_SUBLANES, MAX_ROW_TILE + 1, BF16_SUBLANES) if m % t == 0)


def _pack_w_in(w_in):
    depth, d, cols = w_in.shape
    aw, kvw, iqw, mw = N_HEADS * HEAD_DIM, N_KV_HEADS * HEAD_DIM, IDX_HEADS * IDX_DIM, MEM_HEADS * HEAD_DIM
    o_ki = aw + 2 * kvw + iqw
    o_u = o_ki + IDX_DIM + IDX_HEADS
    uw = cols - o_u - mw - 3 * d
    o_mq = o_u + uw
    o_g = o_mq + mw
    w_t = jnp.swapaxes(w_in, 1, 2)
    zeros = lambda n: jnp.zeros((depth, n, d), w_in.dtype)
    wp = jnp.concatenate([w_t[:, 0:o_ki], w_t[:, o_mq:o_g], w_t[:, o_ki:o_u], zeros(C_U - C_WI - IDX_HEADS),
                          w_t[:, o_u:o_mq], zeros(Z_COLS - C_U - uw)], axis=1)
    return wp.astype(BF16), w_t[:, o_g:].astype(BF16), uw


def _epilogue_rows(qn, kn, mqn):
    aw, kvw, iqw, mw = N_HEADS * HEAD_DIM, N_KV_HEADS * HEAD_DIM, IDX_HEADS * IDX_DIM, MEM_HEADS * HEAD_DIM
    ones = lambda n: jnp.ones((n,), F32)
    zer = lambda n: jnp.zeros((n,), F32)
    flag = jnp.concatenate([ones(aw + kvw), zer(kvw + iqw), ones(mw), zer(Z_COLS - C_KI)])[None]
    gain = jnp.concatenate([
        jnp.tile(qn, N_HEADS), jnp.tile(kn, N_KV_HEADS), ones(kvw + iqw), jnp.tile(mqn, MEM_HEADS),
        ones(IDX_DIM), jnp.full((IDX_HEADS,), IDX_HEADS ** -0.5, F32), ones(Z_COLS - C_WI - IDX_HEADS)])[None]
    return flag, gain


NT_DIMS = (((1,), (1,)), ((), ()))


def _norm_matmul_kernel(x_ref, g_ref, w_ref, flag_ref, gain_ref, z_ref, *rest, emit_h, w_rows_out):
    if emit_h:
        h_ref, hs_ref = rest
    else:
        (hs_ref,) = rest
    j = pl.program_id(1)

    @pl.when(j == 0)
    def _():
        x = x_ref[...]
        r = lax.rsqrt(jnp.mean(x * x, axis=-1, keepdims=True) + EPS)
        h = ((x * r) * g_ref[...]).astype(BF16)
        hs_ref[...] = h
        if emit_h:
            h_ref[...] = h

    if w_rows_out:
        acc = lax.dot_general(hs_ref[...], w_ref[...], NT_DIMS, preferred_element_type=F32)
    else:
        acc = jnp.dot(hs_ref[...], w_ref[...], preferred_element_type=F32)
    for c in range(acc.shape[1] // LANES):
        sl = slice(c * LANES, (c + 1) * LANES)
        y = acc[:, sl]
        r = lax.rsqrt(jnp.mean(y * y, axis=-1, keepdims=True) + EPS)
        scale = jnp.where(flag_ref[:, sl] > 0.0, r, 1.0) * gain_ref[:, sl]
        z_ref[:, sl] = y * scale


def _norm_matmul(x, g, w, flag, gain, *, layer, tm, tn, emit_h, w_rows_out):
    m, d = x.shape
    if w_rows_out:
        n = w.shape[1]
        w_spec = _layer_spec(w, layer, (tn, d), lambda i, j: (j, 0))
    else:
        n = w.shape[2]
        w_spec = _layer_spec(w, layer, (d, tn), lambda i, j: (0, j))
    out_shape = [jax.ShapeDtypeStruct((m, n), F32)]
    out_specs = [pl.BlockSpec((tm, tn), lambda i, j: (i, j))]
    if emit_h:
        out_shape.append(jax.ShapeDtypeStruct((m, d), BF16))
        out_specs.append(pl.BlockSpec((tm, d), lambda i, j: (i, 0)))
    res = pl.pallas_call(
        functools.partial(_norm_matmul_kernel, emit_h=emit_h, w_rows_out=w_rows_out),
        grid=(m // tm, n // tn),
        in_specs=[
            pl.BlockSpec((tm, d), lambda i, j: (i, 0)),
            pl.BlockSpec((1, d), lambda i, j: (0, 0)),
            w_spec,
            pl.BlockSpec((1, tn), lambda i, j: (0, j)),
            pl.BlockSpec((1, tn), lambda i, j: (0, j)),
        ],
        out_specs=out_specs,
        out_shape=out_shape,
        scratch_shapes=[pltpu.VMEM((tm, d), BF16)],
        compiler_params=_cparams(("arbitrary", "arbitrary")),
        name="norm_matmul",
    )(x, g, w, flag, gain)
    return res if emit_h else res[0]


def _ordered_to_f32(key):
    return pltpu.bitcast(jnp.where(key >= 0, key, key ^ 0x7FFFFFFF), F32)


def _count(score_ref, n_slab, slab_w, indicator):
    rows = score_ref.shape[0]

    def slab(s, acc):
        static = isinstance(s, int)
        base = s * slab_w if static else pl.multiple_of(s * slab_w, slab_w)
        parts = []
        for c in range(slab_w // LANES):
            off = base + c * LANES
            x = score_ref[:, off:off + LANES] if static else score_ref[:, pl.ds(off, LANES)]
            parts.append(indicator(x, off))
        while len(parts) > 1:
            parts = [a + b for a, b in zip(parts[0::2], parts[1::2])] + ([parts[-1]] if len(parts) % 2 else [])
        return acc + parts[0]

    acc = jnp.zeros((rows, LANES), F32)
    if isinstance(n_slab, int):
        for s in range(n_slab):
            acc = slab(s, acc)
    else:
        acc = lax.fori_loop(0, n_slab, slab, acc)
    return jnp.sum(acc, axis=1, keepdims=True)


def _kth_largest(score_ref, n_slab, slab_w, k):
    rows = score_ref.shape[0]
    count_ge = lambda cand: _count(score_ref, n_slab, slab_w, lambda x, off: jnp.where(x >= cand, 1.0, 0.0))
    lo = jnp.where(count_ge(jnp.zeros((rows, 1), F32)) >= k, 0, KEY_MIN)

    def body(it, lo):
        cand = lo | (1 << (30 - it))
        return jnp.where(count_ge(_ordered_to_f32(cand)) >= k, cand, lo)

    lo = lax.fori_loop(0, SEARCH_PASSES - 1, body, lo)
    return jnp.where(lo == KEY_MIN, NEG_INF, _ordered_to_f32(lo))


def _stack_heads(x, first, count, width):
    return jnp.concatenate([x[:, (first + r) * width:(first + r + 1) * width] for r in range(count)], axis=0)


def _indexer_scores(qi_stack, w_cols, ki_c, rows, keys_on_lanes=False):
    dims = (((1,), (0,)), ((), ())) if keys_on_lanes else (((1,), (1,)), ((), ()))
    logits = lax.dot_general(qi_stack, ki_c, dims, preferred_element_type=F32)
    score = jnp.maximum(logits[0:rows], 0.0) * w_cols[0]
    for h in range(1, IDX_HEADS):
        score = score + jnp.maximum(logits[h * rows:(h + 1) * rows], 0.0) * w_cols[h]
    return score


TQ = 128
KC = 512
KC_ATT = 1024
ROW_SLAB = KC
LOG2_E = math.log2(math.e)


def _count_keys(score_ref, n_slab, indicator):
    lanes = score_ref.shape[1]

    def slab(s, acc):
        base = pl.multiple_of(s * ROW_SLAB, ROW_SLAB)
        parts = [indicator(score_ref[pl.ds(base + r * SUBLANES, SUBLANES), :]) for r in range(ROW_SLAB // SUBLANES)]
        while len(parts) > 1:
            parts = [a + b for a, b in zip(parts[0::2], parts[1::2])] + ([parts[-1]] if len(parts) % 2 else [])
        return acc + parts[0]

    acc = lax.fori_loop(0, n_slab, slab, jnp.zeros((SUBLANES, lanes), F32))
    return jnp.sum(acc, axis=0, keepdims=True)


def _kth_largest_keys(score_ref, n_slab, k):
    lanes = score_ref.shape[1]
    count_ge = lambda cand: _count_keys(score_ref, n_slab, lambda x: jnp.where(x >= cand, 1.0, 0.0))
    lo = jnp.where(count_ge(jnp.zeros((1, lanes), F32)) >= k, 0, KEY_MIN)

    def body(it, lo):
        cand = lo | (1 << (30 - it))
        return jnp.where(count_ge(_ordered_to_f32(cand)) >= k, cand, lo)

    lo = lax.fori_loop(0, SEARCH_PASSES - 1, body, lo)
    return jnp.where(lo == KEY_MIN, NEG_INF, _ordered_to_f32(lo))


def _bias_keys(score_ref, bias_ref, n_slab, k, thr):
    lanes = score_ref.shape[1]
    need = k - _count_keys(score_ref, n_slab, lambda x: jnp.where(x > thr, 1.0, 0.0))
    tri = (lax.broadcasted_iota(I32, (LANES, LANES), 0)
           >= lax.broadcasted_iota(I32, (LANES, LANES), 1)).astype(BF16)

    def body(s, seen):
        base = pl.multiple_of(s * ROW_SLAB, ROW_SLAB)
        xs = [score_ref[pl.ds(base + j * LANES, LANES), :] for j in range(ROW_SLAB // LANES)]
        eqs = [jnp.where(x == thr, 1.0, 0.0) for x in xs]
        prefixes = [jnp.dot(tri, e.astype(BF16), preferred_element_type=F32) for e in eqs]
        for j, (x, eqf, prefix) in enumerate(zip(xs, eqs, prefixes)):
            keep = jnp.where(x > thr, 1.0, jnp.where(seen + prefix <= need, eqf, 0.0))
            bias_ref[pl.ds(base + j * LANES, LANES), :] = jnp.where(
                keep > 0.0, jnp.where(x > NEG_INF, 0.0, NEG_BIAS), NEG_BIAS)
            seen = seen + prefix[LANES - 1:LANES, :]
        return seen

    lax.fori_loop(0, n_slab, body, jnp.zeros((1, lanes), F32))


def _dsa_prompt_kernel(q_ref, qi_ref, kiwq_ref, k_ref, v_ref, kiw_ref, o_ref,
                       kbf_ref, vt_ref, kibf_ref, score_ref, bias_ref, *, topk, kc_att):
    i = pl.program_id(1)
    seq = k_ref.shape[0]

    @pl.when(i == 0)
    def _():
        kbf_ref[...] = k_ref[...].astype(BF16)
        kibf_ref[...] = kiw_ref[...].astype(BF16)
        for c in range(seq // KC):
            vt_ref[:, c * KC:(c + 1) * KC] = v_ref[c * KC:(c + 1) * KC, :].T.astype(BF16)

    n_kc = (i * TQ + TQ + KC - 1) // KC
    n_att = (n_kc * KC + kc_att - 1) // kc_att
    qi_stack = _stack_heads(qi_ref[...], 0, IDX_HEADS, IDX_DIM).astype(BF16)
    w_t = kiwq_ref[...].T
    scale_i = IDX_DIM ** -0.5
    w_rows = [w_t[IDX_DIM + h:IDX_DIM + h + 1, :] * scale_i for h in range(IDX_HEADS)]
    qpos = i * TQ + lax.broadcasted_iota(I32, (KC, TQ), 1)

    def score_body(c, _):
        off = pl.multiple_of(c * KC, KC)
        logits = lax.dot_general(kibf_ref[pl.ds(off, KC), 0:IDX_DIM], qi_stack, (((1,), (1,)), ((), ())),
                                 preferred_element_type=F32)
        score = jnp.maximum(logits[:, 0:TQ], 0.0) * w_rows[0]
        for h in range(1, IDX_HEADS):
            score = score + jnp.maximum(logits[:, h * TQ:(h + 1) * TQ], 0.0) * w_rows[h]
        kpos = off + lax.broadcasted_iota(I32, (KC, TQ), 0)
        score_ref[pl.ds(off, KC), :] = jnp.where(kpos > qpos, NEG_INF, score)
        return 0

    lax.fori_loop(0, n_kc, score_body, 0)
    thr = _kth_largest_keys(score_ref, n_kc, topk)
    _bias_keys(score_ref, bias_ref, n_kc, topk, thr)

    @pl.when(n_att * kc_att > n_kc * KC)
    def _():
        bias_ref[pl.ds(pl.multiple_of(n_kc * KC, KC), KC), :] = jnp.full((KC, TQ), NEG_BIAS, F32)

    q = q_ref[...] * (HEAD_DIM ** -0.5 * LOG2_E)
    q_gs = [_stack_heads(q, g * KV_GROUP, KV_GROUP, HEAD_DIM).astype(BF16) for g in range(N_KV_HEADS)]
    gq = KV_GROUP * TQ

    def att_body(c, carries):
        off = pl.multiple_of(c * kc_att, kc_att)
        b = bias_ref[pl.ds(off, kc_att), :]
        bias = jnp.concatenate([b] * KV_GROUP, axis=1)
        out = []
        for g in range(N_KV_HEADS):
            gs = slice(g * HEAD_DIM, (g + 1) * HEAD_DIM)
            m, l, acc = carries[g]
            s = lax.dot_general(kbf_ref[pl.ds(off, kc_att), gs], q_gs[g], NT_DIMS,
                                preferred_element_type=F32) + bias
            m_new = jnp.maximum(m, jnp.max(s, axis=0, keepdims=True))
            alpha = jnp.exp2(m - m_new)
            p = jnp.exp2(s - m_new)
            l = alpha * l + jnp.sum(p, axis=0, keepdims=True)
            acc = alpha * acc + jnp.dot(vt_ref[gs, pl.ds(off, kc_att)], p.astype(BF16),
                                        preferred_element_type=F32)
            out.append((m_new, l, acc))
        return tuple(out)

    init = (jnp.full((1, gq), M_INIT, F32), jnp.zeros((1, gq), F32), jnp.zeros((HEAD_DIM, gq), F32))
    carries = lax.fori_loop(0, n_att, att_body, (init,) * N_KV_HEADS)
    for g, (_, l, acc) in enumerate(carries):
        o_t = acc / l
        for r in range(KV_GROUP):
            hh = g * KV_GROUP + r
            o_ref[:, hh * HEAD_DIM:(hh + 1) * HEAD_DIM] = o_t[:, r * TQ:(r + 1) * TQ].T


def _dsa_prompt(z, batch, seq):
    nq = seq // TQ
    topk = min(TOPK_MAX, seq // 4)
    aw, kvw, iqw = N_HEADS * HEAD_DIM, N_KV_HEADS * HEAD_DIM, IDX_HEADS * IDX_DIM
    return pl.pallas_call(
        functools.partial(_dsa_prompt_kernel, topk=topk, kc_att=min(KC_ATT, seq)),
        grid=(batch, nq),
        in_specs=[
            pl.BlockSpec((TQ, aw), lambda b, i: (b * nq + i, C_Q // aw)),
            pl.BlockSpec((TQ, iqw), lambda b, i: (b * nq + i, C_QI // iqw)),
            pl.BlockSpec((TQ, LANES), lambda b, i: (b * nq + i, C_KI // LANES)),
            pl.BlockSpec((seq, kvw), lambda b, i: (b, C_K // kvw)),
            pl.BlockSpec((seq, kvw), lambda b, i: (b, C_V // kvw)),
            pl.BlockSpec((seq, LANES), lambda b, i: (b, C_KI // LANES)),
        ],
        out_specs=pl.BlockSpec((TQ, aw), lambda b, i: (b * nq + i, 0)),
        out_shape=jax.ShapeDtypeStruct((batch * seq, aw), F32),
        scratch_shapes=[
            pltpu.VMEM((seq, kvw), BF16),
            pltpu.VMEM((kvw, seq), BF16),
            pltpu.VMEM((seq, LANES), BF16),
            pltpu.VMEM((seq, TQ), F32),
            pltpu.VMEM((seq, TQ), F32),
        ],
        compiler_params=_cparams(("arbitrary", "arbitrary")),
        name="dsa_prompt",
    )(z, z, z, z, z, z)


KC_S = 1024


def _tie_cut(score_ref, n128, k, thr):
    rows = score_ref.shape[0]
    width = n128 * LANES
    need = k - _count(score_ref, 1, width, lambda x, off: jnp.where(x > thr, 1.0, 0.0))
    ties = _count(score_ref, 1, width, lambda x, off: jnp.where(x == thr, 1.0, 0.0))
    lane = lax.broadcasted_iota(I32, (rows, LANES), 1)
    everything = jnp.full((rows, 1), width - 1, I32)

    def bisect():
        def ties_upto(j):
            return _count(score_ref, 1, width,
                          lambda x, off: jnp.where(x == thr, jnp.where(lane + off <= j, 1.0, 0.0), 0.0))

        def body(_, c):
            lo, hi = c
            mid = (lo + hi) >> 1
            ok = ties_upto(mid) >= need
            return jnp.where(ok, lo, mid), jnp.where(ok, mid, hi)

        steps = int(math.ceil(math.log2(width + 1)))
        return lax.fori_loop(0, steps, body, (jnp.full((rows, 1), -1, I32), everything))[1]

    return lax.cond(jnp.max(ties - need) > 0.0, bisect, lambda: everything)


def _write_bias(score_ref, bias_ref, n128, thr, cut):
    rows = score_ref.shape[0]
    lane = lax.broadcasted_iota(I32, (rows, LANES), 1)
    for c in range(n128):
        x = score_ref[:, c * LANES:(c + 1) * LANES]
        tie = jnp.where(x == thr, jnp.where(lane + c * LANES <= cut, 1.0, 0.0), 0.0)
        keep = jnp.where(x > thr, 1.0, tie)
        bias_ref[:, c * LANES:(c + 1) * LANES] = jnp.where(keep > 0.0, jnp.where(x > NEG_INF, 0.0, NEG_BIAS),
                                                           NEG_BIAS)


def _lane_fold(x, op):
    parts = [x[:, c * LANES:(c + 1) * LANES] for c in range(x.shape[1] // LANES)]
    while len(parts) > 1:
        parts = [op(a, b) for a, b in zip(parts[0::2], parts[1::2])] + ([parts[-1]] if len(parts) % 2 else [])
    return parts[0]


def _dsa_sample_kernel(pt_ref, q_ref, qi_ref, kiw_ref, kn_ref, vn_ref, ck_ref, cv_ref, cki_ref, o_ref,
                       kbuf, vbuf, kibuf, score_ref, bias_ref, s_ref, sem, *, layer, n_pages, page, tq, topk):
    b = pl.program_id(0)
    nb = pl.num_programs(0)
    past = n_pages * page
    slot = b % 2
    streams = ((ck_ref, kbuf, page * N_KV_HEADS), (cv_ref, vbuf, page * N_KV_HEADS), (cki_ref, kibuf, IDX_DIM))

    def page_copy(which, bb, sl, p):
        src, dst, rpp = streams[which]
        return pltpu.make_async_copy(src.at[layer, pt_ref[bb * n_pages + p]],
                                     dst.at[sl, pl.ds(p * rpp, rpp)], sem.at[which, sl])

    def start_fetch(bb, sl):
        def body(p, _):
            for which in range(len(streams)):
                page_copy(which, bb, sl, p).start()
            return 0
        lax.fori_loop(0, n_pages, body, 0)

    def wait_fetch(sl, which):
        _, dst, _ = streams[which]
        pltpu.make_async_copy(dst.at[sl], dst.at[sl], sem.at[which, sl]).wait()

    @pl.when(b == 0)
    def _():
        start_fetch(0, 0)

    @pl.when(b + 1 < nb)
    def _():
        start_fetch(b + 1, 1 - slot)

    qi_stack = _stack_heads(qi_ref[...], 0, IDX_HEADS, IDX_DIM).astype(BF16)
    kiw = kiw_ref[...]
    scale_i = IDX_DIM ** -0.5
    w_cols = [kiw[:, IDX_DIM + h:IDX_DIM + h + 1] * scale_i for h in range(IDX_HEADS)]
    n128 = past // LANES + 1
    ppc = KC // page

    wait_fetch(slot, 2)
    for c in range(past // KC):
        ki_t = jnp.concatenate([kibuf[slot, (c * ppc + j) * IDX_DIM:(c * ppc + j + 1) * IDX_DIM, :]
                                for j in range(ppc)], axis=1).astype(BF16)
        score_ref[:, c * KC:(c + 1) * KC] = _indexer_scores(qi_stack, w_cols, ki_t, tq, keys_on_lanes=True)
    pad = jnp.zeros((LANES - tq, LANES), F32)
    ki_new = jnp.concatenate([kiw, pad], axis=0)[:, 0:IDX_DIM].astype(BF16)
    score_new = _indexer_scores(qi_stack, w_cols, ki_new, tq)
    newer = lax.broadcasted_iota(I32, (tq, LANES), 1) > lax.broadcasted_iota(I32, (tq, LANES), 0)
    score_ref[:, past:past + LANES] = jnp.where(newer, NEG_INF, score_new)

    thr = _kth_largest(score_ref, 1, n128 * LANES, topk)
    _write_bias(score_ref, bias_ref, n128, thr, _tie_cut(score_ref, n128, topk, thr))

    wait_fetch(slot, 0)
    wait_fetch(slot, 1)
    q = q_ref[...] * (HEAD_DIM ** -0.5)
    q_gs = [_stack_heads(q, g * KV_GROUP, KV_GROUP, HEAD_DIM).astype(BF16) for g in range(N_KV_HEADS)]
    kv_pad = jnp.zeros((LANES - tq, N_KV_HEADS * HEAD_DIM), F32)
    k_new = jnp.concatenate([kn_ref[...], kv_pad], axis=0).astype(BF16)
    v_new = jnp.concatenate([vn_ref[...], kv_pad], axis=0).astype(BF16)
    chunks = [(c * KC_S, KC_S) for c in range(past // KC_S)] + [(past, LANES)]

    def keys_of(buf, new, g, start, size):
        if start == past:
            return new[:, g * HEAD_DIM:(g + 1) * HEAD_DIM]
        return buf[slot, pl.ds(start * N_KV_HEADS + g, size, stride=N_KV_HEADS), :].astype(BF16)

    m_acc = [jnp.full((KV_GROUP * tq, LANES), M_INIT, F32) for _ in range(N_KV_HEADS)]
    for start, size in chunks:
        bias = jnp.concatenate([bias_ref[:, start:start + size]] * KV_GROUP, axis=0)
        for g in range(N_KV_HEADS):
            s = lax.dot_general(q_gs[g], keys_of(kbuf, k_new, g, start, size), NT_DIMS,
                                preferred_element_type=F32) + bias
            s_ref[g, :, start:start + size] = s
            m_acc[g] = jnp.maximum(m_acc[g], _lane_fold(s, jnp.maximum))
    m = [jnp.max(a, axis=1, keepdims=True) for a in m_acc]
    l_acc = [jnp.zeros((KV_GROUP * tq, LANES), F32) for _ in range(N_KV_HEADS)]
    acc = [jnp.zeros((KV_GROUP * tq, HEAD_DIM), F32) for _ in range(N_KV_HEADS)]
    for start, size in chunks:
        for g in range(N_KV_HEADS):
            p = jnp.exp(s_ref[g, :, start:start + size] - m[g])
            l_acc[g] = l_acc[g] + _lane_fold(p, jnp.add)
            acc[g] = acc[g] + jnp.dot(p.astype(BF16), keys_of(vbuf, v_new, g, start, size),
                                      preferred_element_type=F32)
    for g in range(N_KV_HEADS):
        o = acc[g] / jnp.sum(l_acc[g], axis=1, keepdims=True)
        for r in range(KV_GROUP):
            hh = g * KV_GROUP + r
            o_ref[:, hh * HEAD_DIM:(hh + 1) * HEAD_DIM] = o[r * tq:(r + 1) * tq]


def _dsa_sample(z, row0, nb, tq, layer, cache_k, cache_v, cache_kidx, page_table):
    depth, n_pool, page = cache_k.shape[:3]
    n_pages = page_table.shape[1]
    past = n_pages * page
    topk = min(TOPK_MAX, (past + tq) // 4)
    aw, kvw, iqw = N_HEADS * HEAD_DIM, N_KV_HEADS * HEAD_DIM, IDX_HEADS * IDX_DIM
    rb = row0 // tq
    grid_spec = pltpu.PrefetchScalarGridSpec(
        num_scalar_prefetch=1,
        grid=(nb,),
        in_specs=[
            pl.BlockSpec((tq, aw), lambda b, pt: (rb + b, C_Q // aw)),
            pl.BlockSpec((tq, iqw), lambda b, pt: (rb + b, C_QI // iqw)),
            pl.BlockSpec((tq, LANES), lambda b, pt: (rb + b, C_KI // LANES)),
            pl.BlockSpec((tq, kvw), lambda b, pt: (rb + b, C_K // kvw)),
            pl.BlockSpec((tq, kvw), lambda b, pt: (rb + b, C_V // kvw)),
            pl.BlockSpec(memory_space=pl.ANY),
            pl.BlockSpec(memory_space=pl.ANY),
            pl.BlockSpec(memory_space=pl.ANY),
        ],
        out_specs=pl.BlockSpec((tq, aw), lambda b, pt: (b, 0)),
        scratch_shapes=[
            pltpu.VMEM((2, past * N_KV_HEADS, HEAD_DIM), F32),
            pltpu.VMEM((2, past * N_KV_HEADS, HEAD_DIM), F32),
            pltpu.VMEM((2, n_pages * IDX_DIM, page), F32),
            pltpu.VMEM((tq, past + LANES), F32),
            pltpu.VMEM((tq, past + LANES), F32),
            pltpu.VMEM((N_KV_HEADS, KV_GROUP * tq, past + LANES), F32),
            pltpu.SemaphoreType.DMA((3, 2)),
        ],
    )
    ck = cache_k.reshape(depth, n_pool, page * N_KV_HEADS, HEAD_DIM)
    cv = cache_v.reshape(depth, n_pool, page * N_KV_HEADS, HEAD_DIM)
    return pl.pallas_call(
        functools.partial(_dsa_sample_kernel, layer=layer, n_pages=n_pages, page=page, tq=tq, topk=topk),
        grid_spec=grid_spec,
        out_shape=jax.ShapeDtypeStruct((nb * tq, aw), F32),
        compiler_params=_cparams(("arbitrary",)),
        name="dsa_sample",
    )(page_table.reshape(-1), z, z, z, z, z, ck, cv, jnp.swapaxes(cache_kidx, 2, 3))


def _mem_attn_kernel(q_ref, k_ref, v_ref, o_ref):
    q = q_ref[...] * (HEAD_DIM ** -0.5)
    m_tok = k_ref.shape[0] // MEM_HEADS
    for h in range(MEM_HEADS):
        hs = slice(h * HEAD_DIM, (h + 1) * HEAD_DIM)
        rows = pl.ds(h, m_tok, stride=MEM_HEADS)
        s = lax.dot_general(q[:, hs].astype(BF16), k_ref[rows, :].astype(BF16),
                            (((1,), (1,)), ((), ())), preferred_element_type=F32)
        m = jnp.max(s, axis=1, keepdims=True)
        p = jnp.exp(s - m)
        l = jnp.sum(p, axis=1, keepdims=True)
        o = jnp.dot(p.astype(BF16), v_ref[rows, :].astype(BF16), preferred_element_type=F32)
        o_ref[:, hs] = o / l


def _mem_attn(z, row0, nb, t, tq, mk, mv, layer):
    mw = MEM_HEADS * HEAD_DIM
    nq = t // tq
    rb = row0 // tq
    kv_spec = pl.BlockSpec((None, None) + mk.shape[2:], lambda b, i: (layer, b, 0, 0))
    return pl.pallas_call(
        _mem_attn_kernel,
        grid=(nb, nq),
        in_specs=[pl.BlockSpec((tq, mw), lambda b, i: (rb + b * nq + i, C_MQ // mw)), kv_spec, kv_spec],
        out_specs=pl.BlockSpec((tq, mw), lambda b, i: (b * nq + i, 0)),
        out_shape=jax.ShapeDtypeStruct((nb * t, mw), F32),
        compiler_params=_cparams(("arbitrary", "arbitrary")),
        name="mem_attn",
    )(z, mk, mv)


def _shift_rows(x, sh):
    rows = lax.broadcasted_iota(I32, x.shape, 0)
    return jnp.where(rows >= sh, pltpu.roll(x, sh, axis=0), 0.0)


def _ssm_pair(u, m_w, wx_w, wc_w, pr, pi, s0, nb, nc):
    half = SSM_PAIR_W // 2
    y_in = jnp.dot(u, m_w, preferred_element_type=F32)
    x = jnp.dot(u, wx_w, preferred_element_type=F32)
    xr, xi = x[:, :half], x[:, half:]
    s0r, s0i = s0[:, :half], s0[:, half:]
    ar, ai = pr[0:1], pi[0:1]
    if nc == 1:
        sr = xr + ar * s0r - ai * s0i
        si = xi + ar * s0i + ai * s0r
        pvr, pvi = s0r, s0i
        sfin = jnp.concatenate([sr, si], axis=1)
    else:
        prev_r, prev_i, fin = [], [], []
        first = lax.broadcasted_iota(I32, (nc, half), 0) == 0
        for b in range(nb):
            rs = slice(b * nc, (b + 1) * nc)
            b0r, b0i = s0r[b:b + 1], s0i[b:b + 1]
            sr = xr[rs] + jnp.where(first, ar * b0r - ai * b0i, 0.0)
            si = xi[rs] + jnp.where(first, ar * b0i + ai * b0r, 0.0)
            for mstep in range(int(math.log2(nc))):
                sh = 2 ** mstep
                qr, qi = pr[mstep:mstep + 1], pi[mstep:mstep + 1]
                tr, ti = _shift_rows(sr, sh), _shift_rows(si, sh)
                sr, si = sr + qr * tr - qi * ti, si + qr * ti + qi * tr
            fin.append(jnp.concatenate([sr[nc - 1:nc], si[nc - 1:nc]], axis=1))
            prev_r.append(jnp.where(first, b0r, pltpu.roll(sr, 1, axis=0)))
            prev_i.append(jnp.where(first, b0i, pltpu.roll(si, 1, axis=0)))
        pvr = jnp.concatenate(prev_r, axis=0)
        pvi = jnp.concatenate(prev_i, axis=0)
        sfin = jnp.concatenate(fin, axis=0)
    prev = jnp.concatenate([pvr, pvi], axis=1).astype(BF16)
    return y_in + jnp.dot(prev, wc_w, preferred_element_type=F32), sfin


def _ssm_kernel(u_ref, m_ref, wx_ref, wc_ref, pr_ref, pi_ref, s0_ref, y_ref, sf_ref, *, nb, nc):
    nch = nb * nc
    gw = SSM_GROUP
    tok = [u_ref[pl.ds(i, nch, stride=SSM_CHUNK), :] for i in range(SSM_CHUNK)]
    ys = []
    for q in range(PAIRS_PER_BLOCK):
        base = q * 2 * gw
        u = jnp.concatenate([tok[i][:, base + gg * gw:base + (gg + 1) * gw]
                             for gg in range(2) for i in range(SSM_CHUNK)], axis=1).astype(BF16)
        y, sfin = _ssm_pair(u, m_ref[q], wx_ref[q], wc_ref[q], pr_ref[q], pi_ref[q], s0_ref[q], nb, nc)
        sf_ref[q] = sfin
        ys.append(y)
    for i in range(SSM_CHUNK):
        y_ref[pl.ds(i, nch, stride=SSM_CHUNK), :] = jnp.concatenate(
            [ys[q][:, (gg * SSM_CHUNK + i) * gw:(gg * SSM_CHUNK + i + 1) * gw]
             for q in range(PAIRS_PER_BLOCK) for gg in range(2)], axis=1)


def _ssm(z, row0, rows, mats, s0_pairs, layer, nb, nc):
    m_mat, wx, wc, pr, pi = mats
    _, npair, _, w = m_mat.shape
    pw = pr.shape[2]
    ppb = PAIRS_PER_BLOCK
    blk3 = lambda r, c: pl.BlockSpec((ppb, r, c), lambda j: (j, 0, 0))
    lay3 = lambda a, r, c: _layer_spec(a, layer, (ppb, r, c), lambda j: (j, 0, 0))
    return pl.pallas_call(
        functools.partial(_ssm_kernel, nb=nb, nc=nc),
        grid=(npair // ppb,),
        in_specs=[pl.BlockSpec((rows, LANES), lambda j: (row0 // rows, C_U // LANES + j)),
                  lay3(m_mat, w, w), lay3(wx, w, w), lay3(wc, w, w), lay3(pr, pw, LANES), lay3(pi, pw, LANES),
                  lay3(s0_pairs, nb, w)],
        out_specs=[pl.BlockSpec((rows, LANES), lambda j: (0, j)), blk3(nb, w)],
        out_shape=[jax.ShapeDtypeStruct((rows, npair * 2 * SSM_GROUP), F32),
                   jax.ShapeDtypeStruct((npair, nb, w), F32)],
        compiler_params=_cparams(("arbitrary",)),
        name="ssm",
    )(z, m_mat, wx, wc, pr, pi, s0_pairs)


def _ssm_matrices(lam_re, lam_im, log_dt, b_re, b_im, c_re, c_im, d_skip, n_steps):
    g, p, c = b_re.shape
    L = SSM_CHUNK
    dt = jnp.exp(log_dt)[:, None]
    zr, zi = lam_re * dt, lam_im * dt
    mag = jnp.exp(zr)
    ar, ai = mag * jnp.cos(zi), mag * jnp.sin(zi)
    den = lam_re * lam_re + lam_im * lam_im
    cr = ((ar - 1.0) * lam_re + ai * lam_im) / den
    ci = (ai * lam_re - (ar - 1.0) * lam_im) / den
    bbr = cr[..., None] * b_re - ci[..., None] * b_im
    bbi = cr[..., None] * b_im + ci[..., None] * b_re
    n = jnp.arange(L + 1, dtype=F32)[:, None, None]
    pm = jnp.exp(zr[None] * n)
    pwr, pwi = pm * jnp.cos(zi[None] * n), pm * jnp.sin(zi[None] * n)
    car = c_re[None] * pwr[:, :, None, :] - c_im[None] * pwi[:, :, None, :]
    cai = c_re[None] * pwi[:, :, None, :] + c_im[None] * pwr[:, :, None, :]
    kern = jnp.einsum('ngop,gpc->ngoc', car[:L], bbr) - jnp.einsum('ngop,gpc->ngoc', cai[:L], bbi)
    skip = (jnp.eye(c)[None] * d_skip[:, :, None])[None]
    kern = jnp.concatenate([kern[:1] + skip, kern[1:]], axis=0)
    lag = jnp.arange(L)[None, :] - jnp.arange(L)[:, None]
    shift = (lag[None] == jnp.arange(L)[:, None, None]).astype(F32)
    m_g = jnp.einsum('ngoc,nji->gjcio', kern, shift).reshape(g, L * c, L * c)
    rev = pwr[L - 1 - jnp.arange(L)], pwi[L - 1 - jnp.arange(L)]
    wxr = rev[0][..., None] * bbr[None] - rev[1][..., None] * bbi[None]
    wxi = rev[0][..., None] * bbi[None] + rev[1][..., None] * bbr[None]
    wxr = jnp.transpose(wxr, (1, 0, 3, 2)).reshape(g, L * c, p)
    wxi = jnp.transpose(wxi, (1, 0, 3, 2)).reshape(g, L * c, p)
    wcr = jnp.transpose(car[1:], (1, 3, 0, 2)).reshape(g, p, L * c)
    wci = -jnp.transpose(cai[1:], (1, 3, 0, 2)).reshape(g, p, L * c)

    def pair_blockdiag(a):
        a = a.reshape(g // 2, 2, a.shape[1], a.shape[2])
        z = jnp.zeros_like(a[:, 0])
        return jnp.concatenate([jnp.concatenate([a[:, 0], z], axis=2),
                                jnp.concatenate([z, a[:, 1]], axis=2)], axis=1)

    m_pair = pair_blockdiag(m_g)
    wx_pair = jnp.concatenate([pair_blockdiag(wxr), pair_blockdiag(wxi)], axis=2)
    wc_pair = jnp.concatenate([pair_blockdiag(wcr), pair_blockdiag(wci)], axis=1)
    qr, qi = [pwr[L]], [pwi[L]]
    for _ in range(n_steps - 1):
        qr, qi = qr + [qr[-1] * qr[-1] - qi[-1] * qi[-1]], qi + [2.0 * qr[-1] * qi[-1]]
    rows = -(-n_steps // SUBLANES) * SUBLANES
    padz = [jnp.zeros_like(qr[0])] * (rows - n_steps)
    pr = jnp.transpose(jnp.stack(qr + padz), (1, 0, 2)).reshape(g // 2, 2, rows, p)
    pi = jnp.transpose(jnp.stack(qi + padz), (1, 0, 2)).reshape(g // 2, 2, rows, p)
    pr = jnp.concatenate([pr[:, 0], pr[:, 1]], axis=2)
    pi = jnp.concatenate([pi[:, 0], pi[:, 1]], axis=2)
    return m_pair.astype(BF16), wx_pair.astype(BF16), wc_pair.astype(BF16), pr, pi


def _state_to_pairs(sr, si):
    nb, g, p = sr.shape
    f = lambda s: jnp.transpose(s.reshape(nb, g // 2, 2 * p), (1, 0, 2))
    return jnp.concatenate([f(sr), f(si)], axis=2)


def _state_from_pairs(s):
    npair, nb, w = s.shape
    f = lambda a: jnp.transpose(a, (1, 0, 2)).reshape(nb, npair * 2, w // 4)
    return f(s[:, :, :w // 2]), f(s[:, :, w // 2:])


def _glu_kernel(y_ref, w_ref, o_ref):
    y = jax.nn.gelu(y_ref[...])
    gate = jnp.dot(y.astype(BF16), w_ref[...], preferred_element_type=F32)
    o_ref[...] = y * jax.nn.sigmoid(gate)


def _glu(y, w, layer, tm):
    m, n = y.shape
    return pl.pallas_call(
        _glu_kernel,
        grid=(m // tm,),
        in_specs=[pl.BlockSpec((tm, n), lambda i: (i, 0)), _layer_spec(w, layer, (n, n), lambda i: (0, 0))],
        out_specs=pl.BlockSpec((tm, n), lambda i: (i, 0)),
        out_shape=jax.ShapeDtypeStruct((m, n), F32),
        compiler_params=_cparams(("arbitrary",)),
        name="glu",
    )(y, w)


def _merge_kernel(h_ref, a_ref, s_ref, m_ref, wga_ref, wgs_ref, wgm_ref, wa_ref, ws_ref, wm_ref, o_ref):
    h = h_ref[...]
    dot = lambda x, w: jnp.dot(x, w[...], preferred_element_type=F32)
    gate = lambda w: jax.nn.sigmoid(lax.dot_general(h, w[...], NT_DIMS, preferred_element_type=F32))
    ga, gs, gm = gate(wga_ref), gate(wgs_ref), gate(wgm_ref)
    merged = (ga * dot(a_ref[...].astype(BF16), wa_ref) + gs * dot(s_ref[...].astype(BF16), ws_ref)
              + gm * dot(m_ref[...].astype(BF16), wm_ref))
    o_ref[...] = merged.astype(BF16)


def _merge(h, a, s, mo, wg, wa, ws, wm, layer, tm, tn):
    m, d = h.shape
    nj = d // tn
    row = lambda w: pl.BlockSpec((tm, w), lambda i, j: (i, 0))
    gate = lambda off: _layer_spec(wg, layer, (tn, d), lambda i, j: (off + j, 0))
    lcol = lambda w: _layer_spec(w, layer, (w.shape[1], tn), lambda i, j: (0, j))
    return pl.pallas_call(
        _merge_kernel,
        grid=(m // tm, nj),
        in_specs=[row(d), row(a.shape[1]), row(s.shape[1]), row(mo.shape[1]),
                  gate(0), gate(nj), gate(2 * nj), lcol(wa), lcol(ws), lcol(wm)],
        out_specs=pl.BlockSpec((tm, tn), lambda i, j: (i, j)),
        out_shape=jax.ShapeDtypeStruct((m, d), BF16),
        compiler_params=_cparams(("arbitrary", "arbitrary")),
        name="merge",
    )(h, a, s, mo, wg, wg, wg, wa, ws, wm)


def _proj_res_kernel(x_ref, m_ref, w_ref, o_ref):
    o_ref[...] = x_ref[...] + jnp.dot(m_ref[...], w_ref[...], preferred_element_type=F32)


def _proj_res(x, merged, w, layer, tm):
    m, d = x.shape
    return pl.pallas_call(
        _proj_res_kernel,
        grid=(m // tm,),
        in_specs=[pl.BlockSpec((tm, d), lambda i: (i, 0)),
                  pl.BlockSpec((tm, d), lambda i: (i, 0)),
                  _layer_spec(w, layer, (d, d), lambda i: (0, 0))],
        out_specs=pl.BlockSpec((tm, d), lambda i: (i, 0)),
        out_shape=jax.ShapeDtypeStruct((m, d), F32),
        compiler_params=_cparams(("arbitrary",)),
        name="proj_res",
    )(x, merged, w)


def _ffn_kernel(x_ref, g_ref, w1_ref, w3_ref, w2_ref, o_ref, h_ref):
    f = pl.program_id(1)

    @pl.when(f == 0)
    def _():
        x = x_ref[...]
        r = lax.rsqrt(jnp.mean(x * x, axis=-1, keepdims=True) + EPS)
        h_ref[...] = ((x * r) * g_ref[...]).astype(BF16)
        o_ref[...] = x

    h = h_ref[...]
    a = jnp.dot(h, w1_ref[...], preferred_element_type=F32)
    b = jnp.dot(h, w3_ref[...], preferred_element_type=F32)
    act = (jax.nn.silu(a) * b).astype(BF16)
    o_ref[...] += jnp.dot(act, w2_ref[...], preferred_element_type=F32)


def _ffn(x, g, w1, w3, w2, layer, tm, tf):
    m, d = x.shape
    ff = w1.shape[2]
    return pl.pallas_call(
        _ffn_kernel,
        grid=(m // tm, ff // tf),
        in_specs=[pl.BlockSpec((tm, d), lambda i, f: (i, 0)),
                  _layer_spec(g, layer, (1, d), lambda i, f: (0, 0)),
                  _layer_spec(w1, layer, (d, tf), lambda i, f: (0, f)),
                  _layer_spec(w3, layer, (d, tf), lambda i, f: (0, f)),
                  _layer_spec(w2, layer, (tf, d), lambda i, f: (f, 0))],
        out_specs=pl.BlockSpec((tm, d), lambda i, f: (i, 0)),
        out_shape=jax.ShapeDtypeStruct((m, d), F32),
        scratch_shapes=[pltpu.VMEM((tm, d), BF16)],
        compiler_params=_cparams(("arbitrary", "arbitrary")),
        name="ffn",
    )(x, g, w1, w3, w2)


def kernel(x_prompt, x_sample, mem_prompt, cache_k, cache_v, cache_kidx, cache_mem_k, cache_mem_v, state_ssm_re, state_ssm_im, page_table, norm1_g, w_in, q_norm_g, k_norm_g, ssm_lam_re, ssm_lam_im, ssm_log_dt, ssm_b_re, ssm_b_im, ssm_c_re, ssm_c_im, ssm_d, w_glu, mem_norm_g, w_mem_kv, mq_norm_g, mk_norm_g, w_br_attn, w_br_ssm, w_br_mem, w_o, norm2_g, w_ff1, w_ff3, w_ff2):
    depth = w_in.shape[0]
    bp, tp, d = x_prompt.shape
    bs, ts, _ = x_sample.shape
    rp, rs = bp * tp, bs * ts
    m_tok = mem_prompt.shape[1]
    mw = MEM_HEADS * HEAD_DIM
    kvw = N_KV_HEADS * HEAD_DIM
    groups = ssm_lam_re.shape[1]
    tm_p, tm_s = _row_tile(rp), _row_tile(rs)

    x_p, x_s = x_prompt.reshape(rp, d), x_sample.reshape(rs, d)
    mem = mem_prompt.reshape(bp * m_tok, d)
    cmk = cache_mem_k.reshape(depth, bs, m_tok * MEM_HEADS, HEAD_DIM)
    cmv = cache_mem_v.reshape(depth, bs, m_tok * MEM_HEADS, HEAD_DIM)
    outs = [[] for _ in range(12)]
    nc_p, nc_s = tp // SSM_CHUNK, ts // SSM_CHUNK
    n_steps = max(1, int(math.log2(nc_p)))

    bf = lambda w: w.astype(BF16)
    w_mem_kv_b, w_glu_b, w_o_b = bf(w_mem_kv), bf(w_glu), bf(w_o)
    w_ba_b, w_bs_b, w_bm_b = bf(w_br_attn), bf(w_br_ssm), bf(w_br_mem)
    w_ff1_b, w_ff3_b, w_ff2_b = bf(w_ff1), bf(w_ff3), bf(w_ff2)
    mats = jax.vmap(functools.partial(_ssm_matrices, n_steps=n_steps))(
        ssm_lam_re, ssm_lam_im, ssm_log_dt, ssm_b_re, ssm_b_im, ssm_c_re, ssm_c_im, ssm_d)
    s0_prompt = jnp.zeros((depth, groups // 2, bp, SSM_PAIR_W), F32)
    s0_sample = jax.vmap(_state_to_pairs)(state_ssm_re, state_ssm_im)
    flags, gains = jax.vmap(_epilogue_rows)(q_norm_g, k_norm_g, mq_norm_g)
    mflag = jnp.concatenate([jnp.ones((mw,), F32), jnp.zeros((mw,), F32)])[None]
    mgains = jnp.concatenate([jnp.tile(mk_norm_g, (1, MEM_HEADS)), jnp.ones((depth, mw), F32)], axis=1)[:, None]
    g1, g2, gm = norm1_g[:, None], norm2_g[:, None], mem_norm_g[:, None]
    wp, wg, uw = _pack_w_in(w_in)

    def dense_tail(x, h, a, y, m, l, tm):
        s = _glu(y, w_glu_b, l, tm)
        merged = _merge(h, a, s, m, wg, w_ba_b, w_bs_b, w_bm_b, l, tm, min(512, d))
        x = _proj_res(x, merged, w_o_b, l, min(tm, 512))
        return _ffn(x, g2, w_ff1_b, w_ff3_b, w_ff2_b, l, tm, 512)

    for l in range(depth):
        z_p, h_p = _norm_matmul(x_p, g1[l], wp, flags[l], gains[l], layer=l, tm=tm_p, tn=1024, emit_h=True,
                                w_rows_out=True)
        z_s, h_s = _norm_matmul(x_s, g1[l], wp, flags[l], gains[l], layer=l, tm=tm_s, tn=1024, emit_h=True,
                                w_rows_out=True)

        mkv = _norm_matmul(mem, gm[l], w_mem_kv_b, mflag, mgains[l], layer=l, tm=bp * m_tok, tn=2 * mw,
                           emit_h=False, w_rows_out=False)
        mk_p = mkv[:, :mw].reshape(bp, m_tok, MEM_HEADS, HEAD_DIM)
        mv_p = mkv[:, mw:].reshape(bp, m_tok, MEM_HEADS, HEAD_DIM)

        a_p = _dsa_prompt(z_p, bp, tp)
        a_s = _dsa_sample(z_s, 0, bs, ts, l, cache_k, cache_v, cache_kidx, page_table)

        y_p, sf_p = _ssm(z_p, 0, rp, mats, s0_prompt, l, bp, nc_p)
        y_s, sf_s = _ssm(z_s, 0, rs, mats, s0_sample, l, bs, nc_s)

        m_p = _mem_attn(z_p, 0, bp, tp, 512, mk_p.reshape(1, bp, m_tok * MEM_HEADS, HEAD_DIM),
                        mv_p.reshape(1, bp, m_tok * MEM_HEADS, HEAD_DIM), 0)
        m_s = _mem_attn(z_s, 0, bs, ts, ts, cmk, cmv, l)

        x_p = dense_tail(x_p, h_p, a_p, y_p, m_p, l, tm_p)
        x_s = dense_tail(x_s, h_s, a_s, y_s, m_s, l, tm_s)

        srp, sip = _state_from_pairs(sf_p)
        srs, sis = _state_from_pairs(sf_s)
        new = [
            z_p[:, C_K:C_K + kvw].reshape(bp, tp, N_KV_HEADS, HEAD_DIM),
            z_p[:, C_V:C_V + kvw].reshape(bp, tp, N_KV_HEADS, HEAD_DIM),
            z_p[:, C_KI:C_KI + IDX_DIM].reshape(bp, tp, IDX_DIM),
            mk_p, mv_p, srp, sip,
            z_s[:, C_K:C_K + kvw].reshape(bs, ts, N_KV_HEADS, HEAD_DIM),
            z_s[:, C_V:C_V + kvw].reshape(bs, ts, N_KV_HEADS, HEAD_DIM),
            z_s[:, C_KI:C_KI + IDX_DIM].reshape(bs, ts, IDX_DIM),
            srs, sis,
        ]
        for o, v in zip(outs, new):
            o.append(v)

    st = lambda xs: jnp.stack(xs, axis=0)
    return (x_p.reshape(bp, tp, d), x_s.reshape(bs, ts, d), *[st(o) for o in outs])
```

```python
import functools
import math

import jax
import jax.numpy as jnp
from jax import lax
from jax.experimental import pallas as pl
from jax.experimental.pallas import tpu as pltpu

F32 = jnp.float32
BF16 = jnp.bfloat16
I32 = jnp.int32

LANES = 128
SUBLANES = 8
BF16_SUBLANES = 16
EPS = 1e-6
HEAD_DIM = 128
N_HEADS = 6
N_KV_HEADS = 2
KV_GROUP = N_HEADS // N_KV_HEADS
IDX_HEADS = 4
IDX_DIM = 64
TOPK_MAX = 256
SSM_GROUP = 16
SSM_STATE = 64
SSM_CHUNK = SUBLANES
SSM_PAIR_W = 2 * SSM_CHUNK * SSM_GROUP
PAIRS_PER_BLOCK = LANES // (2 * SSM_GROUP)
MEM_HEADS = 4
KEY_MIN = -2 ** 31
SEARCH_PASSES = 32
NEG_INF = float("-inf")
NEG_BIAS = -1e30
M_INIT = -1e29
VMEM_LIMIT = 56 * 1024 * 1024
MAX_ROW_TILE = 1024

C_Q, C_K, C_V, C_QI, C_MQ, C_KI, C_WI, C_U = 0, 768, 1024, 1280, 1536, 2048, 2112, 2176
Z_COLS = 3072


def _cparams(sem):
    return pltpu.CompilerParams(dimension_semantics=sem, vmem_limit_bytes=VMEM_LIMIT)


def _layer_spec(w, layer, block, index_map):
    assert w.ndim == len(block) + 1
    return pl.BlockSpec((None,) + tuple(block), lambda *idx: (layer,) + tuple(index_map(*idx)))


def _row_tile(m):
    return max(t for t in range(BF16_SUBLANES, MAX_ROW_TILE + 1, BF16_SUBLANES) if m % t == 0)


def _pack_w_in(w_in):
    depth, d, cols = w_in.shape
    aw, kvw, iqw, mw = N_HEADS * HEAD_DIM, N_KV_HEADS * HEAD_DIM, IDX_HEADS * IDX_DIM, MEM_HEADS * HEAD_DIM
    o_ki = aw + 2 * kvw + iqw
    o_u = o_ki + IDX_DIM + IDX_HEADS
    uw = cols - o_u - mw - 3 * d
    o_mq = o_u + uw
    o_g = o_mq + mw
    head = jnp.swapaxes(w_in[:, :, :o_u], 1, 2).astype(BF16)
    tail = jnp.swapaxes(w_in[:, :, o_u:], 1, 2).astype(BF16)
    zeros = lambda n: jnp.zeros((depth, n, d), BF16)
    wp = jnp.concatenate([head[:, 0:o_ki], tail[:, uw:uw + mw], head[:, o_ki:o_u], zeros(C_U - C_WI - IDX_HEADS),
                          tail[:, 0:uw], zeros(Z_COLS - C_U - uw)], axis=1)
    return wp, tail[:, uw + mw:], uw


def _epilogue_rows(qn, kn, mqn):
    aw, kvw, iqw, mw = N_HEADS * HEAD_DIM, N_KV_HEADS * HEAD_DIM, IDX_HEADS * IDX_DIM, MEM_HEADS * HEAD_DIM
    ones = lambda n: jnp.ones((n,), F32)
    zer = lambda n: jnp.zeros((n,), F32)
    flag = jnp.concatenate([ones(aw + kvw), zer(kvw + iqw), ones(mw), zer(Z_COLS - C_KI)])[None]
    gain = jnp.concatenate([
        jnp.tile(qn, N_HEADS), jnp.tile(kn, N_KV_HEADS), ones(kvw + iqw), jnp.tile(mqn, MEM_HEADS),
        ones(IDX_DIM), jnp.full((IDX_HEADS,), IDX_HEADS ** -0.5, F32), ones(Z_COLS - C_WI - IDX_HEADS)])[None]
    return flag, gain


NT_DIMS = (((1,), (1,)), ((), ()))


def _norm_matmul_kernel(x_ref, g_ref, w_ref, flag_ref, gain_ref, z_ref, *rest, emit_h, w_rows_out):
    if emit_h:
        h_ref, hs_ref = rest
    else:
        (hs_ref,) = rest
    j = pl.program_id(1)

    @pl.when(j == 0)
    def _():
        x = x_ref[...]
        r = lax.rsqrt(jnp.mean(x * x, axis=-1, keepdims=True) + EPS)
        h = ((x * r) * g_ref[...]).astype(BF16)
        hs_ref[...] = h
        if emit_h:
            h_ref[...] = h

    if w_rows_out:
        acc = lax.dot_general(hs_ref[...], w_ref[...], NT_DIMS, preferred_element_type=F32)
    else:
        acc = jnp.dot(hs_ref[...], w_ref[...], preferred_element_type=F32)
    for c in range(acc.shape[1] // LANES):
        sl = slice(c * LANES, (c + 1) * LANES)
        y = acc[:, sl]
        r = lax.rsqrt(jnp.mean(y * y, axis=-1, keepdims=True) + EPS)
        scale = jnp.where(flag_ref[:, sl] > 0.0, r, 1.0) * gain_ref[:, sl]
        z_ref[:, sl] = y * scale


def _norm_matmul(x, g, w, flag, gain, *, layer, tm, tn, emit_h, w_rows_out):
    m, d = x.shape
    if w_rows_out:
        n = w.shape[1]
        w_spec = _layer_spec(w, layer, (tn, d), lambda i, j: (j, 0))
    else:
        n = w.shape[2]
        w_spec = _layer_spec(w, layer, (d, tn), lambda i, j: (0, j))
    out_shape = [jax.ShapeDtypeStruct((m, n), F32)]
    out_specs = [pl.BlockSpec((tm, tn), lambda i, j: (i, j))]
    if emit_h:
        out_shape.append(jax.ShapeDtypeStruct((m, d), BF16))
        out_specs.append(pl.BlockSpec((tm, d), lambda i, j: (i, 0)))
    res = pl.pallas_call(
        functools.partial(_norm_matmul_kernel, emit_h=emit_h, w_rows_out=w_rows_out),
        grid=(m // tm, n // tn),
        in_specs=[
            pl.BlockSpec((tm, d), lambda i, j: (i, 0)),
            pl.BlockSpec((1, d), lambda i, j: (0, 0)),
            w_spec,
            pl.BlockSpec((1, tn), lambda i, j: (0, j)),
            pl.BlockSpec((1, tn), lambda i, j: (0, j)),
        ],
        out_specs=out_specs,
        out_shape=out_shape,
        scratch_shapes=[pltpu.VMEM((tm, d), BF16)],
        compiler_params=_cparams(("arbitrary", "arbitrary")),
        name="norm_matmul",
    )(x, g, w, flag, gain)
    return res if emit_h else res[0]


def _ordered_to_f32(key):
    return pltpu.bitcast(jnp.where(key >= 0, key, key ^ 0x7FFFFFFF), F32)


def _count(score_ref, n_slab, slab_w, indicator):
    rows = score_ref.shape[0]

    def slab(s, acc):
        static = isinstance(s, int)
        base = s * slab_w if static else pl.multiple_of(s * slab_w, slab_w)
        parts = []
        for c in range(slab_w // LANES):
            off = base + c * LANES
            x = score_ref[:, off:off + LANES] if static else score_ref[:, pl.ds(off, LANES)]
            parts.append(indicator(x, off))
        while len(parts) > 1:
            parts = [a + b for a, b in zip(parts[0::2], parts[1::2])] + ([parts[-1]] if len(parts) % 2 else [])
        return acc + parts[0]

    acc = jnp.zeros((rows, LANES), F32)
    if isinstance(n_slab, int):
        for s in range(n_slab):
            acc = slab(s, acc)
    else:
        acc = lax.fori_loop(0, n_slab, slab, acc)
    return jnp.sum(acc, axis=1, keepdims=True)


def _kth_largest(score_ref, n_slab, slab_w, k):
    rows = score_ref.shape[0]
    count_ge = lambda cand: _count(score_ref, n_slab, slab_w, lambda x, off: jnp.where(x >= cand, 1.0, 0.0))
    lo = jnp.where(count_ge(jnp.zeros((rows, 1), F32)) >= k, 0, KEY_MIN)

    def body(it, lo):
        cand = lo | (1 << (30 - it))
        return jnp.where(count_ge(_ordered_to_f32(cand)) >= k, cand, lo)

    lo = lax.fori_loop(0, SEARCH_PASSES - 1, body, lo)
    return jnp.where(lo == KEY_MIN, NEG_INF, _ordered_to_f32(lo))


def _stack_heads(x, first, count, width):
    return jnp.concatenate([x[:, (first + r) * width:(first + r + 1) * width] for r in range(count)], axis=0)


def _indexer_scores(qi_stack, w_cols, ki_c, rows, keys_on_lanes=False):
    dims = (((1,), (0,)), ((), ())) if keys_on_lanes else (((1,), (1,)), ((), ()))
    logits = lax.dot_general(qi_stack, ki_c, dims, preferred_element_type=F32)
    score = jnp.maximum(logits[0:rows], 0.0) * w_cols[0]
    for h in range(1, IDX_HEADS):
        score = score + jnp.maximum(logits[h * rows:(h + 1) * rows], 0.0) * w_cols[h]
    return score


TQ = 128
KC = 512
KC_ATT = 1024
ROW_SLAB = KC
LOG2_E = math.log2(math.e)


def _count_keys(score_ref, n_slab, indicator):
    lanes = score_ref.shape[1]

    def slab(s, acc):
        base = pl.multiple_of(s * ROW_SLAB, ROW_SLAB)
        parts = [indicator(score_ref[pl.ds(base + r * SUBLANES, SUBLANES), :]) for r in range(ROW_SLAB // SUBLANES)]
        while len(parts) > 1:
            parts = [a + b for a, b in zip(parts[0::2], parts[1::2])] + ([parts[-1]] if len(parts) % 2 else [])
        return acc + parts[0]

    acc = lax.fori_loop(0, n_slab, slab, jnp.zeros((SUBLANES, lanes), F32))
    return jnp.sum(acc, axis=0, keepdims=True)


def _kth_largest_keys(score_ref, n_slab, k):
    lanes = score_ref.shape[1]
    count_ge = lambda cand: _count_keys(score_ref, n_slab, lambda x: jnp.where(x >= cand, 1.0, 0.0))
    lo = jnp.where(count_ge(jnp.zeros((1, lanes), F32)) >= k, 0, KEY_MIN)

    def body(it, lo):
        cand = lo | (1 << (30 - it))
        return jnp.where(count_ge(_ordered_to_f32(cand)) >= k, cand, lo)

    lo = lax.fori_loop(0, SEARCH_PASSES - 1, body, lo)
    return jnp.where(lo == KEY_MIN, NEG_INF, _ordered_to_f32(lo))


def _bias_keys(score_ref, bias_ref, n_slab, k, thr):
    lanes = score_ref.shape[1]
    need = k - _count_keys(score_ref, n_slab, lambda x: jnp.where(x > thr, 1.0, 0.0))
    tri = (lax.broadcasted_iota(I32, (LANES, LANES), 0)
           >= lax.broadcasted_iota(I32, (LANES, LANES), 1)).astype(BF16)

    def body(s, seen):
        base = pl.multiple_of(s * ROW_SLAB, ROW_SLAB)
        xs = [score_ref[pl.ds(base + j * LANES, LANES), :] for j in range(ROW_SLAB // LANES)]
        eqs = [jnp.where(x == thr, 1.0, 0.0) for x in xs]
        prefixes = [jnp.dot(tri, e.astype(BF16), preferred_element_type=F32) for e in eqs]
        for j, (x, eqf, prefix) in enumerate(zip(xs, eqs, prefixes)):
            keep = jnp.where(x > thr, 1.0, jnp.where(seen + prefix <= need, eqf, 0.0))
            bias_ref[pl.ds(base + j * LANES, LANES), :] = jnp.where(
                keep > 0.0, jnp.where(x > NEG_INF, 0.0, NEG_BIAS), NEG_BIAS)
            seen = seen + prefix[LANES - 1:LANES, :]
        return seen

    lax.fori_loop(0, n_slab, body, jnp.zeros((1, lanes), F32))


def _dsa_prompt_kernel(q_ref, qi_ref, kiwq_ref, k_ref, v_ref, kiw_ref, o_ref,
                       kbf_ref, vt_ref, kibf_ref, score_ref, bias_ref, *, topk, kc_att):
    i = pl.program_id(1)
    seq = k_ref.shape[0]

    @pl.when(i == 0)
    def _():
        kbf_ref[...] = k_ref[...].astype(BF16)
        kibf_ref[...] = kiw_ref[...].astype(BF16)
        for c in range(seq // KC):
            vt_ref[:, c * KC:(c + 1) * KC] = v_ref[c * KC:(c + 1) * KC, :].T.astype(BF16)

    n_kc = (i * TQ + TQ + KC - 1) // KC
    n_att = (n_kc * KC + kc_att - 1) // kc_att
    qi_stack = _stack_heads(qi_ref[...], 0, IDX_HEADS, IDX_DIM).astype(BF16)
    w_t = kiwq_ref[...].T
    scale_i = IDX_DIM ** -0.5
    w_rows = [w_t[IDX_DIM + h:IDX_DIM + h + 1, :] * scale_i for h in range(IDX_HEADS)]
    qpos = i * TQ + lax.broadcasted_iota(I32, (KC, TQ), 1)

    def score_body(c, _):
        off = pl.multiple_of(c * KC, KC)
        logits = lax.dot_general(kibf_ref[pl.ds(off, KC), 0:IDX_DIM], qi_stack, (((1,), (1,)), ((), ())),
                                 preferred_element_type=F32)
        score = jnp.maximum(logits[:, 0:TQ], 0.0) * w_rows[0]
        for h in range(1, IDX_HEADS):
            score = score + jnp.maximum(logits[:, h * TQ:(h + 1) * TQ], 0.0) * w_rows[h]
        kpos = off + lax.broadcasted_iota(I32, (KC, TQ), 0)
        score_ref[pl.ds(off, KC), :] = jnp.where(kpos > qpos, NEG_INF, score)
        return 0

    lax.fori_loop(0, n_kc, score_body, 0)
    thr = _kth_largest_keys(score_ref, n_kc, topk)
    _bias_keys(score_ref, bias_ref, n_kc, topk, thr)

    @pl.when(n_att * kc_att > n_kc * KC)
    def _():
        bias_ref[pl.ds(pl.multiple_of(n_kc * KC, KC), KC), :] = jnp.full((KC, TQ), NEG_BIAS, F32)

    q = q_ref[...] * (HEAD_DIM ** -0.5 * LOG2_E)
    q_gs = [_stack_heads(q, g * KV_GROUP, KV_GROUP, HEAD_DIM).astype(BF16) for g in range(N_KV_HEADS)]
    gq = KV_GROUP * TQ

    def att_body(c, carries):
        off = pl.multiple_of(c * kc_att, kc_att)
        b = bias_ref[pl.ds(off, kc_att), :]
        bias = jnp.concatenate([b] * KV_GROUP, axis=1)
        out = []
        for g in range(N_KV_HEADS):
            gs = slice(g * HEAD_DIM, (g + 1) * HEAD_DIM)
            m, l, acc = carries[g]
            s = lax.dot_general(kbf_ref[pl.ds(off, kc_att), gs], q_gs[g], NT_DIMS,
                                preferred_element_type=F32) + bias
            m_new = jnp.maximum(m, jnp.max(s, axis=0, keepdims=True))
            alpha = jnp.exp2(m - m_new)
            p = jnp.exp2(s - m_new)
            l = alpha * l + jnp.sum(p, axis=0, keepdims=True)
            acc = alpha * acc + jnp.dot(vt_ref[gs, pl.ds(off, kc_att)], p.astype(BF16),
                                        preferred_element_type=F32)
            out.append((m_new, l, acc))
        return tuple(out)

    init = (jnp.full((1, gq), M_INIT, F32), jnp.zeros((1, gq), F32), jnp.zeros((HEAD_DIM, gq), F32))
    carries = lax.fori_loop(0, n_att, att_body, (init,) * N_KV_HEADS)
    for g, (_, l, acc) in enumerate(carries):
        o_t = acc / l
        for r in range(KV_GROUP):
            hh = g * KV_GROUP + r
            o_ref[:, hh * HEAD_DIM:(hh + 1) * HEAD_DIM] = o_t[:, r * TQ:(r + 1) * TQ].T


def _dsa_prompt(z, batch, seq):
    nq = seq // TQ
    topk = min(TOPK_MAX, seq // 4)
    aw, kvw, iqw = N_HEADS * HEAD_DIM, N_KV_HEADS * HEAD_DIM, IDX_HEADS * IDX_DIM
    return pl.pallas_call(
        functools.partial(_dsa_prompt_kernel, topk=topk, kc_att=min(KC_ATT, seq)),
        grid=(batch, nq),
        in_specs=[
            pl.BlockSpec((TQ, aw), lambda b, i: (b * nq + i, C_Q // aw)),
            pl.BlockSpec((TQ, iqw), lambda b, i: (b * nq + i, C_QI // iqw)),
            pl.BlockSpec((TQ, LANES), lambda b, i: (b * nq + i, C_KI // LANES)),
            pl.BlockSpec((seq, kvw), lambda b, i: (b, C_K // kvw)),
            pl.BlockSpec((seq, kvw), lambda b, i: (b, C_V // kvw)),
            pl.BlockSpec((seq, LANES), lambda b, i: (b, C_KI // LANES)),
        ],
        out_specs=pl.BlockSpec((TQ, aw), lambda b, i: (b * nq + i, 0)),
        out_shape=jax.ShapeDtypeStruct((batch * seq, aw), F32),
        scratch_shapes=[
            pltpu.VMEM((seq, kvw), BF16),
            pltpu.VMEM((kvw, seq), BF16),
            pltpu.VMEM((seq, LANES), BF16),
            pltpu.VMEM((seq, TQ), F32),
            pltpu.VMEM((seq, TQ), F32),
        ],
        compiler_params=_cparams(("arbitrary", "arbitrary")),
        name="dsa_prompt",
    )(z, z, z, z, z, z)


KC_S = 1024


def _tie_cut(score_ref, n128, k, thr):
    rows = score_ref.shape[0]
    width = n128 * LANES
    need = k - _count(score_ref, 1, width, lambda x, off: jnp.where(x > thr, 1.0, 0.0))
    ties = _count(score_ref, 1, width, lambda x, off: jnp.where(x == thr, 1.0, 0.0))
    lane = lax.broadcasted_iota(I32, (rows, LANES), 1)
    everything = jnp.full((rows, 1), width - 1, I32)

    def bisect():
        def ties_upto(j):
            return _count(score_ref, 1, width,
                          lambda x, off: jnp.where(x == thr, jnp.where(lane + off <= j, 1.0, 0.0), 0.0))

        def body(_, c):
            lo, hi = c
            mid = (lo + hi) >> 1
            ok = ties_upto(mid) >= need
            return jnp.where(ok, lo, mid), jnp.where(ok, mid, hi)

        steps = int(math.ceil(math.log2(width + 1)))
        return lax.fori_loop(0, steps, body, (jnp.full((rows, 1), -1, I32), everything))[1]

    return lax.cond(jnp.max(ties - need) > 0.0, bisect, lambda: everything)


def _write_bias(score_ref, bias_ref, n128, thr, cut):
    rows = score_ref.shape[0]
    lane = lax.broadcasted_iota(I32, (rows, LANES), 1)
    for c in range(n128):
        x = score_ref[:, c * LANES:(c + 1) * LANES]
        tie = jnp.where(x == thr, jnp.where(lane + c * LANES <= cut, 1.0, 0.0), 0.0)
        keep = jnp.where(x > thr, 1.0, tie)
        bias_ref[:, c * LANES:(c + 1) * LANES] = jnp.where(keep > 0.0, jnp.where(x > NEG_INF, 0.0, NEG_BIAS),
                                                           NEG_BIAS)


def _lane_fold(x, op):
    parts = [x[:, c * LANES:(c + 1) * LANES] for c in range(x.shape[1] // LANES)]
    while len(parts) > 1:
        parts = [op(a, b) for a, b in zip(parts[0::2], parts[1::2])] + ([parts[-1]] if len(parts) % 2 else [])
    return parts[0]


def _dsa_sample_kernel(pt_ref, q_ref, qi_ref, kiw_ref, kn_ref, vn_ref, ck_ref, cv_ref, cki_ref, o_ref,
                       kbuf, vbuf, kibuf, score_ref, bias_ref, s_ref, sem, *, layer, n_pages, page, tq, topk):
    b = pl.program_id(0)
    nb = pl.num_programs(0)
    past = n_pages * page
    slot = b % 2
    streams = ((ck_ref, kbuf, page * N_KV_HEADS), (cv_ref, vbuf, page * N_KV_HEADS), (cki_ref, kibuf, IDX_DIM))

    def page_copy(which, bb, sl, p):
        src, dst, rpp = streams[which]
        return pltpu.make_async_copy(src.at[layer, pt_ref[bb * n_pages + p]],
                                     dst.at[sl, pl.ds(p * rpp, rpp)], sem.at[which, sl])

    def start_fetch(bb, sl):
        def body(p, _):
            for which in range(len(streams)):
                page_copy(which, bb, sl, p).start()
            return 0
        lax.fori_loop(0, n_pages, body, 0)

    def wait_fetch(sl, which):
        _, dst, _ = streams[which]
        pltpu.make_async_copy(dst.at[sl], dst.at[sl], sem.at[which, sl]).wait()

    @pl.when(b == 0)
    def _():
        start_fetch(0, 0)

    @pl.when(b + 1 < nb)
    def _():
        start_fetch(b + 1, 1 - slot)

    qi_stack = _stack_heads(qi_ref[...], 0, IDX_HEADS, IDX_DIM).astype(BF16)
    kiw = kiw_ref[...]
    scale_i = IDX_DIM ** -0.5
    w_cols = [kiw[:, IDX_DIM + h:IDX_DIM + h + 1] * scale_i for h in range(IDX_HEADS)]
    n128 = past // LANES + 1
    ppc = KC // page

    wait_fetch(slot, 2)
    for c in range(past // KC):
        ki_t = jnp.concatenate([kibuf[slot, (c * ppc + j) * IDX_DIM:(c * ppc + j + 1) * IDX_DIM, :]
                                for j in range(ppc)], axis=1).astype(BF16)
        score_ref[:, c * KC:(c + 1) * KC] = _indexer_scores(qi_stack, w_cols, ki_t, tq, keys_on_lanes=True)
    pad = jnp.zeros((LANES - tq, LANES), F32)
    ki_new = jnp.concatenate([kiw, pad], axis=0)[:, 0:IDX_DIM].astype(BF16)
    score_new = _indexer_scores(qi_stack, w_cols, ki_new, tq)
    newer = lax.broadcasted_iota(I32, (tq, LANES), 1) > lax.broadcasted_iota(I32, (tq, LANES), 0)
    score_ref[:, past:past + LANES] = jnp.where(newer, NEG_INF, score_new)

    thr = _kth_largest(score_ref, 1, n128 * LANES, topk)
    _write_bias(score_ref, bias_ref, n128, thr, _tie_cut(score_ref, n128, topk, thr))

    wait_fetch(slot, 0)
    wait_fetch(slot, 1)
    q = q_ref[...] * (HEAD_DIM ** -0.5)
    q_gs = [_stack_heads(q, g * KV_GROUP, KV_GROUP, HEAD_DIM).astype(BF16) for g in range(N_KV_HEADS)]
    kv_pad = jnp.zeros((LANES - tq, N_KV_HEADS * HEAD_DIM), F32)
    k_new = jnp.concatenate([kn_ref[...], kv_pad], axis=0).astype(BF16)
    v_new = jnp.concatenate([vn_ref[...], kv_pad], axis=0).astype(BF16)
    chunks = [(c * KC_S, KC_S) for c in range(past // KC_S)] + [(past, LANES)]

    def keys_of(buf, new, g, start, size):
        if start == past:
            return new[:, g * HEAD_DIM:(g + 1) * HEAD_DIM]
        return buf[slot, pl.ds(start * N_KV_HEADS + g, size, stride=N_KV_HEADS), :].astype(BF16)

    m_acc = [jnp.full((KV_GROUP * tq, LANES), M_INIT, F32) for _ in range(N_KV_HEADS)]
    for start, size in chunks:
        bias = jnp.concatenate([bias_ref[:, start:start + size]] * KV_GROUP, axis=0)
        for g in range(N_KV_HEADS):
            s = lax.dot_general(q_gs[g], keys_of(kbuf, k_new, g, start, size), NT_DIMS,
                                preferred_element_type=F32) + bias
            s_ref[g, :, start:start + size] = s
            m_acc[g] = jnp.maximum(m_acc[g], _lane_fold(s, jnp.maximum))
    m = [jnp.max(a, axis=1, keepdims=True) for a in m_acc]
    l_acc = [jnp.zeros((KV_GROUP * tq, LANES), F32) for _ in range(N_KV_HEADS)]
    acc = [jnp.zeros((KV_GROUP * tq, HEAD_DIM), F32) for _ in range(N_KV_HEADS)]
    for start, size in chunks:
        for g in range(N_KV_HEADS):
            p = jnp.exp(s_ref[g, :, start:start + size] - m[g])
            l_acc[g] = l_acc[g] + _lane_fold(p, jnp.add)
            acc[g] = acc[g] + jnp.dot(p.astype(BF16), keys_of(vbuf, v_new, g, start, size),
                                      preferred_element_type=F32)
    for g in range(N_KV_HEADS):
        o = acc[g] / jnp.sum(l_acc[g], axis=1, keepdims=True)
        for r in range(KV_GROUP):
            hh = g * KV_GROUP + r
            o_ref[:, hh * HEAD_DIM:(hh + 1) * HEAD_DIM] = o[r * tq:(r + 1) * tq]


def _dsa_sample(z, row0, nb, tq, layer, cache_k, cache_v, cache_kidx, page_table):
    depth, n_pool, page = cache_k.shape[:3]
    n_pages = page_table.shape[1]
    past = n_pages * page
    topk = min(TOPK_MAX, (past + tq) // 4)
    aw, kvw, iqw = N_HEADS * HEAD_DIM, N_KV_HEADS * HEAD_DIM, IDX_HEADS * IDX_DIM
    rb = row0 // tq
    grid_spec = pltpu.PrefetchScalarGridSpec(
        num_scalar_prefetch=1,
        grid=(nb,),
        in_specs=[
            pl.BlockSpec((tq, aw), lambda b, pt: (rb + b, C_Q // aw)),
            pl.BlockSpec((tq, iqw), lambda b, pt: (rb + b, C_QI // iqw)),
            pl.BlockSpec((tq, LANES), lambda b, pt: (rb + b, C_KI // LANES)),
            pl.BlockSpec((tq, kvw), lambda b, pt: (rb + b, C_K // kvw)),
            pl.BlockSpec((tq, kvw), lambda b, pt: (rb + b, C_V // kvw)),
            pl.BlockSpec(memory_space=pl.ANY),
            pl.BlockSpec(memory_space=pl.ANY),
            pl.BlockSpec(memory_space=pl.ANY),
        ],
        out_specs=pl.BlockSpec((tq, aw), lambda b, pt: (b, 0)),
        scratch_shapes=[
            pltpu.VMEM((2, past * N_KV_HEADS, HEAD_DIM), F32),
            pltpu.VMEM((2, past * N_KV_HEADS, HEAD_DIM), F32),
            pltpu.VMEM((2, n_pages * IDX_DIM, page), F32),
            pltpu.VMEM((tq, past + LANES), F32),
            pltpu.VMEM((tq, past + LANES), F32),
            pltpu.VMEM((N_KV_HEADS, KV_GROUP * tq, past + LANES), F32),
            pltpu.SemaphoreType.DMA((3, 2)),
        ],
    )
    ck = cache_k.reshape(depth, n_pool, page * N_KV_HEADS, HEAD_DIM)
    cv = cache_v.reshape(depth, n_pool, page * N_KV_HEADS, HEAD_DIM)
    return pl.pallas_call(
        functools.partial(_dsa_sample_kernel, layer=layer, n_pages=n_pages, page=page, tq=tq, topk=topk),
        grid_spec=grid_spec,
        out_shape=jax.ShapeDtypeStruct((nb * tq, aw), F32),
        compiler_params=_cparams(("arbitrary",)),
        name="dsa_sample",
    )(page_table.reshape(-1), z, z, z, z, z, ck, cv, jnp.swapaxes(cache_kidx, 2, 3))


def _mem_attn_kernel(q_ref, k_ref, v_ref, o_ref):
    q = q_ref[...] * (HEAD_DIM ** -0.5)
    m_tok = k_ref.shape[0] // MEM_HEADS
    for h in range(MEM_HEADS):
        hs = slice(h * HEAD_DIM, (h + 1) * HEAD_DIM)
        rows = pl.ds(h, m_tok, stride=MEM_HEADS)
        s = lax.dot_general(q[:, hs].astype(BF16), k_ref[rows, :].astype(BF16),
                            (((1,), (1,)), ((), ())), preferred_element_type=F32)
        m = jnp.max(s, axis=1, keepdims=True)
        p = jnp.exp(s - m)
        l = jnp.sum(p, axis=1, keepdims=True)
        o = jnp.dot(p.astype(BF16), v_ref[rows, :].astype(BF16), preferred_element_type=F32)
        o_ref[:, hs] = o / l


def _mem_attn(z, row0, nb, t, tq, mk, mv, layer):
    mw = MEM_HEADS * HEAD_DIM
    nq = t // tq
    rb = row0 // tq
    kv_spec = pl.BlockSpec((None, None) + mk.shape[2:], lambda b, i: (layer, b, 0, 0))
    return pl.pallas_call(
        _mem_attn_kernel,
        grid=(nb, nq),
        in_specs=[pl.BlockSpec((tq, mw), lambda b, i: (rb + b * nq + i, C_MQ // mw)), kv_spec, kv_spec],
        out_specs=pl.BlockSpec((tq, mw), lambda b, i: (b * nq + i, 0)),
        out_shape=jax.ShapeDtypeStruct((nb * t, mw), F32),
        compiler_params=_cparams(("arbitrary", "arbitrary")),
        name="mem_attn",
    )(z, mk, mv)


def _shift_rows(x, sh):
    rows = lax.broadcasted_iota(I32, x.shape, 0)
    return jnp.where(rows >= sh, pltpu.roll(x, sh, axis=0), 0.0)


def _ssm_pair(u, m_w, wx_w, wc_w, pr, pi, s0, nb, nc):
    half = SSM_PAIR_W // 2
    y_in = jnp.dot(u, m_w, preferred_element_type=F32)
    x = jnp.dot(u, wx_w, preferred_element_type=F32)
    xr, xi = x[:, :half], x[:, half:]
    s0r, s0i = s0[:, :half], s0[:, half:]
    ar, ai = pr[0:1], pi[0:1]
    if nc == 1:
        sr = xr + ar * s0r - ai * s0i
        si = xi + ar * s0i + ai * s0r
        pvr, pvi = s0r, s0i
        sfin = jnp.concatenate([sr, si], axis=1)
    else:
        prev_r, prev_i, fin = [], [], []
        first = lax.broadcasted_iota(I32, (nc, half), 0) == 0
        for b in range(nb):
            rs = slice(b * nc, (b + 1) * nc)
            b0r, b0i = s0r[b:b + 1], s0i[b:b + 1]
            sr = xr[rs] + jnp.where(first, ar * b0r - ai * b0i, 0.0)
            si = xi[rs] + jnp.where(first, ar * b0i + ai * b0r, 0.0)
            for mstep in range(int(math.log2(nc))):
                sh = 2 ** mstep
                qr, qi = pr[mstep:mstep + 1], pi[mstep:mstep + 1]
                tr, ti = _shift_rows(sr, sh), _shift_rows(si, sh)
                sr, si = sr + qr * tr - qi * ti, si + qr * ti + qi * tr
            fin.append(jnp.concatenate([sr[nc - 1:nc], si[nc - 1:nc]], axis=1))
            prev_r.append(jnp.where(first, b0r, pltpu.roll(sr, 1, axis=0)))
            prev_i.append(jnp.where(first, b0i, pltpu.roll(si, 1, axis=0)))
        pvr = jnp.concatenate(prev_r, axis=0)
        pvi = jnp.concatenate(prev_i, axis=0)
        sfin = jnp.concatenate(fin, axis=0)
    prev = jnp.concatenate([pvr, pvi], axis=1).astype(BF16)
    return y_in + jnp.dot(prev, wc_w, preferred_element_type=F32), sfin


def _ssm_kernel(u_ref, m_ref, wx_ref, wc_ref, pr_ref, pi_ref, s0_ref, y_ref, sf_ref, *, nb, nc):
    nch = nb * nc
    gw = SSM_GROUP
    tok = [u_ref[pl.ds(i, nch, stride=SSM_CHUNK), :] for i in range(SSM_CHUNK)]
    ys = []
    for q in range(PAIRS_PER_BLOCK):
        base = q * 2 * gw
        u = jnp.concatenate([tok[i][:, base + gg * gw:base + (gg + 1) * gw]
                             for gg in range(2) for i in range(SSM_CHUNK)], axis=1).astype(BF16)
        y, sfin = _ssm_pair(u, m_ref[q], wx_ref[q], wc_ref[q], pr_ref[q], pi_ref[q], s0_ref[q], nb, nc)
        sf_ref[q] = sfin
        ys.append(y)
    for i in range(SSM_CHUNK):
        y_ref[pl.ds(i, nch, stride=SSM_CHUNK), :] = jnp.concatenate(
            [ys[q][:, (gg * SSM_CHUNK + i) * gw:(gg * SSM_CHUNK + i + 1) * gw]
             for q in range(PAIRS_PER_BLOCK) for gg in range(2)], axis=1)


def _ssm(z, row0, rows, mats, s0_pairs, layer, nb, nc):
    m_mat, wx, wc, pr, pi = mats
    _, npair, _, w = m_mat.shape
    pw = pr.shape[2]
    ppb = PAIRS_PER_BLOCK
    blk3 = lambda r, c: pl.BlockSpec((ppb, r, c), lambda j: (j, 0, 0))
    lay3 = lambda a, r, c: _layer_spec(a, layer, (ppb, r, c), lambda j: (j, 0, 0))
    return pl.pallas_call(
        functools.partial(_ssm_kernel, nb=nb, nc=nc),
        grid=(npair // ppb,),
        in_specs=[pl.BlockSpec((rows, LANES), lambda j: (row0 // rows, C_U // LANES + j)),
                  lay3(m_mat, w, w), lay3(wx, w, w), lay3(wc, w, w), lay3(pr, pw, LANES), lay3(pi, pw, LANES),
                  lay3(s0_pairs, nb, w)],
        out_specs=[pl.BlockSpec((rows, LANES), lambda j: (0, j)), blk3(nb, w)],
        out_shape=[jax.ShapeDtypeStruct((rows, npair * 2 * SSM_GROUP), F32),
                   jax.ShapeDtypeStruct((npair, nb, w), F32)],
        compiler_params=_cparams(("arbitrary",)),
        name="ssm",
    )(z, m_mat, wx, wc, pr, pi, s0_pairs)


def _ssm_matrices(lam_re, lam_im, log_dt, b_re, b_im, c_re, c_im, d_skip, n_steps):
    g, p, c = b_re.shape
    L = SSM_CHUNK
    dt = jnp.exp(log_dt)[:, None]
    zr, zi = lam_re * dt, lam_im * dt
    mag = jnp.exp(zr)
    ar, ai = mag * jnp.cos(zi), mag * jnp.sin(zi)
    den = lam_re * lam_re + lam_im * lam_im
    cr = ((ar - 1.0) * lam_re + ai * lam_im) / den
    ci = (ai * lam_re - (ar - 1.0) * lam_im) / den
    bbr = cr[..., None] * b_re - ci[..., None] * b_im
    bbi = cr[..., None] * b_im + ci[..., None] * b_re
    n = jnp.arange(L + 1, dtype=F32)[:, None, None]
    pm = jnp.exp(zr[None] * n)
    pwr, pwi = pm * jnp.cos(zi[None] * n), pm * jnp.sin(zi[None] * n)
    car = c_re[None] * pwr[:, :, None, :] - c_im[None] * pwi[:, :, None, :]
    cai = c_re[None] * pwi[:, :, None, :] + c_im[None] * pwr[:, :, None, :]
    kern = jnp.einsum('ngop,gpc->ngoc', car[:L], bbr) - jnp.einsum('ngop,gpc->ngoc', cai[:L], bbi)
    skip = (jnp.eye(c)[None] * d_skip[:, :, None])[None]
    kern = jnp.concatenate([kern[:1] + skip, kern[1:]], axis=0)
    lag = jnp.arange(L)[None, :] - jnp.arange(L)[:, None]
    shift = (lag[None] == jnp.arange(L)[:, None, None]).astype(F32)
    m_g = jnp.einsum('ngoc,nji->gjcio', kern, shift).reshape(g, L * c, L * c)
    rev = pwr[L - 1 - jnp.arange(L)], pwi[L - 1 - jnp.arange(L)]
    wxr = rev[0][..., None] * bbr[None] - rev[1][..., None] * bbi[None]
    wxi = rev[0][..., None] * bbi[None] + rev[1][..., None] * bbr[None]
    wxr = jnp.transpose(wxr, (1, 0, 3, 2)).reshape(g, L * c, p)
    wxi = jnp.transpose(wxi, (1, 0, 3, 2)).reshape(g, L * c, p)
    wcr = jnp.transpose(car[1:], (1, 3, 0, 2)).reshape(g, p, L * c)
    wci = -jnp.transpose(cai[1:], (1, 3, 0, 2)).reshape(g, p, L * c)

    def pair_blockdiag(a):
        a = a.reshape(g // 2, 2, a.shape[1], a.shape[2])
        z = jnp.zeros_like(a[:, 0])
        return jnp.concatenate([jnp.concatenate([a[:, 0], z], axis=2),
                                jnp.concatenate([z, a[:, 1]], axis=2)], axis=1)

    m_pair = pair_blockdiag(m_g)
    wx_pair = jnp.concatenate([pair_blockdiag(wxr), pair_blockdiag(wxi)], axis=2)
    wc_pair = jnp.concatenate([pair_blockdiag(wcr), pair_blockdiag(wci)], axis=1)
    qr, qi = [pwr[L]], [pwi[L]]
    for _ in range(n_steps - 1):
        qr, qi = qr + [qr[-1] * qr[-1] - qi[-1] * qi[-1]], qi + [2.0 * qr[-1] * qi[-1]]
    rows = -(-n_steps // SUBLANES) * SUBLANES
    padz = [jnp.zeros_like(qr[0])] * (rows - n_steps)
    pr = jnp.transpose(jnp.stack(qr + padz), (1, 0, 2)).reshape(g // 2, 2, rows, p)
    pi = jnp.transpose(jnp.stack(qi + padz), (1, 0, 2)).reshape(g // 2, 2, rows, p)
    pr = jnp.concatenate([pr[:, 0], pr[:, 1]], axis=2)
    pi = jnp.concatenate([pi[:, 0], pi[:, 1]], axis=2)
    return m_pair.astype(BF16), wx_pair.astype(BF16), wc_pair.astype(BF16), pr, pi


def _state_to_pairs(sr, si):
    nb, g, p = sr.shape
    f = lambda s: jnp.transpose(s.reshape(nb, g // 2, 2 * p), (1, 0, 2))
    return jnp.concatenate([f(sr), f(si)], axis=2)


def _state_from_pairs(s):
    npair, nb, w = s.shape
    f = lambda a: jnp.transpose(a, (1, 0, 2)).reshape(nb, npair * 2, w // 4)
    return f(s[:, :, :w // 2]), f(s[:, :, w // 2:])


def _glu_kernel(y_ref, w_ref, o_ref):
    y = jax.nn.gelu(y_ref[...])
    gate = jnp.dot(y.astype(BF16), w_ref[...], preferred_element_type=F32)
    o_ref[...] = y * jax.nn.sigmoid(gate)


def _glu(y, w, layer, tm):
    m, n = y.shape
    return pl.pallas_call(
        _glu_kernel,
        grid=(m // tm,),
        in_specs=[pl.BlockSpec((tm, n), lambda i: (i, 0)), _layer_spec(w, layer, (n, n), lambda i: (0, 0))],
        out_specs=pl.BlockSpec((tm, n), lambda i: (i, 0)),
        out_shape=jax.ShapeDtypeStruct((m, n), F32),
        compiler_params=_cparams(("arbitrary",)),
        name="glu",
    )(y, w)


def _merge_kernel(h_ref, a_ref, s_ref, m_ref, wga_ref, wgs_ref, wgm_ref, wa_ref, ws_ref, wm_ref, o_ref):
    h = h_ref[...]
    dot = lambda x, w: jnp.dot(x, w[...], preferred_element_type=F32)
    gate = lambda w: jax.nn.sigmoid(lax.dot_general(h, w[...], NT_DIMS, preferred_element_type=F32))
    ga, gs, gm = gate(wga_ref), gate(wgs_ref), gate(wgm_ref)
    merged = (ga * dot(a_ref[...].astype(BF16), wa_ref) + gs * dot(s_ref[...].astype(BF16), ws_ref)
              + gm * dot(m_ref[...].astype(BF16), wm_ref))
    o_ref[...] = merged.astype(BF16)


def _merge(h, a, s, mo, wg, wa, ws, wm, layer, tm, tn):
    m, d = h.shape
    nj = d // tn
    row = lambda w: pl.BlockSpec((tm, w), lambda i, j: (i, 0))
    gate = lambda off: _layer_spec(wg, layer, (tn, d), lambda i, j: (off + j, 0))
    lcol = lambda w: _layer_spec(w, layer, (w.shape[1], tn), lambda i, j: (0, j))
    return pl.pallas_call(
        _merge_kernel,
        grid=(m // tm, nj),
        in_specs=[row(d), row(a.shape[1]), row(s.shape[1]), row(mo.shape[1]),
                  gate(0), gate(nj), gate(2 * nj), lcol(wa), lcol(ws), lcol(wm)],
        out_specs=pl.BlockSpec((tm, tn), lambda i, j: (i, j)),
        out_shape=jax.ShapeDtypeStruct((m, d), BF16),
        compiler_params=_cparams(("arbitrary", "arbitrary")),
        name="merge",
    )(h, a, s, mo, wg, wg, wg, wa, ws, wm)


def _proj_res_kernel(x_ref, m_ref, w_ref, o_ref):
    o_ref[...] = x_ref[...] + jnp.dot(m_ref[...], w_ref[...], preferred_element_type=F32)


def _proj_res(x, merged, w, layer, tm):
    m, d = x.shape
    return pl.pallas_call(
        _proj_res_kernel,
        grid=(m // tm,),
        in_specs=[pl.BlockSpec((tm, d), lambda i: (i, 0)),
                  pl.BlockSpec((tm, d), lambda i: (i, 0)),
                  _layer_spec(w, layer, (d, d), lambda i: (0, 0))],
        out_specs=pl.BlockSpec((tm, d), lambda i: (i, 0)),
        out_shape=jax.ShapeDtypeStruct((m, d), F32),
        compiler_params=_cparams(("arbitrary",)),
        name="proj_res",
    )(x, merged, w)


def _ffn_kernel(x_ref, g_ref, w1_ref, w3_ref, w2_ref, o_ref, h_ref):
    f = pl.program_id(1)

    @pl.when(f == 0)
    def _():
        x = x_ref[...]
        r = lax.rsqrt(jnp.mean(x * x, axis=-1, keepdims=True) + EPS)
        h_ref[...] = ((x * r) * g_ref[...]).astype(BF16)
        o_ref[...] = x

    h = h_ref[...]
    a = jnp.dot(h, w1_ref[...], preferred_element_type=F32)
    b = jnp.dot(h, w3_ref[...], preferred_element_type=F32)
    act = (jax.nn.silu(a) * b).astype(BF16)
    o_ref[...] += jnp.dot(act, w2_ref[...], preferred_element_type=F32)


def _ffn(x, g, w1, w3, w2, layer, tm, tf):
    m, d = x.shape
    ff = w1.shape[2]
    return pl.pallas_call(
        _ffn_kernel,
        grid=(m // tm, ff // tf),
        in_specs=[pl.BlockSpec((tm, d), lambda i, f: (i, 0)),
                  _layer_spec(g, layer, (1, d), lambda i, f: (0, 0)),
                  _layer_spec(w1, layer, (d, tf), lambda i, f: (0, f)),
                  _layer_spec(w3, layer, (d, tf), lambda i, f: (0, f)),
                  _layer_spec(w2, layer, (tf, d), lambda i, f: (f, 0))],
        out_specs=pl.BlockSpec((tm, d), lambda i, f: (i, 0)),
        out_shape=jax.ShapeDtypeStruct((m, d), F32),
        scratch_shapes=[pltpu.VMEM((tm, d), BF16)],
        compiler_params=_cparams(("arbitrary", "arbitrary")),
        name="ffn",
    )(x, g, w1, w3, w2)


def kernel(x_prompt, x_sample, mem_prompt, cache_k, cache_v, cache_kidx, cache_mem_k, cache_mem_v, state_ssm_re, state_ssm_im, page_table, norm1_g, w_in, q_norm_g, k_norm_g, ssm_lam_re, ssm_lam_im, ssm_log_dt, ssm_b_re, ssm_b_im, ssm_c_re, ssm_c_im, ssm_d, w_glu, mem_norm_g, w_mem_kv, mq_norm_g, mk_norm_g, w_br_attn, w_br_ssm, w_br_mem, w_o, norm2_g, w_ff1, w_ff3, w_ff2):
    depth = w_in.shape[0]
    bp, tp, d = x_prompt.shape
    bs, ts, _ = x_sample.shape
    rp, rs = bp * tp, bs * ts
    m_tok = mem_prompt.shape[1]
    mw = MEM_HEADS * HEAD_DIM
    kvw = N_KV_HEADS * HEAD_DIM
    groups = ssm_lam_re.shape[1]
    tm_p, tm_s = _row_tile(rp), _row_tile(rs)

    x_p, x_s = x_prompt.reshape(rp, d), x_sample.reshape(rs, d)
    mem = mem_prompt.reshape(bp * m_tok, d)
    cmk = cache_mem_k.reshape(depth, bs, m_tok * MEM_HEADS, HEAD_DIM)
    cmv = cache_mem_v.reshape(depth, bs, m_tok * MEM_HEADS, HEAD_DIM)
    outs = [[] for _ in range(12)]
    nc_p, nc_s = tp // SSM_CHUNK, ts // SSM_CHUNK
    n_steps = max(1, int(math.log2(nc_p)))

    bf = lambda w: w.astype(BF16)
    w_mem_kv_b, w_glu_b, w_o_b = bf(w_mem_kv), bf(w_glu), bf(w_o)
    w_ba_b, w_bs_b, w_bm_b = bf(w_br_attn), bf(w_br_ssm), bf(w_br_mem)
    w_ff1_b, w_ff3_b, w_ff2_b = bf(w_ff1), bf(w_ff3), bf(w_ff2)
    mats = jax.vmap(functools.partial(_ssm_matrices, n_steps=n_steps))(
        ssm_lam_re, ssm_lam_im, ssm_log_dt, ssm_b_re, ssm_b_im, ssm_c_re, ssm_c_im, ssm_d)
    s0_prompt = jnp.zeros((depth, groups // 2, bp, SSM_PAIR_W), F32)
    s0_sample = jax.vmap(_state_to_pairs)(state_ssm_re, state_ssm_im)
    flags, gains = jax.vmap(_epilogue_rows)(q_norm_g, k_norm_g, mq_norm_g)
    mflag = jnp.concatenate([jnp.ones((mw,), F32), jnp.zeros((mw,), F32)])[None]
    mgains = jnp.concatenate([jnp.tile(mk_norm_g, (1, MEM_HEADS)), jnp.ones((depth, mw), F32)], axis=1)[:, None]
    g1, g2, gm = norm1_g[:, None], norm2_g[:, None], mem_norm_g[:, None]
    wp, wg, uw = _pack_w_in(w_in)

    def dense_tail(x, h, a, y, m, l, tm):
        s = _glu(y, w_glu_b, l, tm)
        merged = _merge(h, a, s, m, wg, w_ba_b, w_bs_b, w_bm_b, l, tm, min(512, d))
        x = _proj_res(x, merged, w_o_b, l, min(tm, 512))
        return _ffn(x, g2, w_ff1_b, w_ff3_b, w_ff2_b, l, tm, 512)

    for l in range(depth):
        z_p, h_p = _norm_matmul(x_p, g1[l], wp, flags[l], gains[l], layer=l, tm=tm_p, tn=1024, emit_h=True,
                                w_rows_out=True)
        z_s, h_s = _norm_matmul(x_s, g1[l], wp, flags[l], gains[l], layer=l, tm=tm_s, tn=1024, emit_h=True,
                                w_rows_out=True)

        mkv = _norm_matmul(mem, gm[l], w_mem_kv_b, mflag, mgains[l], layer=l, tm=bp * m_tok, tn=2 * mw,
                           emit_h=False, w_rows_out=False)
        mk_p = mkv[:, :mw].reshape(bp, m_tok, MEM_HEADS, HEAD_DIM)
        mv_p = mkv[:, mw:].reshape(bp, m_tok, MEM_HEADS, HEAD_DIM)

        a_p = _dsa_prompt(z_p, bp, tp)
        a_s = _dsa_sample(z_s, 0, bs, ts, l, cache_k, cache_v, cache_kidx, page_table)

        y_p, sf_p = _ssm(z_p, 0, rp, mats, s0_prompt, l, bp, nc_p)
        y_s, sf_s = _ssm(z_s, 0, rs, mats, s0_sample, l, bs, nc_s)

        m_p = _mem_attn(z_p, 0, bp, tp, 512, mk_p.reshape(1, bp, m_tok * MEM_HEADS, HEAD_DIM),
                        mv_p.reshape(1, bp, m_tok * MEM_HEADS, HEAD_DIM), 0)
        m_s = _mem_attn(z_s, 0, bs, ts, ts, cmk, cmv, l)

        x_p = dense_tail(x_p, h_p, a_p, y_p, m_p, l, tm_p)
        x_s = dense_tail(x_s, h_s, a_s, y_s, m_s, l, tm_s)

        srp, sip = _state_from_pairs(sf_p)
        srs, sis = _state_from_pairs(sf_s)
        new = [
            z_p[:, C_K:C_K + kvw].reshape(bp, tp, N_KV_HEADS, HEAD_DIM),
            z_p[:, C_V:C_V + kvw].reshape(bp, tp, N_KV_HEADS, HEAD_DIM),
            z_p[:, C_KI:C_KI + IDX_DIM].reshape(bp, tp, IDX_DIM),
            mk_p, mv_p, srp, sip,
            z_s[:, C_K:C_K + kvw].reshape(bs, ts, N_KV_HEADS, HEAD_DIM),
            z_s[:, C_V:C_V + kvw].reshape(bs, ts, N_KV_HEADS, HEAD_DIM),
            z_s[:, C_KI:C_KI + IDX_DIM].reshape(bs, ts, IDX_DIM),
            srs, sis,
        ]
        for o, v in zip(outs, new):
            o.append(v)

    st = lambda xs: jnp.stack(xs, axis=0)
    return (x_p.reshape(bp, tp, d), x_s.reshape(bs, ts, d), *[st(o) for o in outs])
```

```python
import functools
import math

import jax
import jax.numpy as jnp
from jax import lax
from jax.experimental import pallas as pl
from jax.experimental.pallas import tpu as pltpu

F32 = jnp.float32
BF16 = jnp.bfloat16
I32 = jnp.int32

LANES = 128
SUBLANES = 8
BF16_SUBLANES = 16
EPS = 1e-6
HEAD_DIM = 128
N_HEADS = 6
N_KV_HEADS = 2
KV_GROUP = N_HEADS // N_KV_HEADS
IDX_HEADS = 4
IDX_DIM = 64
TOPK_MAX = 256
SSM_GROUP = 16
SSM_STATE = 64
SSM_CHUNK = SUBLANES
SSM_PAIR_W = 2 * SSM_CHUNK * SSM_GROUP
PAIRS_PER_BLOCK = LANES // (2 * SSM_GROUP)
MEM_HEADS = 4
KEY_MIN = -2 ** 31
SEARCH_PASSES = 32
NEG_INF = float("-inf")
NEG_BIAS = -1e30
M_INIT = -1e29
VMEM_LIMIT = 56 * 1024 * 1024
MAX_ROW_TILE = 1024

C_Q, C_K, C_V, C_QI, C_MQ, C_KI, C_WI, C_U = 0, 768, 1024, 1280, 1536, 2048, 2112, 2176
Z_COLS = 3072


def _cparams(sem):
    return pltpu.CompilerParams(dimension_semantics=sem, vmem_limit_bytes=VMEM_LIMIT)


def _layer_spec(w, layer, block, index_map):
    assert w.ndim == len(block) + 1
    return pl.BlockSpec((None,) + tuple(block), lambda *idx: (layer,) + tuple(index_map(*idx)))


def _row_tile(m):
    return max(t for t in range(BF16_SUBLANES, MAX_ROW_TILE + 1, BF16_SUBLANES) if m % t == 0)


def _pack_w_in(w_in):
    depth, d, cols = w_in.shape
    aw, kvw, iqw, mw = N_HEADS * HEAD_DIM, N_KV_HEADS * HEAD_DIM, IDX_HEADS * IDX_DIM, MEM_HEADS * HEAD_DIM
    o_ki = aw + 2 * kvw + iqw
    o_u = o_ki + IDX_DIM + IDX_HEADS
    uw = cols - o_u - mw - 3 * d
    o_mq = o_u + uw
    o_g = o_mq + mw
    w_t = jnp.swapaxes(w_in, 1, 2)
    zeros = lambda n: jnp.zeros((depth, n, d), w_in.dtype)
    wp = jnp.concatenate([w_t[:, 0:o_ki], w_t[:, o_mq:o_g], w_t[:, o_ki:o_u], zeros(C_U - C_WI - IDX_HEADS),
                          w_t[:, o_u:o_mq], zeros(Z_COLS - C_U - uw)], axis=1)
    return wp.astype(BF16), w_t[:, o_g:].astype(BF16), uw


def _epilogue_rows(qn, kn, mqn):
    aw, kvw, iqw, mw = N_HEADS * HEAD_DIM, N_KV_HEADS * HEAD_DIM, IDX_HEADS * IDX_DIM, MEM_HEADS * HEAD_DIM
    ones = lambda n: jnp.ones((n,), F32)
    zer = lambda n: jnp.zeros((n,), F32)
    flag = jnp.concatenate([ones(aw + kvw), zer(kvw + iqw), ones(mw), zer(Z_COLS - C_KI)])[None]
    gain = jnp.concatenate([
        jnp.tile(qn, N_HEADS), jnp.tile(kn, N_KV_HEADS), ones(kvw + iqw), jnp.tile(mqn, MEM_HEADS),
        ones(IDX_DIM), jnp.full((IDX_HEADS,), IDX_HEADS ** -0.5, F32), ones(Z_COLS - C_WI - IDX_HEADS)])[None]
    return flag, gain


NT_DIMS = (((1,), (1,)), ((), ()))


def _norm_matmul_kernel(x_ref, g_ref, w_ref, flag_ref, gain_ref, z_ref, *rest, emit_h, w_rows_out):
    if emit_h:
        h_ref, hs_ref = rest
    else:
        (hs_ref,) = rest
    j = pl.program_id(1)

    @pl.when(j == 0)
    def _():
        x = x_ref[...]
        r = lax.rsqrt(jnp.mean(x * x, axis=-1, keepdims=True) + EPS)
        h = ((x * r) * g_ref[...]).astype(BF16)
        hs_ref[...] = h
        if emit_h:
            h_ref[...] = h

    if w_rows_out:
        acc = lax.dot_general(hs_ref[...], w_ref[...], NT_DIMS, preferred_element_type=F32)
    else:
        acc = jnp.dot(hs_ref[...], w_ref[...], preferred_element_type=F32)
    for c in range(acc.shape[1] // LANES):
        sl = slice(c * LANES, (c + 1) * LANES)
        y = acc[:, sl]
        r = lax.rsqrt(jnp.mean(y * y, axis=-1, keepdims=True) + EPS)
        scale = jnp.where(flag_ref[:, sl] > 0.0, r, 1.0) * gain_ref[:, sl]
        z_ref[:, sl] = y * scale


def _norm_matmul(x, g, w, flag, gain, *, layer, tm, tn, emit_h, w_rows_out):
    m, d = x.shape
    if w_rows_out:
        n = w.shape[1]
        w_spec = _layer_spec(w, layer, (tn, d), lambda i, j: (j, 0))
    else:
        n = w.shape[2]
        w_spec = _layer_spec(w, layer, (d, tn), lambda i, j: (0, j))
    out_shape = [jax.ShapeDtypeStruct((m, n), F32)]
    out_specs = [pl.BlockSpec((tm, tn), lambda i, j: (i, j))]
    if emit_h:
        out_shape.append(jax.ShapeDtypeStruct((m, d), BF16))
        out_specs.append(pl.BlockSpec((tm, d), lambda i, j: (i, 0)))
    res = pl.pallas_call(
        functools.partial(_norm_matmul_kernel, emit_h=emit_h, w_rows_out=w_rows_out),
        grid=(m // tm, n // tn),
        in_specs=[
            pl.BlockSpec((tm, d), lambda i, j: (i, 0)),
            pl.BlockSpec((1, d), lambda i, j: (0, 0)),
            w_spec,
            pl.BlockSpec((1, tn), lambda i, j: (0, j)),
            pl.BlockSpec((1, tn), lambda i, j: (0, j)),
        ],
        out_specs=out_specs,
        out_shape=out_shape,
        scratch_shapes=[pltpu.VMEM((tm, d), BF16)],
        compiler_params=_cparams(("arbitrary", "arbitrary")),
        name="norm_matmul",
    )(x, g, w, flag, gain)
    return res if emit_h else res[0]


def _ordered_to_f32(key):
    return pltpu.bitcast(jnp.where(key >= 0, key, key ^ 0x7FFFFFFF), F32)


def _count(score_ref, n_slab, slab_w, indicator):
    rows = score_ref.shape[0]

    def slab(s, acc):
        static = isinstance(s, int)
        base = s * slab_w if static else pl.multiple_of(s * slab_w, slab_w)
        parts = []
        for c in range(slab_w // LANES):
            off = base + c * LANES
            x = score_ref[:, off:off + LANES] if static else score_ref[:, pl.ds(off, LANES)]
            parts.append(indicator(x, off))
        while len(parts) > 1:
            parts = [a + b for a, b in zip(parts[0::2], parts[1::2])] + ([parts[-1]] if len(parts) % 2 else [])
        return acc + parts[0]

    acc = jnp.zeros((rows, LANES), F32)
    if isinstance(n_slab, int):
        for s in range(n_slab):
            acc = slab(s, acc)
    else:
        acc = lax.fori_loop(0, n_slab, slab, acc)
    return jnp.sum(acc, axis=1, keepdims=True)


def _kth_largest(score_ref, n_slab, slab_w, k):
    rows = score_ref.shape[0]
    count_ge = lambda cand: _count(score_ref, n_slab, slab_w, lambda x, off: jnp.where(x >= cand, 1.0, 0.0))
    lo = jnp.where(count_ge(jnp.zeros((rows, 1), F32)) >= k, 0, KEY_MIN)

    def body(it, lo):
        cand = lo | (1 << (30 - it))
        return jnp.where(count_ge(_ordered_to_f32(cand)) >= k, cand, lo)

    lo = lax.fori_loop(0, SEARCH_PASSES - 1, body, lo)
    return jnp.where(lo == KEY_MIN, NEG_INF, _ordered_to_f32(lo))


def _stack_heads(x, first, count, width):
    return jnp.concatenate([x[:, (first + r) * width:(first + r + 1) * width] for r in range(count)], axis=0)


def _indexer_scores(qi_stack, w_cols, ki_c, rows, keys_on_lanes=False):
    dims = (((1,), (0,)), ((), ())) if keys_on_lanes else (((1,), (1,)), ((), ()))
    logits = lax.dot_general(qi_stack, ki_c, dims, preferred_element_type=F32)
    score = jnp.maximum(logits[0:rows], 0.0) * w_cols[0]
    for h in range(1, IDX_HEADS):
        score = score + jnp.maximum(logits[h * rows:(h + 1) * rows], 0.0) * w_cols[h]
    return score


TQ = 128
KC = 512
KC_ATT = 1024
ROW_SLAB = KC
LOG2_E = math.log2(math.e)


def _count_keys(score_ref, n_slab, indicator):
    lanes = score_ref.shape[1]

    def slab(s, acc):
        base = pl.multiple_of(s * ROW_SLAB, ROW_SLAB)
        parts = [indicator(score_ref[pl.ds(base + r * SUBLANES, SUBLANES), :]) for r in range(ROW_SLAB // SUBLANES)]
        while len(parts) > 1:
            parts = [a + b for a, b in zip(parts[0::2], parts[1::2])] + ([parts[-1]] if len(parts) % 2 else [])
        return acc + parts[0]

    acc = lax.fori_loop(0, n_slab, slab, jnp.zeros((SUBLANES, lanes), F32))
    return jnp.sum(acc, axis=0, keepdims=True)


def _kth_largest_keys(score_ref, n_slab, k):
    lanes = score_ref.shape[1]
    count_ge = lambda cand: _count_keys(score_ref, n_slab, lambda x: jnp.where(x >= cand, 1.0, 0.0))
    lo = jnp.where(count_ge(jnp.zeros((1, lanes), F32)) >= k, 0, KEY_MIN)

    def body(it, lo):
        cand = lo | (1 << (30 - it))
        return jnp.where(count_ge(_ordered_to_f32(cand)) >= k, cand, lo)

    lo = lax.fori_loop(0, SEARCH_PASSES - 1, body, lo)
    return jnp.where(lo == KEY_MIN, NEG_INF, _ordered_to_f32(lo))


def _bias_keys(score_ref, bias_ref, n_slab, k, thr):
    lanes = score_ref.shape[1]
    need = k - _count_keys(score_ref, n_slab, lambda x: jnp.where(x > thr, 1.0, 0.0))
    tri = (lax.broadcasted_iota(I32, (LANES, LANES), 0)
           >= lax.broadcasted_iota(I32, (LANES, LANES), 1)).astype(BF16)

    def body(s, seen):
        base = pl.multiple_of(s * ROW_SLAB, ROW_SLAB)
        xs = [score_ref[pl.ds(base + j * LANES, LANES), :] for j in range(ROW_SLAB // LANES)]
        eqs = [jnp.where(x == thr, 1.0, 0.0) for x in xs]
        prefixes = [jnp.dot(tri, e.astype(BF16), preferred_element_type=F32) for e in eqs]
        for j, (x, eqf, prefix) in enumerate(zip(xs, eqs, prefixes)):
            keep = jnp.where(x > thr, 1.0, jnp.where(seen + prefix <= need, eqf, 0.0))
            bias_ref[pl.ds(base + j * LANES, LANES), :] = jnp.where(
                keep > 0.0, jnp.where(x > NEG_INF, 0.0, NEG_BIAS), NEG_BIAS)
            seen = seen + prefix[LANES - 1:LANES, :]
        return seen

    lax.fori_loop(0, n_slab, body, jnp.zeros((1, lanes), F32))


def _dsa_prompt_kernel(q_ref, qi_ref, kiwq_ref, k_ref, v_ref, kiw_ref, o_ref,
                       kbf_ref, vt_ref, kibf_ref, score_ref, bias_ref, *, topk, kc_att):
    i = pl.program_id(1)
    seq = k_ref.shape[0]

    @pl.when(i == 0)
    def _():
        kbf_ref[...] = k_ref[...].astype(BF16)
        kibf_ref[...] = kiw_ref[...].astype(BF16)
        for c in range(seq // KC):
            vt_ref[:, c * KC:(c + 1) * KC] = v_ref[c * KC:(c + 1) * KC, :].T.astype(BF16)

    n_kc = (i * TQ + TQ + KC - 1) // KC
    n_att = (n_kc * KC + kc_att - 1) // kc_att
    qi_stack = _stack_heads(qi_ref[...], 0, IDX_HEADS, IDX_DIM).astype(BF16)
    w_t = kiwq_ref[...].T
    scale_i = IDX_DIM ** -0.5
    w_rows = [w_t[IDX_DIM + h:IDX_DIM + h + 1, :] * scale_i for h in range(IDX_HEADS)]
    qpos = i * TQ + lax.broadcasted_iota(I32, (KC, TQ), 1)

    def score_body(c, _):
        off = pl.multiple_of(c * KC, KC)
        logits = lax.dot_general(kibf_ref[pl.ds(off, KC), 0:IDX_DIM], qi_stack, (((1,), (1,)), ((), ())),
                                 preferred_element_type=F32)
        score = jnp.maximum(logits[:, 0:TQ], 0.0) * w_rows[0]
        for h in range(1, IDX_HEADS):
            score = score + jnp.maximum(logits[:, h * TQ:(h + 1) * TQ], 0.0) * w_rows[h]
        kpos = off + lax.broadcasted_iota(I32, (KC, TQ), 0)
        score_ref[pl.ds(off, KC), :] = jnp.where(kpos > qpos, NEG_INF, score)
        return 0

    lax.fori_loop(0, n_kc, score_body, 0)
    thr = _kth_largest_keys(score_ref, n_kc, topk)
    _bias_keys(score_ref, bias_ref, n_kc, topk, thr)

    @pl.when(n_att * kc_att > n_kc * KC)
    def _():
        bias_ref[pl.ds(pl.multiple_of(n_kc * KC, KC), KC), :] = jnp.full((KC, TQ), NEG_BIAS, F32)

    q = q_ref[...] * (HEAD_DIM ** -0.5 * LOG2_E)
    q_gs = [_stack_heads(q, g * KV_GROUP, KV_GROUP, HEAD_DIM).astype(BF16) for g in range(N_KV_HEADS)]
    gq = KV_GROUP * TQ

    def att_body(c, carries):
        off = pl.multiple_of(c * kc_att, kc_att)
        b = bias_ref[pl.ds(off, kc_att), :]
        bias = jnp.concatenate([b] * KV_GROUP, axis=1)
        out = []
        for g in range(N_KV_HEADS):
            gs = slice(g * HEAD_DIM, (g + 1) * HEAD_DIM)
            m, l, acc = carries[g]
            s = lax.dot_general(kbf_ref[pl.ds(off, kc_att), gs], q_gs[g], NT_DIMS,
                                preferred_element_type=F32) + bias
            m_new = jnp.maximum(m, jnp.max(s, axis=0, keepdims=True))
            alpha = jnp.exp2(m - m_new)
            p = jnp.exp2(s - m_new)
            l = alpha * l + jnp.sum(p, axis=0, keepdims=True)
            acc = alpha * acc + jnp.dot(vt_ref[gs, pl.ds(off, kc_att)], p.astype(BF16),
                                        preferred_element_type=F32)
            out.append((m_new, l, acc))
        return tuple(out)

    init = (jnp.full((1, gq), M_INIT, F32), jnp.zeros((1, gq), F32), jnp.zeros((HEAD_DIM, gq), F32))
    carries = lax.fori_loop(0, n_att, att_body, (init,) * N_KV_HEADS)
    for g, (_, l, acc) in enumerate(carries):
        o_t = acc / l
        for r in range(KV_GROUP):
            hh = g * KV_GROUP + r
            o_ref[:, hh * HEAD_DIM:(hh + 1) * HEAD_DIM] = o_t[:, r * TQ:(r + 1) * TQ].T


def _dsa_prompt(z, batch, seq):
    nq = seq // TQ
    topk = min(TOPK_MAX, seq // 4)
    aw, kvw, iqw = N_HEADS * HEAD_DIM, N_KV_HEADS * HEAD_DIM, IDX_HEADS * IDX_DIM
    return pl.pallas_call(
        functools.partial(_dsa_prompt_kernel, topk=topk, kc_att=min(KC_ATT, seq)),
        grid=(batch, nq),
        in_specs=[
            pl.BlockSpec((TQ, aw), lambda b, i: (b * nq + i, C_Q // aw)),
            pl.BlockSpec((TQ, iqw), lambda b, i: (b * nq + i, C_QI // iqw)),
            pl.BlockSpec((TQ, LANES), lambda b, i: (b * nq + i, C_KI // LANES)),
            pl.BlockSpec((seq, kvw), lambda b, i: (b, C_K // kvw)),
            pl.BlockSpec((seq, kvw), lambda b, i: (b, C_V // kvw)),
            pl.BlockSpec((seq, LANES), lambda b, i: (b, C_KI // LANES)),
        ],
        out_specs=pl.BlockSpec((TQ, aw), lambda b, i: (b * nq + i, 0)),
        out_shape=jax.ShapeDtypeStruct((batch * seq, aw), F32),
        scratch_shapes=[
            pltpu.VMEM((seq, kvw), BF16),
            pltpu.VMEM((kvw, seq), BF16),
            pltpu.VMEM((seq, LANES), BF16),
            pltpu.VMEM((seq, TQ), F32),
            pltpu.VMEM((seq, TQ), F32),
        ],
        compiler_params=_cparams(("arbitrary", "arbitrary")),
        name="dsa_prompt",
    )(z, z, z, z, z, z)


KC_S = 1024


def _tie_cut(score_ref, n128, k, thr):
    rows = score_ref.shape[0]
    width = n128 * LANES
    need = k - _count(score_ref, 1, width, lambda x, off: jnp.where(x > thr, 1.0, 0.0))
    ties = _count(score_ref, 1, width, lambda x, off: jnp.where(x == thr, 1.0, 0.0))
    lane = lax.broadcasted_iota(I32, (rows, LANES), 1)
    everything = jnp.full((rows, 1), width - 1, I32)

    def bisect():
        def ties_upto(j):
            return _count(score_ref, 1, width,
                          lambda x, off: jnp.where(x == thr, jnp.where(lane + off <= j, 1.0, 0.0), 0.0))

        def body(_, c):
            lo, hi = c
            mid = (lo + hi) >> 1
            ok = ties_upto(mid) >= need
            return jnp.where(ok, lo, mid), jnp.where(ok, mid, hi)

        steps = int(math.ceil(math.log2(width + 1)))
        return lax.fori_loop(0, steps, body, (jnp.full((rows, 1), -1, I32), everything))[1]

    return lax.cond(jnp.max(ties - need) > 0.0, bisect, lambda: everything)


def _write_bias(score_ref, bias_ref, n128, thr, cut):
    rows = score_ref.shape[0]
    lane = lax.broadcasted_iota(I32, (rows, LANES), 1)
    for c in range(n128):
        x = score_ref[:, c * LANES:(c + 1) * LANES]
        tie = jnp.where(x == thr, jnp.where(lane + c * LANES <= cut, 1.0, 0.0), 0.0)
        keep = jnp.where(x > thr, 1.0, tie)
        bias_ref[:, c * LANES:(c + 1) * LANES] = jnp.where(keep > 0.0, jnp.where(x > NEG_INF, 0.0, NEG_BIAS),
                                                           NEG_BIAS)


def _lane_fold(x, op):
    parts = [x[:, c * LANES:(c + 1) * LANES] for c in range(x.shape[1] // LANES)]
    while len(parts) > 1:
        parts = [op(a, b) for a, b in zip(parts[0::2], parts[1::2])] + ([parts[-1]] if len(parts) % 2 else [])
    return parts[0]


def _dsa_sample_kernel(pt_ref, q_ref, qi_ref, kiw_ref, kn_ref, vn_ref, ck_ref, cv_ref, cki_ref, o_ref,
                       kbuf, vbuf, kibuf, score_ref, bias_ref, s_ref, sem, *, layer, n_pages, page, tq, topk):
    b = pl.program_id(0)
    nb = pl.num_programs(0)
    past = n_pages * page
    slot = b % 2
    streams = ((ck_ref, kbuf, page * N_KV_HEADS), (cv_ref, vbuf, page * N_KV_HEADS), (cki_ref, kibuf, IDX_DIM))

    def page_copy(which, bb, sl, p):
        src, dst, rpp = streams[which]
        return pltpu.make_async_copy(src.at[layer, pt_ref[bb * n_pages + p]],
                                     dst.at[sl, pl.ds(p * rpp, rpp)], sem.at[which, sl])

    def start_fetch(bb, sl):
        def body(p, _):
            for which in range(len(streams)):
                page_copy(which, bb, sl, p).start(priority=which % 2)
            return 0
        lax.fori_loop(0, n_pages, body, 0)

    def wait_fetch(sl, which):
        _, dst, _ = streams[which]
        pltpu.make_async_copy(dst.at[sl], dst.at[sl], sem.at[which, sl]).wait()

    @pl.when(b == 0)
    def _():
        start_fetch(0, 0)

    @pl.when(b + 1 < nb)
    def _():
        start_fetch(b + 1, 1 - slot)

    qi_stack = _stack_heads(qi_ref[...], 0, IDX_HEADS, IDX_DIM).astype(BF16)
    kiw = kiw_ref[...]
    scale_i = IDX_DIM ** -0.5
    w_cols = [kiw[:, IDX_DIM + h:IDX_DIM + h + 1] * scale_i for h in range(IDX_HEADS)]
    n128 = past // LANES + 1
    ppc = KC // page

    wait_fetch(slot, 2)
    for c in range(past // KC):
        ki_t = jnp.concatenate([kibuf[slot, (c * ppc + j) * IDX_DIM:(c * ppc + j + 1) * IDX_DIM, :]
                                for j in range(ppc)], axis=1).astype(BF16)
        score_ref[:, c * KC:(c + 1) * KC] = _indexer_scores(qi_stack, w_cols, ki_t, tq, keys_on_lanes=True)
    pad = jnp.zeros((LANES - tq, LANES), F32)
    ki_new = jnp.concatenate([kiw, pad], axis=0)[:, 0:IDX_DIM].astype(BF16)
    score_new = _indexer_scores(qi_stack, w_cols, ki_new, tq)
    newer = lax.broadcasted_iota(I32, (tq, LANES), 1) > lax.broadcasted_iota(I32, (tq, LANES), 0)
    score_ref[:, past:past + LANES] = jnp.where(newer, NEG_INF, score_new)

    thr = _kth_largest(score_ref, 1, n128 * LANES, topk)
    _write_bias(score_ref, bias_ref, n128, thr, _tie_cut(score_ref, n128, topk, thr))

    wait_fetch(slot, 0)
    wait_fetch(slot, 1)
    q = q_ref[...] * (HEAD_DIM ** -0.5)
    q_gs = [_stack_heads(q, g * KV_GROUP, KV_GROUP, HEAD_DIM).astype(BF16) for g in range(N_KV_HEADS)]
    kv_pad = jnp.zeros((LANES - tq, N_KV_HEADS * HEAD_DIM), F32)
    k_new = jnp.concatenate([kn_ref[...], kv_pad], axis=0).astype(BF16)
    v_new = jnp.concatenate([vn_ref[...], kv_pad], axis=0).astype(BF16)
    chunks = [(c * KC_S, KC_S) for c in range(past // KC_S)] + [(past, LANES)]

    def keys_of(buf, new, g, start, size):
        if start == past:
            return new[:, g * HEAD_DIM:(g + 1) * HEAD_DIM]
        return buf[slot, pl.ds(start * N_KV_HEADS + g, size, stride=N_KV_HEADS), :].astype(BF16)

    m_acc = [jnp.full((KV_GROUP * tq, LANES), M_INIT, F32) for _ in range(N_KV_HEADS)]
    for start, size in chunks:
        bias = jnp.concatenate([bias_ref[:, start:start + size]] * KV_GROUP, axis=0)
        for g in range(N_KV_HEADS):
            s = lax.dot_general(q_gs[g], keys_of(kbuf, k_new, g, start, size), NT_DIMS,
                                preferred_element_type=F32) + bias
            s_ref[g, :, start:start + size] = s
            m_acc[g] = jnp.maximum(m_acc[g], _lane_fold(s, jnp.maximum))
    m = [jnp.max(a, axis=1, keepdims=True) for a in m_acc]
    l_acc = [jnp.zeros((KV_GROUP * tq, LANES), F32) for _ in range(N_KV_HEADS)]
    acc = [jnp.zeros((KV_GROUP * tq, HEAD_DIM), F32) for _ in range(N_KV_HEADS)]
    for start, size in chunks:
        for g in range(N_KV_HEADS):
            p = jnp.exp(s_ref[g, :, start:start + size] - m[g])
            l_acc[g] = l_acc[g] + _lane_fold(p, jnp.add)
            acc[g] = acc[g] + jnp.dot(p.astype(BF16), keys_of(vbuf, v_new, g, start, size),
                                      preferred_element_type=F32)
    for g in range(N_KV_HEADS):
        o = acc[g] / jnp.sum(l_acc[g], axis=1, keepdims=True)
        for r in range(KV_GROUP):
            hh = g * KV_GROUP + r
            o_ref[:, hh * HEAD_DIM:(hh + 1) * HEAD_DIM] = o[r * tq:(r + 1) * tq]


def _dsa_sample(z, row0, nb, tq, layer, cache_k, cache_v, cache_kidx, page_table):
    depth, n_pool, page = cache_k.shape[:3]
    n_pages = page_table.shape[1]
    past = n_pages * page
    topk = min(TOPK_MAX, (past + tq) // 4)
    aw, kvw, iqw = N_HEADS * HEAD_DIM, N_KV_HEADS * HEAD_DIM, IDX_HEADS * IDX_DIM
    rb = row0 // tq
    grid_spec = pltpu.PrefetchScalarGridSpec(
        num_scalar_prefetch=1,
        grid=(nb,),
        in_specs=[
            pl.BlockSpec((tq, aw), lambda b, pt: (rb + b, C_Q // aw)),
            pl.BlockSpec((tq, iqw), lambda b, pt: (rb + b, C_QI // iqw)),
            pl.BlockSpec((tq, LANES), lambda b, pt: (rb + b, C_KI // LANES)),
            pl.BlockSpec((tq, kvw), lambda b, pt: (rb + b, C_K // kvw)),
            pl.BlockSpec((tq, kvw), lambda b, pt: (rb + b, C_V // kvw)),
            pl.BlockSpec(memory_space=pl.ANY),
            pl.BlockSpec(memory_space=pl.ANY),
            pl.BlockSpec(memory_space=pl.ANY),
        ],
        out_specs=pl.BlockSpec((tq, aw), lambda b, pt: (b, 0)),
        scratch_shapes=[
            pltpu.VMEM((2, past * N_KV_HEADS, HEAD_DIM), F32),
            pltpu.VMEM((2, past * N_KV_HEADS, HEAD_DIM), F32),
            pltpu.VMEM((2, n_pages * IDX_DIM, page), F32),
            pltpu.VMEM((tq, past + LANES), F32),
            pltpu.VMEM((tq, past + LANES), F32),
            pltpu.VMEM((N_KV_HEADS, KV_GROUP * tq, past + LANES), F32),
            pltpu.SemaphoreType.DMA((3, 2)),
        ],
    )
    ck = cache_k.reshape(depth, n_pool, page * N_KV_HEADS, HEAD_DIM)
    cv = cache_v.reshape(depth, n_pool, page * N_KV_HEADS, HEAD_DIM)
    return pl.pallas_call(
        functools.partial(_dsa_sample_kernel, layer=layer, n_pages=n_pages, page=page, tq=tq, topk=topk),
        grid_spec=grid_spec,
        out_shape=jax.ShapeDtypeStruct((nb * tq, aw), F32),
        compiler_params=_cparams(("arbitrary",)),
        name="dsa_sample",
    )(page_table.reshape(-1), z, z, z, z, z, ck, cv, jnp.swapaxes(cache_kidx, 2, 3))


def _mem_attn_kernel(q_ref, k_ref, v_ref, o_ref):
    q = q_ref[...] * (HEAD_DIM ** -0.5)
    m_tok = k_ref.shape[0] // MEM_HEADS
    for h in range(MEM_HEADS):
        hs = slice(h * HEAD_DIM, (h + 1) * HEAD_DIM)
        rows = pl.ds(h, m_tok, stride=MEM_HEADS)
        s = lax.dot_general(q[:, hs].astype(BF16), k_ref[rows, :].astype(BF16),
                            (((1,), (1,)), ((), ())), preferred_element_type=F32)
        m = jnp.max(s, axis=1, keepdims=True)
        p = jnp.exp(s - m)
        l = jnp.sum(p, axis=1, keepdims=True)
        o = jnp.dot(p.astype(BF16), v_ref[rows, :].astype(BF16), preferred_element_type=F32)
        o_ref[:, hs] = o / l


def _mem_attn(z, row0, nb, t, tq, mk, mv, layer):
    mw = MEM_HEADS * HEAD_DIM
    nq = t // tq
    rb = row0 // tq
    kv_spec = pl.BlockSpec((None, None) + mk.shape[2:], lambda b, i: (layer, b, 0, 0))
    return pl.pallas_call(
        _mem_attn_kernel,
        grid=(nb, nq),
        in_specs=[pl.BlockSpec((tq, mw), lambda b, i: (rb + b * nq + i, C_MQ // mw)), kv_spec, kv_spec],
        out_specs=pl.BlockSpec((tq, mw), lambda b, i: (b * nq + i, 0)),
        out_shape=jax.ShapeDtypeStruct((nb * t, mw), F32),
        compiler_params=_cparams(("arbitrary", "arbitrary")),
        name="mem_attn",
    )(z, mk, mv)


def _shift_rows(x, sh):
    rows = lax.broadcasted_iota(I32, x.shape, 0)
    return jnp.where(rows >= sh, pltpu.roll(x, sh, axis=0), 0.0)


def _ssm_pair(u, m_w, wx_w, wc_w, pr, pi, s0, nb, nc):
    half = SSM_PAIR_W // 2
    y_in = jnp.dot(u, m_w, preferred_element_type=F32)
    x = jnp.dot(u, wx_w, preferred_element_type=F32)
    xr, xi = x[:, :half], x[:, half:]
    s0r, s0i = s0[:, :half], s0[:, half:]
    ar, ai = pr[0:1], pi[0:1]
    if nc == 1:
        sr = xr + ar * s0r - ai * s0i
        si = xi + ar * s0i + ai * s0r
        pvr, pvi = s0r, s0i
        sfin = jnp.concatenate([sr, si], axis=1)
    else:
        prev_r, prev_i, fin = [], [], []
        first = lax.broadcasted_iota(I32, (nc, half), 0) == 0
        for b in range(nb):
            rs = slice(b * nc, (b + 1) * nc)
            b0r, b0i = s0r[b:b + 1], s0i[b:b + 1]
            sr = xr[rs] + jnp.where(first, ar * b0r - ai * b0i, 0.0)
            si = xi[rs] + jnp.where(first, ar * b0i + ai * b0r, 0.0)
            for mstep in range(int(math.log2(nc))):
                sh = 2 ** mstep
                qr, qi = pr[mstep:mstep + 1], pi[mstep:mstep + 1]
                tr, ti = _shift_rows(sr, sh), _shift_rows(si, sh)
                sr, si = sr + qr * tr - qi * ti, si + qr * ti + qi * tr
            fin.append(jnp.concatenate([sr[nc - 1:nc], si[nc - 1:nc]], axis=1))
            prev_r.append(jnp.where(first, b0r, pltpu.roll(sr, 1, axis=0)))
            prev_i.append(jnp.where(first, b0i, pltpu.roll(si, 1, axis=0)))
        pvr = jnp.concatenate(prev_r, axis=0)
        pvi = jnp.concatenate(prev_i, axis=0)
        sfin = jnp.concatenate(fin, axis=0)
    prev = jnp.concatenate([pvr, pvi], axis=1).astype(BF16)
    return y_in + jnp.dot(prev, wc_w, preferred_element_type=F32), sfin


def _ssm_kernel(u_ref, m_ref, wx_ref, wc_ref, pr_ref, pi_ref, s0_ref, y_ref, sf_ref, *, nb, nc):
    nch = nb * nc
    gw = SSM_GROUP
    tok = [u_ref[pl.ds(i, nch, stride=SSM_CHUNK), :] for i in range(SSM_CHUNK)]
    ys = []
    for q in range(PAIRS_PER_BLOCK):
        base = q * 2 * gw
        u = jnp.concatenate([tok[i][:, base + gg * gw:base + (gg + 1) * gw]
                             for gg in range(2) for i in range(SSM_CHUNK)], axis=1).astype(BF16)
        y, sfin = _ssm_pair(u, m_ref[q], wx_ref[q], wc_ref[q], pr_ref[q], pi_ref[q], s0_ref[q], nb, nc)
        sf_ref[q] = sfin
        ys.append(y)
    for i in range(SSM_CHUNK):
        y_ref[pl.ds(i, nch, stride=SSM_CHUNK), :] = jnp.concatenate(
            [ys[q][:, (gg * SSM_CHUNK + i) * gw:(gg * SSM_CHUNK + i + 1) * gw]
             for q in range(PAIRS_PER_BLOCK) for gg in range(2)], axis=1)


def _ssm(z, row0, rows, mats, s0_pairs, layer, nb, nc):
    m_mat, wx, wc, pr, pi = mats
    _, npair, _, w = m_mat.shape
    pw = pr.shape[2]
    ppb = PAIRS_PER_BLOCK
    blk3 = lambda r, c: pl.BlockSpec((ppb, r, c), lambda j: (j, 0, 0))
    lay3 = lambda a, r, c: _layer_spec(a, layer, (ppb, r, c), lambda j: (j, 0, 0))
    return pl.pallas_call(
        functools.partial(_ssm_kernel, nb=nb, nc=nc),
        grid=(npair // ppb,),
        in_specs=[pl.BlockSpec((rows, LANES), lambda j: (row0 // rows, C_U // LANES + j)),
                  lay3(m_mat, w, w), lay3(wx, w, w), lay3(wc, w, w), lay3(pr, pw, LANES), lay3(pi, pw, LANES),
                  lay3(s0_pairs, nb, w)],
        out_specs=[pl.BlockSpec((rows, LANES), lambda j: (0, j)), blk3(nb, w)],
        out_shape=[jax.ShapeDtypeStruct((rows, npair * 2 * SSM_GROUP), F32),
                   jax.ShapeDtypeStruct((npair, nb, w), F32)],
        compiler_params=_cparams(("arbitrary",)),
        name="ssm",
    )(z, m_mat, wx, wc, pr, pi, s0_pairs)


def _ssm_matrices(lam_re, lam_im, log_dt, b_re, b_im, c_re, c_im, d_skip, n_steps):
    g, p, c = b_re.shape
    L = SSM_CHUNK
    dt = jnp.exp(log_dt)[:, None]
    zr, zi = lam_re * dt, lam_im * dt
    mag = jnp.exp(zr)
    ar, ai = mag * jnp.cos(zi), mag * jnp.sin(zi)
    den = lam_re * lam_re + lam_im * lam_im
    cr = ((ar - 1.0) * lam_re + ai * lam_im) / den
    ci = (ai * lam_re - (ar - 1.0) * lam_im) / den
    bbr = cr[..., None] * b_re - ci[..., None] * b_im
    bbi = cr[..., None] * b_im + ci[..., None] * b_re
    n = jnp.arange(L + 1, dtype=F32)[:, None, None]
    pm = jnp.exp(zr[None] * n)
    pwr, pwi = pm * jnp.cos(zi[None] * n), pm * jnp.sin(zi[None] * n)
    car = c_re[None] * pwr[:, :, None, :] - c_im[None] * pwi[:, :, None, :]
    cai = c_re[None] * pwi[:, :, None, :] + c_im[None] * pwr[:, :, None, :]
    kern = jnp.einsum('ngop,gpc->ngoc', car[:L], bbr) - jnp.einsum('ngop,gpc->ngoc', cai[:L], bbi)
    skip = (jnp.eye(c)[None] * d_skip[:, :, None])[None]
    kern = jnp.concatenate([kern[:1] + skip, kern[1:]], axis=0)
    lag = jnp.arange(L)[None, :] - jnp.arange(L)[:, None]
    shift = (lag[None] == jnp.arange(L)[:, None, None]).astype(F32)
    m_g = jnp.einsum('ngoc,nji->gjcio', kern, shift).reshape(g, L * c, L * c)
    rev = pwr[L - 1 - jnp.arange(L)], pwi[L - 1 - jnp.arange(L)]
    wxr = rev[0][..., None] * bbr[None] - rev[1][..., None] * bbi[None]
    wxi = rev[0][..., None] * bbi[None] + rev[1][..., None] * bbr[None]
    wxr = jnp.transpose(wxr, (1, 0, 3, 2)).reshape(g, L * c, p)
    wxi = jnp.transpose(wxi, (1, 0, 3, 2)).reshape(g, L * c, p)
    wcr = jnp.transpose(car[1:], (1, 3, 0, 2)).reshape(g, p, L * c)
    wci = -jnp.transpose(cai[1:], (1, 3, 0, 2)).reshape(g, p, L * c)

    def pair_blockdiag(a):
        a = a.reshape(g // 2, 2, a.shape[1], a.shape[2])
        z = jnp.zeros_like(a[:, 0])
        return jnp.concatenate([jnp.concatenate([a[:, 0], z], axis=2),
                                jnp.concatenate([z, a[:, 1]], axis=2)], axis=1)

    m_pair = pair_blockdiag(m_g)
    wx_pair = jnp.concatenate([pair_blockdiag(wxr), pair_blockdiag(wxi)], axis=2)
    wc_pair = jnp.concatenate([pair_blockdiag(wcr), pair_blockdiag(wci)], axis=1)
    qr, qi = [pwr[L]], [pwi[L]]
    for _ in range(n_steps - 1):
        qr, qi = qr + [qr[-1] * qr[-1] - qi[-1] * qi[-1]], qi + [2.0 * qr[-1] * qi[-1]]
    rows = -(-n_steps // SUBLANES) * SUBLANES
    padz = [jnp.zeros_like(qr[0])] * (rows - n_steps)
    pr = jnp.transpose(jnp.stack(qr + padz), (1, 0, 2)).reshape(g // 2, 2, rows, p)
    pi = jnp.transpose(jnp.stack(qi + padz), (1, 0, 2)).reshape(g // 2, 2, rows, p)
    pr = jnp.concatenate([pr[:, 0], pr[:, 1]], axis=2)
    pi = jnp.concatenate([pi[:, 0], pi[:, 1]], axis=2)
    return m_pair.astype(BF16), wx_pair.astype(BF16), wc_pair.astype(BF16), pr, pi


def _state_to_pairs(sr, si):
    nb, g, p = sr.shape
    f = lambda s: jnp.transpose(s.reshape(nb, g // 2, 2 * p), (1, 0, 2))
    return jnp.concatenate([f(sr), f(si)], axis=2)


def _state_from_pairs(s):
    npair, nb, w = s.shape
    f = lambda a: jnp.transpose(a, (1, 0, 2)).reshape(nb, npair * 2, w // 4)
    return f(s[:, :, :w // 2]), f(s[:, :, w // 2:])


def _glu_kernel(y_ref, w_ref, o_ref):
    y = jax.nn.gelu(y_ref[...])
    gate = jnp.dot(y.astype(BF16), w_ref[...], preferred_element_type=F32)
    o_ref[...] = y * jax.nn.sigmoid(gate)


def _glu(y, w, layer, tm):
    m, n = y.shape
    return pl.pallas_call(
        _glu_kernel,
        grid=(m // tm,),
        in_specs=[pl.BlockSpec((tm, n), lambda i: (i, 0)), _layer_spec(w, layer, (n, n), lambda i: (0, 0))],
        out_specs=pl.BlockSpec((tm, n), lambda i: (i, 0)),
        out_shape=jax.ShapeDtypeStruct((m, n), F32),
        compiler_params=_cparams(("arbitrary",)),
        name="glu",
    )(y, w)


def _merge_kernel(h_ref, a_ref, s_ref, m_ref, wga_ref, wgs_ref, wgm_ref, wa_ref, ws_ref, wm_ref, o_ref):
    h = h_ref[...]
    dot = lambda x, w: jnp.dot(x, w[...], preferred_element_type=F32)
    gate = lambda w: jax.nn.sigmoid(lax.dot_general(h, w[...], NT_DIMS, preferred_element_type=F32))
    ga, gs, gm = gate(wga_ref), gate(wgs_ref), gate(wgm_ref)
    merged = (ga * dot(a_ref[...].astype(BF16), wa_ref) + gs * dot(s_ref[...].astype(BF16), ws_ref)
              + gm * dot(m_ref[...].astype(BF16), wm_ref))
    o_ref[...] = merged.astype(BF16)


def _merge(h, a, s, mo, wg, wa, ws, wm, layer, tm, tn):
    m, d = h.shape
    nj = d // tn
    row = lambda w: pl.BlockSpec((tm, w), lambda i, j: (i, 0))
    gate = lambda off: _layer_spec(wg, layer, (tn, d), lambda i, j: (off + j, 0))
    lcol = lambda w: _layer_spec(w, layer, (w.shape[1], tn), lambda i, j: (0, j))
    return pl.pallas_call(
        _merge_kernel,
        grid=(m // tm, nj),
        in_specs=[row(d), row(a.shape[1]), row(s.shape[1]), row(mo.shape[1]),
                  gate(0), gate(nj), gate(2 * nj), lcol(wa), lcol(ws), lcol(wm)],
        out_specs=pl.BlockSpec((tm, tn), lambda i, j: (i, j)),
        out_shape=jax.ShapeDtypeStruct((m, d), BF16),
        compiler_params=_cparams(("arbitrary", "arbitrary")),
        name="merge",
    )(h, a, s, mo, wg, wg, wg, wa, ws, wm)


def _proj_res_kernel(x_ref, m_ref, w_ref, o_ref):
    o_ref[...] = x_ref[...] + jnp.dot(m_ref[...], w_ref[...], preferred_element_type=F32)


def _proj_res(x, merged, w, layer, tm):
    m, d = x.shape
    return pl.pallas_call(
        _proj_res_kernel,
        grid=(m // tm,),
        in_specs=[pl.BlockSpec((tm, d), lambda i: (i, 0)),
                  pl.BlockSpec((tm, d), lambda i: (i, 0)),
                  _layer_spec(w, layer, (d, d), lambda i: (0, 0))],
        out_specs=pl.BlockSpec((tm, d), lambda i: (i, 0)),
        out_shape=jax.ShapeDtypeStruct((m, d), F32),
        compiler_params=_cparams(("arbitrary",)),
        name="proj_res",
    )(x, merged, w)


def _ffn_kernel(x_ref, g_ref, w1_ref, w3_ref, w2_ref, o_ref, h_ref):
    f = pl.program_id(1)

    @pl.when(f == 0)
    def _():
        x = x_ref[...]
        r = lax.rsqrt(jnp.mean(x * x, axis=-1, keepdims=True) + EPS)
        h_ref[...] = ((x * r) * g_ref[...]).astype(BF16)
        o_ref[...] = x

    h = h_ref[...]
    a = jnp.dot(h, w1_ref[...], preferred_element_type=F32)
    b = jnp.dot(h, w3_ref[...], preferred_element_type=F32)
    act = (jax.nn.silu(a) * b).astype(BF16)
    o_ref[...] += jnp.dot(act, w2_ref[...], preferred_element_type=F32)


def _ffn(x, g, w1, w3, w2, layer, tm, tf):
    m, d = x.shape
    ff = w1.shape[2]
    return pl.pallas_call(
        _ffn_kernel,
        grid=(m // tm, ff // tf),
        in_specs=[pl.BlockSpec((tm, d), lambda i, f: (i, 0)),
                  _layer_spec(g, layer, (1, d), lambda i, f: (0, 0)),
                  _layer_spec(w1, layer, (d, tf), lambda i, f: (0, f)),
                  _layer_spec(w3, layer, (d, tf), lambda i, f: (0, f)),
                  _layer_spec(w2, layer, (tf, d), lambda i, f: (f, 0))],
        out_specs=pl.BlockSpec((tm, d), lambda i, f: (i, 0)),
        out_shape=jax.ShapeDtypeStruct((m, d), F32),
        scratch_shapes=[pltpu.VMEM((tm, d), BF16)],
        compiler_params=_cparams(("arbitrary", "arbitrary")),
        name="ffn",
    )(x, g, w1, w3, w2)


def kernel(x_prompt, x_sample, mem_prompt, cache_k, cache_v, cache_kidx, cache_mem_k, cache_mem_v, state_ssm_re, state_ssm_im, page_table, norm1_g, w_in, q_norm_g, k_norm_g, ssm_lam_re, ssm_lam_im, ssm_log_dt, ssm_b_re, ssm_b_im, ssm_c_re, ssm_c_im, ssm_d, w_glu, mem_norm_g, w_mem_kv, mq_norm_g, mk_norm_g, w_br_attn, w_br_ssm, w_br_mem, w_o, norm2_g, w_ff1, w_ff3, w_ff2):
    depth = w_in.shape[0]
    bp, tp, d = x_prompt.shape
    bs, ts, _ = x_sample.shape
    rp, rs = bp * tp, bs * ts
    m_tok = mem_prompt.shape[1]
    mw = MEM_HEADS * HEAD_DIM
    kvw = N_KV_HEADS * HEAD_DIM
    groups = ssm_lam_re.shape[1]
    tm_p, tm_s = _row_tile(rp), _row_tile(rs)

    x_p, x_s = x_prompt.reshape(rp, d), x_sample.reshape(rs, d)
    mem = mem_prompt.reshape(bp * m_tok, d)
    cmk = cache_mem_k.reshape(depth, bs, m_tok * MEM_HEADS, HEAD_DIM)
    cmv = cache_mem_v.reshape(depth, bs, m_tok * MEM_HEADS, HEAD_DIM)
    outs = [[] for _ in range(12)]
    nc_p, nc_s = tp // SSM_CHUNK, ts // SSM_CHUNK
    n_steps = max(1, int(math.log2(nc_p)))

    bf = lambda w: w.astype(BF16)
    w_mem_kv_b, w_glu_b, w_o_b = bf(w_mem_kv), bf(w_glu), bf(w_o)
    w_ba_b, w_bs_b, w_bm_b = bf(w_br_attn), bf(w_br_ssm), bf(w_br_mem)
    w_ff1_b, w_ff3_b, w_ff2_b = bf(w_ff1), bf(w_ff3), bf(w_ff2)
    mats = jax.vmap(functools.partial(_ssm_matrices, n_steps=n_steps))(
        ssm_lam_re, ssm_lam_im, ssm_log_dt, ssm_b_re, ssm_b_im, ssm_c_re, ssm_c_im, ssm_d)
    s0_prompt = jnp.zeros((depth, groups // 2, bp, SSM_PAIR_W), F32)
    s0_sample = jax.vmap(_state_to_pairs)(state_ssm_re, state_ssm_im)
    flags, gains = jax.vmap(_epilogue_rows)(q_norm_g, k_norm_g, mq_norm_g)
    mflag = jnp.concatenate([jnp.ones((mw,), F32), jnp.zeros((mw,), F32)])[None]
    mgains = jnp.concatenate([jnp.tile(mk_norm_g, (1, MEM_HEADS)), jnp.ones((depth, mw), F32)], axis=1)[:, None]
    g1, g2, gm = norm1_g[:, None], norm2_g[:, None], mem_norm_g[:, None]
    wp, wg, uw = _pack_w_in(w_in)

    def dense_tail(x, h, a, y, m, l, tm):
        s = _glu(y, w_glu_b, l, tm)
        merged = _merge(h, a, s, m, wg, w_ba_b, w_bs_b, w_bm_b, l, tm, min(512, d))
        x = _proj_res(x, merged, w_o_b, l, min(tm, 512))
        return _ffn(x, g2, w_ff1_b, w_ff3_b, w_ff2_b, l, tm, 512)

    for l in range(depth):
        z_p, h_p = _norm_matmul(x_p, g1[l], wp, flags[l], gains[l], layer=l, tm=tm_p, tn=1024, emit_h=True,
                                w_rows_out=True)
        z_s, h_s = _norm_matmul(x_s, g1[l], wp, flags[l], gains[l], layer=l, tm=tm_s, tn=1024, emit_h=True,
                                w_rows_out=True)

        mkv = _norm_matmul(mem, gm[l], w_mem_kv_b, mflag, mgains[l], layer=l, tm=bp * m_tok, tn=2 * mw,
                           emit_h=False, w_rows_out=False)
        mk_p = mkv[:, :mw].reshape(bp, m_tok, MEM_HEADS, HEAD_DIM)
        mv_p = mkv[:, mw:].reshape(bp, m_tok, MEM_HEADS, HEAD_DIM)

        a_p = _dsa_prompt(z_p, bp, tp)
        a_s = _dsa_sample(z_s, 0, bs, ts, l, cache_k, cache_v, cache_kidx, page_table)

        y_p, sf_p = _ssm(z_p, 0, rp, mats, s0_prompt, l, bp, nc_p)
        y_s, sf_s = _ssm(z_s, 0, rs, mats, s0_sample, l, bs, nc_s)

        m_p = _mem_attn(z_p, 0, bp, tp, 512, mk_p.reshape(1, bp, m_tok * MEM_HEADS, HEAD_DIM),
                        mv_p.reshape(1, bp, m_tok * MEM_HEADS, HEAD_DIM), 0)
        m_s = _mem_attn(z_s, 0, bs, ts, ts, cmk, cmv, l)

        x_p = dense_tail(x_p, h_p, a_p, y_p, m_p, l, tm_p)
        x_s = dense_tail(x_s, h_s, a_s, y_s, m_s, l, tm_s)

        srp, sip = _state_from_pairs(sf_p)
        srs, sis = _state_from_pairs(sf_s)
        new = [
            z_p[:, C_K:C_K + kvw].reshape(bp, tp, N_KV_HEADS, HEAD_DIM),
            z_p[:, C_V:C_V + kvw].reshape(bp, tp, N_KV_HEADS, HEAD_DIM),
            z_p[:, C_KI:C_KI + IDX_DIM].reshape(bp, tp, IDX_DIM),
            mk_p, mv_p, srp, sip,
            z_s[:, C_K:C_K + kvw].reshape(bs, ts, N_KV_HEADS, HEAD_DIM),
            z_s[:, C_V:C_V + kvw].reshape(bs, ts, N_KV_HEADS, HEAD_DIM),
            z_s[:, C_KI:C_KI + IDX_DIM].reshape(bs, ts, IDX_DIM),
            srs, sis,
        ]
        for o, v in zip(outs, new):
            o.append(v)

    st = lambda xs: jnp.stack(xs, axis=0)
    return (x_p.reshape(bp, tp, d), x_s.reshape(bs, ts, d), *[st(o) for o in outs])
```
